```python
import jax, jax.numpy as jnp
from jax import lax
import numpy as np

D_MODEL = 4096
BATCH = 2
SEQ = 4096
DEPTH = 2
DEC_BATCH = 8
DEC_SEQ = 32
PAST_LEN = 4096

CHUNK = 64
MIX_WIDTH = D_MODEL
W_A = MIX_WIDTH // 2
W_B = MIX_WIDTH // 4
W_C = MIX_WIDTH // 4
DV_A = 128
H_A = W_A // DV_A
DK_A = DV_A // 2
QK_A = H_A * DK_A
GLA_RANK = 16
GLA_TAU = 16.0
GMLP_CHUNK = 128
H_B = 8
DH_B = W_B // H_B
H_C = 8
DH_C = W_C // H_C
CONV_W = 4
LRU_C = 8.0
D_FF = 4 * D_MODEL
EPS = 1e-6
PROJ_SIZES = (QK_A, QK_A, W_A, W_A, GLA_RANK, W_B, W_B, W_C, W_C)
PROJ_WIDTH = sum(PROJ_SIZES)

kernel_name = 'hybrid_stream_gla_gmlp_rglru'


def rmsnorm(x, g):
    xf = x.astype(jnp.float32)
    y = xf * lax.rsqrt(jnp.mean(jnp.square(xf), axis=-1, keepdims=True) + EPS)
    return (y * g.astype(jnp.float32)).astype(x.dtype)


def gla_chunked(q, k, v, log_a, s0):
    bsz, t, h, dk = q.shape
    c = min(CHUNK, t)
    n = t // c
    def to_chunks(z):
        return jnp.swapaxes(z.astype(jnp.float32).reshape(bsz, n, c, *z.shape[2:]), 0, 1)
    qs, ks, vs, gs = to_chunks(q), to_chunks(k), to_chunks(v), to_chunks(log_a)
    mask = jnp.tril(jnp.ones((c, c), bool))
    def step(s, inp):
        qc, kc, vc, gc = inp
        b = jnp.cumsum(gc, axis=1)
        b_end = b[:, -1]
        q_t = qc * jnp.exp(b)
        k_t = kc * jnp.exp(-b)
        k_e = kc * jnp.exp(b_end[:, None] - b)
        att = jnp.where(mask, jnp.einsum('bihk,bjhk->bhij', q_t, k_t), 0.0)
        o = jnp.einsum('bhij,bjhv->bihv', att, vc) + jnp.einsum('bihk,bhkv->bihv', q_t, s)
        s = jnp.exp(b_end)[..., None] * s + jnp.einsum('bjhk,bjhv->bhkv', k_e, vc)
        return s, o
    s, o = lax.scan(step, s0.astype(jnp.float32), (qs, ks, vs, gs))
    return jnp.swapaxes(o, 0, 1).reshape(bsz, t, h, -1), s


def spatial_gate(u, v, w_sp, b_sp):
    bsz, t, _ = v.shape
    c = min(GMLP_CHUNK, t)
    n = t // c
    vh = v.reshape(bsz, n, c, H_B, DH_B)
    w = jnp.where(jnp.tril(jnp.ones((c, c), bool)), w_sp[:, :c, :c], 0.0)
    mixed = jnp.einsum('hij,bnjhd->bnihd', w, vh) + b_sp[:, :c].T[None, None, :, :, None]
    return u * mixed.reshape(bsz, t, W_B).astype(u.dtype)


def causal_conv(x, buf, w, b):
    t = x.shape[1]
    xp = jnp.concatenate([buf.astype(x.dtype), x], axis=1)
    y = b + xp[:, 0:t] * w[0]
    for j in range(1, CONV_W):
        y = y + xp[:, j:j + t] * w[j]
    return y, xp[:, -(CONV_W - 1):]


def rg_lru(xc, h0, w_r, b_r, w_i, b_i, lam):
    f32 = jnp.float32
    xf = xc.astype(f32)
    bsz, t, _ = xf.shape
    xh = xf.reshape(bsz, t, H_C, DH_C)
    r = jax.nn.sigmoid(jnp.einsum('bthi,hij->bthj', xh, w_r.astype(f32)).reshape(bsz, t, W_C) + b_r.astype(f32))
    ig = jax.nn.sigmoid(jnp.einsum('bthi,hij->bthj', xh, w_i.astype(f32)).reshape(bsz, t, W_C) + b_i.astype(f32))
    log_a = -LRU_C * r * jax.nn.softplus(-lam.astype(f32))
    a = jnp.exp(log_a)
    bt = jnp.sqrt(-jnp.expm1(2.0 * log_a)) * (ig * xf)
    bt = bt.at[:, 0].add(a[:, 0] * h0.astype(f32))
    def combine(e1, e2):
        a1, b1 = e1
        a2, b2 = e2
        return a1 * a2, a2 * b1 + b2
    _, h = lax.associative_scan(combine, (a, bt), axis=1)
    return h, h[:, -1]


def trunk_layer(x, s_gla, s_conv, s_lru, ln1, w_in, w_alpha2, b_alpha, g_onorm, w_spatial, b_spatial,
                conv_w, conv_b, w_rgate, b_rgate, w_igate, b_igate, lru_lambda, w_out, ln2, w_up, w_down):
    bsz, t, _ = x.shape
    hproj = rmsnorm(x, ln1) @ w_in
    splits = np.cumsum(PROJ_SIZES)[:-1].tolist()
    q, k, v, g, a_lr, u_b, v_b, x_c, gate_c = jnp.split(hproj, splits, axis=-1)
    q = q.reshape(bsz, t, H_A, DK_A) * (DK_A ** -0.5)
    k = k.reshape(bsz, t, H_A, DK_A)
    v = v.reshape(bsz, t, H_A, DV_A)
    log_alpha = jax.nn.log_sigmoid((a_lr @ w_alpha2 + b_alpha).astype(jnp.float32)) / GLA_TAU
    o_a, s_gla_new = gla_chunked(q, k, v, log_alpha.reshape(bsz, t, H_A, DK_A), s_gla)
    o_a = rmsnorm(o_a, g_onorm).reshape(bsz, t, W_A).astype(x.dtype) * jax.nn.silu(g)
    u_b = jax.nn.gelu(u_b)
    v_b = jax.nn.gelu(v_b)
    o_b = spatial_gate(u_b, v_b, w_spatial, b_spatial)
    xcv, s_conv_new = causal_conv(x_c, s_conv, conv_w, conv_b)
    hseq, s_lru_new = rg_lru(xcv, s_lru, w_rgate, b_rgate, w_igate, b_igate, lru_lambda)
    o_c = hseq.astype(x.dtype) * jax.nn.gelu(gate_c)
    x = x + jnp.concatenate([o_a, o_b, o_c], axis=-1) @ w_out
    x = x + jnp.square(jax.nn.relu(rmsnorm(x, ln2) @ w_up)) @ w_down
    return x, s_gla_new, s_conv_new, s_lru_new, v_b


def setup_inputs(seed: int = 0) -> dict:
    key = jax.random.key(seed)
    ks = jax.random.split(key, 24)
    f32 = jnp.float32
    nrm = lambda kk, shape: jax.random.normal(kk, shape, f32)
    d = DEPTH
    a0 = jax.random.uniform(ks[18], (d, W_C), f32, 0.9, 0.999)
    a_root = a0 ** (1.0 / LRU_C)
    return {
        'x_prompt': nrm(ks[0], (BATCH, SEQ, D_MODEL)),
        'x_sample': nrm(ks[1], (DEC_BATCH, DEC_SEQ, D_MODEL)),
        'state_gla': 0.5 * nrm(ks[2], (d, DEC_BATCH, H_A, DK_A, DV_A)),
        'state_conv': nrm(ks[3], (d, DEC_BATCH, CONV_W - 1, W_C)),
        'state_lru': 0.5 * nrm(ks[4], (d, DEC_BATCH, W_C)),
        'ln1': 1.0 + 0.02 * nrm(ks[5], (d, D_MODEL)),
        'w_in': nrm(ks[6], (d, D_MODEL, PROJ_WIDTH)) * D_MODEL ** -0.5,
        'w_alpha2': nrm(ks[7], (d, GLA_RANK, QK_A)) * GLA_RANK ** -0.5,
        'b_alpha': 0.1 * nrm(ks[8], (d, QK_A)),
        'g_onorm': 1.0 + 0.02 * nrm(ks[9], (d, DV_A)),
        'w_spatial': nrm(ks[10], (d, H_B, GMLP_CHUNK, GMLP_CHUNK)) * GMLP_CHUNK ** -0.5,
        'b_spatial': 1.0 + 0.02 * nrm(ks[11], (d, H_B, GMLP_CHUNK)),
        'conv_w': nrm(ks[12], (d, CONV_W, W_C)) * CONV_W ** -0.5,
        'conv_b': 0.02 * nrm(ks[13], (d, W_C)),
        'w_rgate': nrm(ks[14], (d, H_C, DH_C, DH_C)) * DH_C ** -0.5,
        'b_rgate': 0.02 * nrm(ks[15], (d, W_C)),
        'w_igate': nrm(ks[16], (d, H_C, DH_C, DH_C)) * DH_C ** -0.5,
        'b_igate': 0.02 * nrm(ks[17], (d, W_C)),
        'lru_lambda': jnp.log(a_root) - jnp.log1p(-a_root),
        'w_out': nrm(ks[19], (d, MIX_WIDTH, D_MODEL)) * MIX_WIDTH ** -0.5,
        'ln2': 1.0 + 0.02 * nrm(ks[20], (d, D_MODEL)),
        'w_up': nrm(ks[21], (d, D_MODEL, D_FF)) * D_MODEL ** -0.5,
        'w_down': nrm(ks[22], (d, D_FF, D_MODEL)) * D_FF ** -0.5,
        'ln_final': 1.0 + 0.02 * nrm(ks[23], (D_MODEL,)),
    }


def reference(x_prompt, x_sample, state_gla, state_conv, state_lru, ln1, w_in, w_alpha2, b_alpha, g_onorm,
              w_spatial, b_spatial, conv_w, conv_b, w_rgate, b_rgate, w_igate, b_igate, lru_lambda, w_out,
              ln2, w_up, w_down, ln_final):
    bp = x_prompt.shape[0]
    zero_gla = jnp.zeros((bp, H_A, DK_A, DV_A), jnp.float32)
    zero_conv = jnp.zeros((bp, CONV_W - 1, W_C), x_prompt.dtype)
    zero_lru = jnp.zeros((bp, W_C), jnp.float32)
    xp, xs = x_prompt, x_sample
    gla_p, conv_p, lru_p, gla_s, conv_s, lru_s, v_s = [], [], [], [], [], [], []
    for l in range(DEPTH):
        w = (ln1[l], w_in[l], w_alpha2[l], b_alpha[l], g_onorm[l], w_spatial[l], b_spatial[l], conv_w[l],
             conv_b[l], w_rgate[l], b_rgate[l], w_igate[l], b_igate[l], lru_lambda[l], w_out[l], ln2[l],
             w_up[l], w_down[l])
        xp, g1, c1, h1, _ = trunk_layer(xp, zero_gla, zero_conv, zero_lru, *w)
        xs, g2, c2, h2, v2 = trunk_layer(xs, state_gla[l], state_conv[l], state_lru[l], *w)
        gla_p.append(g1); conv_p.append(c1); lru_p.append(h1)
        gla_s.append(g2); conv_s.append(c2); lru_s.append(h2); v_s.append(v2)
    y_prompt = rmsnorm(xp, ln_final)
    y_sample = rmsnorm(xs, ln_final)
    return (y_prompt, y_sample, jnp.stack(gla_p), jnp.stack(conv_p), jnp.stack(lru_p),
            jnp.stack(gla_s), jnp.stack(conv_s), jnp.stack(lru_s), jnp.stack(v_s))
```

```python
import functools

import jax
import jax.numpy as jnp
from jax import lax
from jax.experimental import pallas as pl
from jax.experimental.pallas import tpu as pltpu

F32 = jnp.float32
BF16 = jnp.bfloat16

DV_A = 128
DK_A = 64
GLA_RANK = 16
GLA_TAU = 16.0
CHUNK = 64
GMLP_CHUNK = 128
H_B = 8
H_C = 8
CONV_W = 4
LRU_C = 8.0
EPS = 1e-6

LANES = 128
SUBLANES = 8
V7X_VMEM_BYTES = 64 * 1024 * 1024
VMEM_CAP = V7X_VMEM_BYTES * 7 // 8


def _vmem_limit(*block_bytes):
    need = 4 * sum(block_bytes)
    return int(min(max(need, 16 * 1024 * 1024), VMEM_CAP))


def _nbytes(shape, dtype):
    n = 1
    for s in shape:
        n *= s
    return n * jnp.dtype(dtype).itemsize


def _rmsnorm_body(x_ref, g_ref, o_ref):
    x = x_ref[...]
    ms = jnp.mean(jnp.square(x), axis=-1, keepdims=True)
    o_ref[...] = (x * lax.rsqrt(ms + EPS) * g_ref[...]).astype(o_ref.dtype)


def _rmsnorm_proj_body(x_ref, g_ref, w_ref, o_ref, p_ref):
    x = x_ref[...]
    ms = jnp.mean(jnp.square(x), axis=-1, keepdims=True)
    y = (x * lax.rsqrt(ms + EPS) * g_ref[...]).astype(BF16)
    o_ref[...] = y
    p_ref[...] = jnp.dot(y, w_ref[...], preferred_element_type=F32)


def _rmsnorm(x, g, out_dtype, w_proj=None):
    t, d = x.shape
    tb = min(t, 256)
    grid = (t // tb,)
    x_spec = pl.BlockSpec((tb, d), lambda i: (i, 0))
    g_spec = pl.BlockSpec((1, d), lambda i: (0, 0))
    params = pltpu.CompilerParams(
        dimension_semantics=("parallel",),
        vmem_limit_bytes=_vmem_limit(_nbytes((tb, d), F32), _nbytes((tb, d), out_dtype)))
    if w_proj is None:
        return pl.pallas_call(
            _rmsnorm_body, grid=grid, in_specs=[x_spec, g_spec], out_specs=x_spec,
            out_shape=jax.ShapeDtypeStruct((t, d), out_dtype), compiler_params=params,
            name="rmsnorm")(x, g.reshape(1, d))
    n = w_proj.shape[1]
    return pl.pallas_call(
        _rmsnorm_proj_body, grid=grid,
        in_specs=[x_spec, g_spec, pl.BlockSpec((d, n), lambda i: (0, 0))],
        out_specs=[x_spec, pl.BlockSpec((tb, n), lambda i: (i, 0))],
        out_shape=[jax.ShapeDtypeStruct((t, d), BF16), jax.ShapeDtypeStruct((t, n), F32)],
        compiler_params=params, name="rmsnorm_proj")(x, g.reshape(1, d), w_proj)


def _matmul_body(*refs, k_bounds, has_res, act, nk):
    n_lhs = len(k_bounds)
    lhs_refs, w_ref = refs[:n_lhs], refs[n_lhs]
    res_ref = refs[n_lhs + 1] if has_res else None
    o_ref = refs[-1]
    acc = None
    for a_ref, (k0, k1) in zip(lhs_refs, k_bounds):
        part = jnp.dot(a_ref[...], w_ref[k0:k1, :], preferred_element_type=F32)
        acc = part if acc is None else acc + part
    if nk == 1:
        if has_res:
            acc = res_ref[...] + acc
        if act == "relu2":
            acc = jnp.square(jnp.maximum(acc, 0.0))
        o_ref[...] = acc.astype(o_ref.dtype)
    else:
        k = pl.program_id(2)

        @pl.when(k == 0)
        def _():
            o_ref[...] = (res_ref[...] + acc) if has_res else acc

        @pl.when(k > 0)
        def _():
            o_ref[...] += acc


def _matmul(lhs_list, w, *, bm, bn, bk=None, residual=None, act=None, out_dtype=F32, name="matmul"):
    m = lhs_list[0].shape[0]
    k_total, n = w.shape
    bm = min(bm, m)
    widths = [a.shape[1] for a in lhs_list]
    assert sum(widths) == k_total and m % bm == 0 and n % bn == 0
    if bk is None:
        bk = k_total
    nk = k_total // bk
    assert nk * bk == k_total and (nk == 1 or (len(lhs_list) == 1 and out_dtype == F32 and act is None))
    if nk == 1:
        bounds, k0 = [], 0
        for wd in widths:
            bounds.append((k0, k0 + wd))
            k0 += wd
        lhs_specs = [pl.BlockSpec((bm, wd), lambda i, j, k: (i, 0)) for wd in widths]
    else:
        bounds = [(0, bk)]
        lhs_specs = [pl.BlockSpec((bm, bk), lambda i, j, k: (i, k))]
    in_specs = lhs_specs + [pl.BlockSpec((bk, bn), lambda i, j, k: (k, j))]
    args = list(lhs_list) + [w]
    io_spec = pl.BlockSpec((bm, bn), lambda i, j, k: (i, j))
    if residual is not None:
        in_specs.append(io_spec)
        args.append(residual)
    body = functools.partial(_matmul_body, k_bounds=tuple(bounds), has_res=residual is not None,
                             act=act, nk=nk)
    limit = _vmem_limit(_nbytes((bm, bk), BF16), _nbytes((bk, bn), BF16), _nbytes((bm, bn), F32),
                        _nbytes((bm, bn), F32) if residual is not None else 0)
    return pl.pallas_call(
        body, grid=(m // bm, n // bn, nk), in_specs=in_specs, out_specs=io_spec,
        out_shape=jax.ShapeDtypeStruct((m, n), out_dtype),
        compiler_params=pltpu.CompilerParams(
            dimension_semantics=("parallel", "parallel", "arbitrary"), vmem_limit_bytes=limit),
        name=name)(*args)


def _shift_rows(x, s, fill):
    row = lax.broadcasted_iota(jnp.int32, x.shape, 0)
    return jnp.where(row >= s, pltpu.roll(x, s, 0), fill)


def _softplus(x):
    return jnp.maximum(x, 0.0) + jnp.log1p(jnp.exp(-jnp.abs(x)))


def _gla_body(q_ref, k_ref, v_ref, g_ref, alr_ref, wa_ref, ba_ref, gn_ref, s0_ref,
              o_ref, s_out_ref, st_ref, *, chunk, n_chunks):
    n = pl.program_id(2)

    @pl.when(n == 0)
    def _():
        st_ref[...] = s0_ref[0, 0].T

    c = chunk
    lane = lax.broadcasted_iota(jnp.int32, (c, 2 * DK_A), 1)
    head0 = lane < DK_A
    row2 = lax.broadcasted_iota(jnp.int32, (2 * c, 2 * c), 0)
    col2 = lax.broadcasted_iota(jnp.int32, (2 * c, 2 * c), 1)
    causal = ((row2 >= c) == (col2 >= c)) & (row2 >= col2)
    nt = (((1,), (1,)), ((), ()))
    tn = (((0,), (0,)), ((), ()))

    def stack_heads(z):
        return jnp.concatenate([jnp.where(head0, z, 0.0), jnp.where(head0, 0.0, z)], axis=0).astype(BF16)

    for ci in range(n_chunks):
        rows = pl.ds(ci * c, c)
        z = jnp.dot(alr_ref[rows, :].astype(BF16), wa_ref[...], preferred_element_type=F32) + ba_ref[...]
        log_alpha = -_softplus(-z) / GLA_TAU
        b = log_alpha
        s = 1
        while s < c:
            b = b + _shift_rows(b, s, 0.0)
            s *= 2
        b_end = b[c - 1:c, :]
        q = q_ref[rows, :] * (DK_A ** -0.5)
        k = k_ref[rows, :]
        q_t = stack_heads(q * jnp.exp(b))
        k_t = stack_heads(k * jnp.exp(-b))
        k_e = stack_heads(k * jnp.exp(b_end - b))
        v = v_ref[rows, :]
        v_s = jnp.concatenate([v[:, :DV_A], v[:, DV_A:]], axis=0).astype(BF16)
        att = lax.dot_general(q_t, k_t, nt, preferred_element_type=F32)
        att = jnp.where(causal, att, 0.0).astype(BF16)
        st = st_ref[...]
        o = (jnp.dot(att, v_s, preferred_element_type=F32)
             + lax.dot_general(q_t, st.astype(BF16), nt, preferred_element_type=F32))
        st_ref[...] = jnp.exp(b_end) * st + lax.dot_general(v_s, k_e, tn, preferred_element_type=F32)
        ms = jnp.mean(jnp.square(o), axis=-1, keepdims=True)
        o = (o * lax.rsqrt(ms + EPS) * gn_ref[...])
        gate = g_ref[rows, :]
        gate = gate * jax.nn.sigmoid(gate)
        o_ref[rows, :DV_A] = (o[:c] * gate[:, :DV_A]).astype(o_ref.dtype)
        o_ref[rows, DV_A:] = (o[c:] * gate[:, DV_A:]).astype(o_ref.dtype)

    @pl.when(n == pl.num_programs(2) - 1)
    def _():
        s_out_ref[0, 0] = st_ref[...].T


def _gla(hproj, alr, wa, ba, gn, s0, *, n_seq, seq_len, col_q, col_k, col_v, col_g):
    t_total = hproj.shape[0]
    h_a = s0.shape[1]
    pairs = h_a // 2
    chunk = min(CHUNK, seq_len)
    n_chunks = min(4, seq_len // chunk)
    tb = chunk * n_chunks
    nb = seq_len // tb
    qw, vw = 2 * DK_A, 2 * DV_A
    cq, ck, cv, cg = col_q // qw, col_k // qw, col_v // vw, col_g // vw
    row = lambda s, p, n: s * nb + n
    in_specs = [
        pl.BlockSpec((tb, qw), lambda s, p, n: (row(s, p, n), cq + p)),
        pl.BlockSpec((tb, qw), lambda s, p, n: (row(s, p, n), ck + p)),
        pl.BlockSpec((tb, vw), lambda s, p, n: (row(s, p, n), cv + p)),
        pl.BlockSpec((tb, vw), lambda s, p, n: (row(s, p, n), cg + p)),
        pl.BlockSpec((tb, LANES), lambda s, p, n: (row(s, p, n), 0)),
        pl.BlockSpec((LANES, qw), lambda s, p, n: (0, p)),
        pl.BlockSpec((1, qw), lambda s, p, n: (0, p)),
        pl.BlockSpec((1, DV_A), lambda s, p, n: (0, 0)),
        pl.BlockSpec((1, 1, qw, DV_A), lambda s, p, n: (s, p, 0, 0)),
    ]
    out_specs = [
        pl.BlockSpec((tb, vw), lambda s, p, n: (row(s, p, n), p)),
        pl.BlockSpec((1, 1, qw, DV_A), lambda s, p, n: (s, p, 0, 0)),
    ]
    o_a, s_new = pl.pallas_call(
        functools.partial(_gla_body, chunk=chunk, n_chunks=n_chunks),
        grid=(n_seq, pairs, nb), in_specs=in_specs, out_specs=out_specs,
        out_shape=[jax.ShapeDtypeStruct((t_total, h_a * DV_A), BF16),
                   jax.ShapeDtypeStruct((n_seq, pairs, qw, DV_A), F32)],
        scratch_shapes=[pltpu.VMEM((DV_A, qw), F32)],
        compiler_params=pltpu.CompilerParams(
            dimension_semantics=("parallel", "parallel", "arbitrary"),
            vmem_limit_bytes=_vmem_limit(_nbytes((tb, 2 * qw + 2 * vw + LANES), F32), _nbytes((tb, vw), BF16))),
        name="gla")(hproj, hproj, hproj, hproj, alr, wa, ba, gn, s0.reshape(n_seq, pairs, qw, DV_A))
    return o_a, s_new.reshape(n_seq, h_a, DK_A, DV_A)


def _gmlp_body(u_ref, v_ref, w_ref, b_ref, o_ref, *vg_refs, c, dh):
    vg = jax.nn.gelu(v_ref[...])
    if vg_refs:
        vg_refs[0][...] = vg
    ug = jax.nn.gelu(u_ref[...])
    tril = (lax.broadcasted_iota(jnp.int32, (c, c), 0) >= lax.broadcasted_iota(jnp.int32, (c, c), 1))
    for h in range(w_ref.shape[0]):
        cols = pl.ds(h * dh, dh)
        w = jnp.where(tril, w_ref[h, :c, :c], 0.0).astype(BF16)
        mixed = jnp.dot(w, vg[:, h * dh:(h + 1) * dh].astype(BF16), preferred_element_type=F32) + b_ref[:c, cols]
        o_ref[:, cols] = (ug[:, h * dh:(h + 1) * dh] * mixed).astype(o_ref.dtype)


def _gmlp(hproj, w_sp, b_full, *, seq_len, col_u, col_v, want_v):
    t_total = hproj.shape[0]
    n_h, gc, _ = w_sp.shape
    w_b = b_full.shape[1]
    dh = w_b // n_h
    c = min(gc, seq_len)
    blk = lambda col: pl.BlockSpec((c, w_b), lambda r: (r, col // w_b))
    out_specs = [pl.BlockSpec((c, w_b), lambda r: (r, 0))]
    out_shape = [jax.ShapeDtypeStruct((t_total, w_b), BF16)]
    if want_v:
        out_specs.append(pl.BlockSpec((c, w_b), lambda r: (r, 0)))
        out_shape.append(jax.ShapeDtypeStruct((t_total, w_b), F32))
    outs = pl.pallas_call(
        functools.partial(_gmlp_body, c=c, dh=dh), grid=(t_total // c,),
        in_specs=[blk(col_u), blk(col_v),
                  pl.BlockSpec((n_h, gc, gc), lambda r: (0, 0, 0)),
                  pl.BlockSpec((gc, w_b), lambda r: (0, 0))],
        out_specs=out_specs, out_shape=out_shape,
        compiler_params=pltpu.CompilerParams(
            dimension_semantics=("parallel",),
            vmem_limit_bytes=_vmem_limit(4 * _nbytes((c, w_b), F32), _nbytes((n_h, gc, gc), F32))),
        name="gmlp")(hproj, hproj, w_sp, b_full)
    return (outs[0], outs[1]) if want_v else (outs[0], None)


def _lru_body(x_ref, gate_ref, cw_ref, cb_ref, wr_ref, br_ref, wi_ref, bi_ref, lam_ref, h0_ref, tail0_ref,
              o_ref, h_ref, tail_ref, *, dh):
    n = pl.program_id(1)

    @pl.when(n == 0)
    def _():
        h_ref[...] = h0_ref[...]
        tail_ref[...] = tail0_ref[...]

    tb = x_ref.shape[0]
    row8 = lax.broadcasted_iota(jnp.int32, (SUBLANES, dh), 0)
    for hb in range(x_ref.shape[1] // dh):
        cols = pl.ds(hb * dh, dh)
        x = x_ref[:, cols]
        tail = tail_ref[0, :, cols]

        def delayed(j):
            rolled = pltpu.roll(x, j, 0)
            head = jnp.where(row8 >= j, rolled[:SUBLANES], pltpu.roll(tail, j, 0))
            return jnp.concatenate([head, rolled[SUBLANES:]], axis=0)

        y = cb_ref[:, cols] + delayed(CONV_W - 1) * cw_ref[0:1, cols]
        for j in range(1, CONV_W - 1):
            y = y + delayed(CONV_W - 1 - j) * cw_ref[j:j + 1, cols]
        y = y + x * cw_ref[CONV_W - 1:CONV_W, cols]
        tail_ref[0, :, cols] = x[tb - SUBLANES:]

        yb = y.astype(BF16)
        r = jax.nn.sigmoid(jnp.dot(yb, wr_ref[hb], preferred_element_type=F32) + br_ref[:, cols])
        ig = jax.nn.sigmoid(jnp.dot(yb, wi_ref[hb], preferred_element_type=F32) + bi_ref[:, cols])
        log_a = -LRU_C * r * _softplus(-lam_ref[:, cols])
        a = jnp.exp(log_a)
        bt = jnp.sqrt(-jnp.tanh(log_a) * (a * a + 1.0)) * (ig * y)
        s = 1
        while s < tb:
            bt = a * _shift_rows(bt, s, 0.0) + bt
            a = a * _shift_rows(a, s, 1.0)
            s *= 2
        hseq = bt + a * h_ref[0, :, cols]
        h_ref[0, :, cols] = hseq[tb - 1:]
        o_ref[:, cols] = (hseq * jax.nn.gelu(gate_ref[:, cols])).astype(o_ref.dtype)


def _lru(hproj, cw, cb, wr, br, wi, bi, lam, h0, tail0, *, n_seq, seq_len, col_x, col_gate):
    t_total = hproj.shape[0]
    w_c = cw.shape[1]
    n_h, dh, _ = wr.shape
    tb = min(seq_len, 256)
    nb = seq_len // tb
    vec = lambda rows: pl.BlockSpec((rows, w_c), lambda s, n: (0, 0))
    wspec = pl.BlockSpec((n_h, dh, dh), lambda s, n: (0, 0, 0))
    hspec = pl.BlockSpec((1, 1, w_c), lambda s, n: (s, 0, 0))
    tspec = pl.BlockSpec((1, SUBLANES, w_c), lambda s, n: (s, 0, 0))
    o_c, h_new, tail = pl.pallas_call(
        functools.partial(_lru_body, dh=dh), grid=(n_seq, nb),
        in_specs=[pl.BlockSpec((tb, w_c), lambda s, n: (s * nb + n, col_x // w_c)),
                  pl.BlockSpec((tb, w_c), lambda s, n: (s * nb + n, col_gate // w_c)),
                  vec(CONV_W), vec(1), wspec, vec(1), wspec, vec(1), vec(1), hspec, tspec],
        out_specs=[pl.BlockSpec((tb, w_c), lambda s, n: (s * nb + n, 0)), hspec, tspec],
        out_shape=[jax.ShapeDtypeStruct((t_total, w_c), BF16),
                   jax.ShapeDtypeStruct((n_seq, 1, w_c), F32),
                   jax.ShapeDtypeStruct((n_seq, SUBLANES, w_c), F32)],
        compiler_params=pltpu.CompilerParams(
            dimension_semantics=("parallel", "arbitrary"),
            vmem_limit_bytes=_vmem_limit(4 * _nbytes((tb, w_c), F32))),
        name="conv_rglru")(hproj, hproj, cw, cb.reshape(1, w_c), wr, br.reshape(1, w_c), wi,
                           bi.reshape(1, w_c), lam.reshape(1, w_c), h0.reshape(n_seq, 1, w_c), tail0)
    return o_c, h_new.reshape(n_seq, w_c), tail


def _layer(x, s_gla, s_conv, s_lru, w, *, n_seq, seq_len, want_v):
    d = x.shape[1]
    w_a, w_b, w_c = d // 2, d // 4, d // 4
    qk_a = (w_a // DV_A) * DK_A
    col_q, col_k, col_v, col_g = 0, qk_a, 2 * qk_a, 2 * qk_a + w_a
    col_u = col_g + w_a
    col_vb, col_x, col_gate = col_u + w_b, col_u + 2 * w_b, col_u + 2 * w_b + w_c
    bm = 1024

    xn, alr = _rmsnorm(x, w["ln1"], BF16, w_proj=w["w_alr"])
    hproj = _matmul([xn], w["w_in"], bm=bm, bn=1024, name="in_proj")

    o_a, s_gla_new = _gla(hproj, alr, w["w_alpha2"], w["b_alpha"], w["g_onorm"], s_gla, n_seq=n_seq,
                          seq_len=seq_len, col_q=col_q, col_k=col_k, col_v=col_v, col_g=col_g)
    o_b, vg = _gmlp(hproj, w["w_spatial"], w["b_spatial"], seq_len=seq_len, col_u=col_u, col_v=col_vb,
                    want_v=want_v)
    tail0 = jnp.pad(s_conv, ((0, 0), (SUBLANES - (CONV_W - 1), 0), (0, 0)))
    o_c, s_lru_new, tail = _lru(hproj, w["conv_w"], w["conv_b"], w["w_rgate"], w["b_rgate"], w["w_igate"],
                                w["b_igate"], w["lru_lambda"], s_lru, tail0, n_seq=n_seq, seq_len=seq_len,
                                col_x=col_x, col_gate=col_gate)
    s_conv_new = tail[:, SUBLANES - (CONV_W - 1):]

    x = _matmul([o_a, o_b, o_c], w["w_out"], bm=bm, bn=1024, residual=x, name="out_proj")
    xn2 = _rmsnorm(x, w["ln2"], BF16)
    hid = _matmul([xn2], w["w_up"], bm=bm, bn=1024, act="relu2", out_dtype=BF16, name="mlp_up")
    x = _matmul([hid], w["w_down"], bm=bm, bn=1024, bk=2048, residual=x, name="mlp_down")
    return x, s_gla_new, s_conv_new, s_lru_new, vg


def _prep_layer_weights(l, ln1, w_in, w_alpha2, b_alpha, g_onorm, w_spatial, b_spatial, conv_w, conv_b,
                        w_rgate, b_rgate, w_igate, b_igate, lru_lambda, w_out, ln2, w_up, w_down):
    d = w_in.shape[1]
    w_a, w_b = d // 2, d // 4
    qk_a = (w_a // DV_A) * DK_A
    c_alr = 2 * qk_a + 2 * w_a
    w_in_l = w_in[l]
    w_main = jnp.concatenate([w_in_l[:, :c_alr], w_in_l[:, c_alr + GLA_RANK:]], axis=1).astype(BF16)
    w_alr = jnp.pad(w_in_l[:, c_alr:c_alr + GLA_RANK], ((0, 0), (0, LANES - GLA_RANK))).astype(BF16)
    w_alpha2_p = jnp.pad(w_alpha2[l], ((0, LANES - GLA_RANK), (0, 0))).astype(BF16)
    dh_b = w_b // H_B
    b_full = jnp.repeat(b_spatial[l].T, dh_b, axis=1)
    return dict(
        ln1=ln1[l], w_in=w_main, w_alr=w_alr, w_alpha2=w_alpha2_p, b_alpha=b_alpha[l].reshape(1, qk_a),
        g_onorm=g_onorm[l].reshape(1, DV_A), w_spatial=w_spatial[l], b_spatial=b_full,
        conv_w=conv_w[l], conv_b=conv_b[l], w_rgate=w_rgate[l].astype(BF16), b_rgate=b_rgate[l],
        w_igate=w_igate[l].astype(BF16), b_igate=b_igate[l], lru_lambda=lru_lambda[l],
        w_out=w_out[l].astype(BF16), ln2=ln2[l], w_up=w_up[l].astype(BF16), w_down=w_down[l].astype(BF16))


def kernel(x_prompt, x_sample, state_gla, state_conv, state_lru, ln1, w_in, w_alpha2, b_alpha, g_onorm, w_spatial, b_spatial, conv_w, conv_b, w_rgate, b_rgate, w_igate, b_igate, lru_lambda, w_out, ln2, w_up, w_down, ln_final):
    bp, tp, d = x_prompt.shape
    bs, ts, _ = x_sample.shape
    depth = w_in.shape[0]
    h_a = (d // 2) // DV_A
    w_c = d // 4
    zero_gla = jnp.zeros((bp, h_a, DK_A, DV_A), F32)
    zero_conv = jnp.zeros((bp, CONV_W - 1, w_c), F32)
    zero_lru = jnp.zeros((bp, w_c), F32)
    xp = x_prompt.reshape(bp * tp, d)
    xs = x_sample.reshape(bs * ts, d)
    gla_p, conv_p, lru_p, gla_s, conv_s, lru_s, v_s = [], [], [], [], [], [], []
    for l in range(depth):
        w = _prep_layer_weights(l, ln1, w_in, w_alpha2, b_alpha, g_onorm, w_spatial, b_spatial, conv_w,
                                conv_b, w_rgate, b_rgate, w_igate, b_igate, lru_lambda, w_out, ln2, w_up,
                                w_down)
        xp, g1, c1, h1, _ = _layer(xp, zero_gla, zero_conv, zero_lru, w, n_seq=bp, seq_len=tp, want_v=False)
        xs, g2, c2, h2, v2 = _layer(xs, state_gla[l], state_conv[l], state_lru[l], w, n_seq=bs, seq_len=ts,
                                    want_v=True)
        gla_p.append(g1); conv_p.append(c1); lru_p.append(h1)
        gla_s.append(g2); conv_s.append(c2); lru_s.append(h2); v_s.append(v2.reshape(bs, ts, -1))
    y_prompt = _rmsnorm(xp, ln_final, F32).reshape(bp, tp, d)
    y_sample = _rmsnorm(xs, ln_final, F32).reshape(bs, ts, d)
    return (y_prompt, y_sample, jnp.stack(gla_p), jnp.stack(conv_p), jnp.stack(lru_p),
            jnp.stack(gla_s), jnp.stack(conv_s), jnp.stack(lru_s), jnp.stack(v_s))
```

```python
import functools

import jax
import jax.numpy as jnp
from jax import lax
from jax.experimental import pallas as pl
from jax.experimental.pallas import tpu as pltpu

F32 = jnp.float32
BF16 = jnp.bfloat16

DV_A = 128
DK_A = 64
GLA_RANK = 16
GLA_TAU = 16.0
CHUNK = 64
GMLP_CHUNK = 128
H_B = 8
H_C = 8
CONV_W = 4
LRU_C = 8.0
EPS = 1e-6

LANES = 128
SUBLANES = 8
V7X_VMEM_BYTES = 64 * 1024 * 1024
VMEM_CAP = V7X_VMEM_BYTES * 7 // 8

GLA_PAIRS_PER_STEP = 4


def _vmem_limit(*block_bytes):
    need = 4 * sum(block_bytes)
    return int(min(max(need, 16 * 1024 * 1024), VMEM_CAP))


def _nbytes(shape, dtype):
    n = 1
    for s in shape:
        n *= s
    return n * jnp.dtype(dtype).itemsize


def _rmsnorm_body(x_ref, g_ref, o_ref):
    x = x_ref[...]
    ms = jnp.mean(jnp.square(x), axis=-1, keepdims=True)
    o_ref[...] = (x * lax.rsqrt(ms + EPS) * g_ref[...]).astype(o_ref.dtype)


def _rmsnorm_proj_body(x_ref, g_ref, w_ref, o_ref, p_ref):
    x = x_ref[...]
    ms = jnp.mean(jnp.square(x), axis=-1, keepdims=True)
    y = (x * lax.rsqrt(ms + EPS) * g_ref[...]).astype(BF16)
    o_ref[...] = y
    p_ref[...] = jnp.dot(y, w_ref[...], preferred_element_type=F32)


def _rmsnorm(x, g, out_dtype, w_proj=None):
    t, d = x.shape
    tb = min(t, 256)
    grid = (t // tb,)
    x_spec = pl.BlockSpec((tb, d), lambda i: (i, 0))
    g_spec = pl.BlockSpec((1, d), lambda i: (0, 0))
    params = pltpu.CompilerParams(
        dimension_semantics=("parallel",),
        vmem_limit_bytes=_vmem_limit(_nbytes((tb, d), F32), _nbytes((tb, d), out_dtype)))
    if w_proj is None:
        return pl.pallas_call(
            _rmsnorm_body, grid=grid, in_specs=[x_spec, g_spec], out_specs=x_spec,
            out_shape=jax.ShapeDtypeStruct((t, d), out_dtype), compiler_params=params,
            name="rmsnorm")(x, g.reshape(1, d))
    n = w_proj.shape[1]
    return pl.pallas_call(
        _rmsnorm_proj_body, grid=grid,
        in_specs=[x_spec, g_spec, pl.BlockSpec((d, n), lambda i: (0, 0))],
        out_specs=[x_spec, pl.BlockSpec((tb, n), lambda i: (i, 0))],
        out_shape=[jax.ShapeDtypeStruct((t, d), BF16), jax.ShapeDtypeStruct((t, n), F32)],
        compiler_params=params, name="rmsnorm_proj")(x, g.reshape(1, d), w_proj)


def _matmul_body(*refs, k_bounds, has_res, act, nk, n_prompt_blocks):
    n_lhs = len(k_bounds)
    lhs_p, lhs_s = refs[:n_lhs], refs[n_lhs:2 * n_lhs]
    w_ref = refs[2 * n_lhs]
    res_p, res_s = (refs[2 * n_lhs + 1], refs[2 * n_lhs + 2]) if has_res else (None, None)
    o_p, o_s = refs[-2], refs[-1]
    i = pl.program_id(0)

    def run(lhs_refs, res_ref, o_ref):
        if nk > 1:
            @pl.when(pl.program_id(2) == 0)
            def _():
                o_ref[...] = res_ref[...] if has_res else jnp.zeros(o_ref.shape, o_ref.dtype)

        acc = None
        for a_ref, (k0, k1) in zip(lhs_refs, k_bounds):
            part = jnp.dot(a_ref[...], w_ref[k0:k1, :].astype(BF16), preferred_element_type=F32)
            acc = part if acc is None else acc + part
        if nk > 1:
            o_ref[...] += acc
        else:
            if has_res:
                acc = res_ref[...] + acc
            if act == "relu2":
                acc = jnp.square(jnp.maximum(acc, 0.0))
            o_ref[...] = acc.astype(o_ref.dtype)

    @pl.when(i < n_prompt_blocks)
    def _():
        run(lhs_p, res_p, o_p)

    @pl.when(i == n_prompt_blocks)
    def _():
        run(lhs_s, res_s, o_s)


def _matmul(lhs_p, lhs_s, w, layer, *, n_cols, bm, bn, bk=None, res=None, act=None, out_dtype=F32,
            name="matmul"):
    mp, ms = lhs_p[0].shape[0], lhs_s[0].shape[0]
    k_total = w.shape[1]
    widths = [a.shape[1] for a in lhs_p]
    assert sum(widths) == k_total and mp % bm == 0 and n_cols % bn == 0
    if bk is None:
        bk = k_total
    nk = k_total // bk
    assert nk * bk == k_total and (nk == 1 or (len(lhs_p) == 1 and out_dtype == F32 and act is None))
    n_p, n_j = mp // bm, n_cols // bn

    def p_row(i):
        return jnp.minimum(i, n_p - 1)

    def p_col(i, j):
        return jnp.where(i < n_p, j, n_j - 1)

    def s_col(i, j):
        return jnp.where(i == n_p, j, 0)

    if nk == 1:
        bounds, k0 = [], 0
        for wd in widths:
            bounds.append((k0, k0 + wd))
            k0 += wd
        specs_p = [pl.BlockSpec((bm, wd), lambda i, j, k: (p_row(i), 0)) for wd in widths]
        specs_s = [pl.BlockSpec((ms, wd), lambda i, j, k: (0, 0)) for wd in widths]
    else:
        bounds = [(0, bk)]
        specs_p = [pl.BlockSpec((bm, bk), lambda i, j, k: (p_row(i), jnp.where(i < n_p, k, nk - 1)))]
        specs_s = [pl.BlockSpec((ms, bk), lambda i, j, k: (0, jnp.where(i == n_p, k, 0)))]
    w_spec = pl.BlockSpec((None, bk, bn), lambda i, j, k: (layer, k, j))
    io_p = pl.BlockSpec((bm, bn), lambda i, j, k: (p_row(i), p_col(i, j)))
    io_s = pl.BlockSpec((ms, bn), lambda i, j, k: (0, s_col(i, j)))
    in_specs = specs_p + specs_s + [w_spec]
    args = list(lhs_p) + list(lhs_s) + [w]
    if res is not None:
        in_specs += [io_p, io_s]
        args += list(res)
    body = functools.partial(_matmul_body, k_bounds=tuple(bounds), has_res=res is not None, act=act, nk=nk,
                             n_prompt_blocks=n_p)
    n_io = 2 if res is not None else 1
    limit = _vmem_limit(_nbytes((bm + ms, bk), BF16), _nbytes((bk, bn), F32),
                        n_io * _nbytes((bm + ms, bn), F32))
    return pl.pallas_call(
        body, grid=(n_p + 1, n_j, nk), in_specs=in_specs, out_specs=[io_p, io_s],
        out_shape=[jax.ShapeDtypeStruct((mp, n_cols), out_dtype), jax.ShapeDtypeStruct((ms, n_cols), out_dtype)],
        compiler_params=pltpu.CompilerParams(
            dimension_semantics=("arbitrary", "arbitrary", "arbitrary"), vmem_limit_bytes=limit),
        name=name)(*args)


def _shift_rows(x, s, fill):
    row = lax.broadcasted_iota(jnp.int32, x.shape, 0)
    return jnp.where(row >= s, pltpu.roll(x, s, 0), fill)


def _softplus(x):
    return jnp.maximum(x, 0.0) + jnp.log1p(jnp.exp(-jnp.abs(x)))


def _gla_body(q_ref, k_ref, v_ref, g_ref, alr_ref, wa_ref, ba_ref, gn_ref, s0_ref,
              o_ref, s_out_ref, st_ref, *, chunk, n_chunks, n_pairs):
    n = pl.program_id(2)

    @pl.when(n == 0)
    def _():
        for p in range(n_pairs):
            st_ref[p] = s0_ref[0, p].T

    c = chunk
    qw, vw = 2 * DK_A, 2 * DV_A
    lane = lax.broadcasted_iota(jnp.int32, (c, qw), 1)
    head0 = lane < DK_A
    row2 = lax.broadcasted_iota(jnp.int32, (2 * c, 2 * c), 0)
    col2 = lax.broadcasted_iota(jnp.int32, (2 * c, 2 * c), 1)
    causal = ((row2 >= c) == (col2 >= c)) & (row2 >= col2)
    nt = (((1,), (1,)), ((), ()))
    tn = (((0,), (0,)), ((), ()))

    def stack_heads(z):
        return jnp.concatenate([jnp.where(head0, z, 0.0), jnp.where(head0, 0.0, z)], axis=0).astype(BF16)

    for ci in range(n_chunks):
        rows = pl.ds(ci * c, c)
        z = jnp.dot(alr_ref[rows, :].astype(BF16), wa_ref[...], preferred_element_type=F32) + ba_ref[...]
        b = -_softplus(-z) / GLA_TAU
        s = 1
        while s < c:
            b = b + _shift_rows(b, s, 0.0)
            s *= 2
        b_end = b[c - 1:c, :]
        q = q_ref[rows, :] * (DK_A ** -0.5)
        k = k_ref[rows, :]
        q_dec = q * jnp.exp(b)
        k_inv = k * jnp.exp(-b)
        k_end = k * jnp.exp(b_end - b)
        decay = jnp.exp(b_end)
        for p in range(n_pairs):
            ql = slice(p * qw, (p + 1) * qw)
            q_t, k_t, k_e = stack_heads(q_dec[:, ql]), stack_heads(k_inv[:, ql]), stack_heads(k_end[:, ql])
            v0 = v_ref[rows, pl.ds(p * vw, DV_A)]
            v1 = v_ref[rows, pl.ds(p * vw + DV_A, DV_A)]
            v_s = jnp.concatenate([v0, v1], axis=0).astype(BF16)
            att = lax.dot_general(q_t, k_t, nt, preferred_element_type=F32)
            att = jnp.where(causal, att, 0.0).astype(BF16)
            st = st_ref[p]
            o = (jnp.dot(att, v_s, preferred_element_type=F32)
                 + lax.dot_general(q_t, st.astype(BF16), nt, preferred_element_type=F32))
            st_ref[p] = decay[:, ql] * st + lax.dot_general(v_s, k_e, tn, preferred_element_type=F32)
            ms = jnp.mean(jnp.square(o), axis=-1, keepdims=True)
            o = (o * lax.rsqrt(ms + EPS) * gn_ref[...])
            g0 = g_ref[rows, pl.ds(p * vw, DV_A)]
            g1 = g_ref[rows, pl.ds(p * vw + DV_A, DV_A)]
            o_ref[rows, pl.ds(p * vw, DV_A)] = (o[:c] * (g0 * jax.nn.sigmoid(g0))).astype(o_ref.dtype)
            o_ref[rows, pl.ds(p * vw + DV_A, DV_A)] = (o[c:] * (g1 * jax.nn.sigmoid(g1))).astype(o_ref.dtype)

    @pl.when(n == pl.num_programs(2) - 1)
    def _():
        for p in range(n_pairs):
            s_out_ref[0, p] = st_ref[p].T


def _gla(hproj, alr, wa, ba, gn, s0, *, n_seq, seq_len, col_q, col_k, col_v, col_g):
    t_total = hproj.shape[0]
    h_a = s0.shape[1]
    pairs = h_a // 2
    pps = GLA_PAIRS_PER_STEP
    assert pairs % pps == 0
    chunk = min(CHUNK, seq_len)
    n_chunks = min(4, seq_len // chunk)
    tb = chunk * n_chunks
    nb = seq_len // tb
    qw, vw = 2 * DK_A * pps, 2 * DV_A * pps
    assert col_q % qw == 0 and col_k % qw == 0 and col_v % vw == 0 and col_g % vw == 0
    cq, ck, cv, cg = col_q // qw, col_k // qw, col_v // vw, col_g // vw
    row = lambda s, n: s * nb + n
    in_specs = [
        pl.BlockSpec((tb, qw), lambda s, p, n: (row(s, n), cq + p)),
        pl.BlockSpec((tb, qw), lambda s, p, n: (row(s, n), ck + p)),
        pl.BlockSpec((tb, vw), lambda s, p, n: (row(s, n), cv + p)),
        pl.BlockSpec((tb, vw), lambda s, p, n: (row(s, n), cg + p)),
        pl.BlockSpec((tb, LANES), lambda s, p, n: (row(s, n), 0)),
        pl.BlockSpec((LANES, qw), lambda s, p, n: (0, p)),
        pl.BlockSpec((1, qw), lambda s, p, n: (0, p)),
        pl.BlockSpec((1, DV_A), lambda s, p, n: (0, 0)),
        pl.BlockSpec((1, pps, 2 * DK_A, DV_A), lambda s, p, n: (s, p, 0, 0)),
    ]
    out_specs = [
        pl.BlockSpec((tb, vw), lambda s, p, n: (row(s, n), p)),
        pl.BlockSpec((1, pps, 2 * DK_A, DV_A), lambda s, p, n: (s, p, 0, 0)),
    ]
    o_a, s_new = pl.pallas_call(
        functools.partial(_gla_body, chunk=chunk, n_chunks=n_chunks, n_pairs=pps),
        grid=(n_seq, pairs // pps, nb), in_specs=in_specs, out_specs=out_specs,
        out_shape=[jax.ShapeDtypeStruct((t_total, h_a * DV_A), BF16),
                   jax.ShapeDtypeStruct((n_seq, pairs, 2 * DK_A, DV_A), F32)],
        scratch_shapes=[pltpu.VMEM((pps, DV_A, 2 * DK_A), F32)],
        compiler_params=pltpu.CompilerParams(
            dimension_semantics=("parallel", "parallel", "arbitrary"),
            vmem_limit_bytes=_vmem_limit(_nbytes((tb, 2 * qw + 2 * vw + LANES), F32), _nbytes((tb, vw), BF16))),
        name="gla")(hproj, hproj, hproj, hproj, alr, wa, ba, gn, s0.reshape(n_seq, pairs, 2 * DK_A, DV_A))
    return o_a, s_new.reshape(n_seq, h_a, DK_A, DV_A)


def _gmlp_body(u_ref, v_ref, w_ref, b_ref, o_ref, *vg_refs, c, dh):
    vg = jax.nn.gelu(v_ref[...])
    if vg_refs:
        vg_refs[0][...] = vg
    ug = jax.nn.gelu(u_ref[...])
    tril = (lax.broadcasted_iota(jnp.int32, (c, c), 0) >= lax.broadcasted_iota(jnp.int32, (c, c), 1))
    for h in range(w_ref.shape[0]):
        cols = pl.ds(h * dh, dh)
        w = jnp.where(tril, w_ref[h, :c, :c], 0.0).astype(BF16)
        mixed = jnp.dot(w, vg[:, h * dh:(h + 1) * dh].astype(BF16), preferred_element_type=F32) + b_ref[:c, cols]
        o_ref[:, cols] = (ug[:, h * dh:(h + 1) * dh] * mixed).astype(o_ref.dtype)


def _gmlp(hproj, w_sp, b_full, *, seq_len, col_u, col_v, want_v):
    t_total = hproj.shape[0]
    n_h, gc, _ = w_sp.shape
    w_b = b_full.shape[1]
    dh = w_b // n_h
    c = min(gc, seq_len)
    blk = lambda col: pl.BlockSpec((c, w_b), lambda r: (r, col // w_b))
    out_specs = [pl.BlockSpec((c, w_b), lambda r: (r, 0))]
    out_shape = [jax.ShapeDtypeStruct((t_total, w_b), BF16)]
    if want_v:
        out_specs.append(pl.BlockSpec((c, w_b), lambda r: (r, 0)))
        out_shape.append(jax.ShapeDtypeStruct((t_total, w_b), F32))
    outs = pl.pallas_call(
        functools.partial(_gmlp_body, c=c, dh=dh), grid=(t_total // c,),
        in_specs=[blk(col_u), blk(col_v),
                  pl.BlockSpec((n_h, gc, gc), lambda r: (0, 0, 0)),
                  pl.BlockSpec((gc, w_b), lambda r: (0, 0))],
        out_specs=out_specs, out_shape=out_shape,
        compiler_params=pltpu.CompilerParams(
            dimension_semantics=("parallel",),
            vmem_limit_bytes=_vmem_limit(4 * _nbytes((c, w_b), F32), _nbytes((n_h, gc, gc), F32))),
        name="gmlp")(hproj, hproj, w_sp, b_full)
    return (outs[0], outs[1]) if want_v else (outs[0], None)


def _lru_body(x_ref, gate_ref, cw_ref, cb_ref, wr_ref, br_ref, wi_ref, bi_ref, lam_ref, h0_ref, tail0_ref,
              o_ref, h_ref, tail_ref, *, dh):
    n = pl.program_id(1)

    @pl.when(n == 0)
    def _():
        h_ref[...] = h0_ref[...]
        tail_ref[...] = tail0_ref[...]

    tb = x_ref.shape[0]
    row8 = lax.broadcasted_iota(jnp.int32, (SUBLANES, dh), 0)
    for hb in range(x_ref.shape[1] // dh):
        cols = pl.ds(hb * dh, dh)
        x = x_ref[:, cols]
        tail = tail_ref[0, :, cols]

        def delayed(j):
            rolled = pltpu.roll(x, j, 0)
            head = jnp.where(row8 >= j, rolled[:SUBLANES], pltpu.roll(tail, j, 0))
            return jnp.concatenate([head, rolled[SUBLANES:]], axis=0)

        y = cb_ref[:, cols] + delayed(CONV_W - 1) * cw_ref[0:1, cols]
        for j in range(1, CONV_W - 1):
            y = y + delayed(CONV_W - 1 - j) * cw_ref[j:j + 1, cols]
        y = y + x * cw_ref[CONV_W - 1:CONV_W, cols]
        tail_ref[0, :, cols] = x[tb - SUBLANES:]

        yb = y.astype(BF16)
        r = jax.nn.sigmoid(jnp.dot(yb, wr_ref[hb].astype(BF16), preferred_element_type=F32) + br_ref[:, cols])
        ig = jax.nn.sigmoid(jnp.dot(yb, wi_ref[hb].astype(BF16), preferred_element_type=F32) + bi_ref[:, cols])
        log_a = -LRU_C * r * _softplus(-lam_ref[:, cols])
        a = jnp.exp(log_a)
        bt = jnp.sqrt(-jnp.tanh(log_a) * (a * a + 1.0)) * (ig * y)
        s = 1
        while s < tb:
            bt = a * _shift_rows(bt, s, 0.0) + bt
            a = a * _shift_rows(a, s, 1.0)
            s *= 2
        hseq = bt + a * h_ref[0, :, cols]
        h_ref[0, :, cols] = hseq[tb - 1:]
        o_ref[:, cols] = (hseq * jax.nn.gelu(gate_ref[:, cols])).astype(o_ref.dtype)


def _lru(hproj, cw, cb, wr, br, wi, bi, lam, h0, tail0, *, n_seq, seq_len, col_x, col_gate):
    t_total = hproj.shape[0]
    w_c = cw.shape[1]
    n_h, dh, _ = wr.shape
    tb = min(seq_len, 256)
    nb = seq_len // tb
    vec = lambda rows: pl.BlockSpec((rows, w_c), lambda s, n: (0, 0))
    wspec = pl.BlockSpec((n_h, dh, dh), lambda s, n: (0, 0, 0))
    hspec = pl.BlockSpec((1, 1, w_c), lambda s, n: (s, 0, 0))
    tspec = pl.BlockSpec((1, SUBLANES, w_c), lambda s, n: (s, 0, 0))
    o_c, h_new, tail = pl.pallas_call(
        functools.partial(_lru_body, dh=dh), grid=(n_seq, nb),
        in_specs=[pl.BlockSpec((tb, w_c), lambda s, n: (s * nb + n, col_x // w_c)),
                  pl.BlockSpec((tb, w_c), lambda s, n: (s * nb + n, col_gate // w_c)),
                  vec(CONV_W), vec(1), wspec, vec(1), wspec, vec(1), vec(1), hspec, tspec],
        out_specs=[pl.BlockSpec((tb, w_c), lambda s, n: (s * nb + n, 0)), hspec, tspec],
        out_shape=[jax.ShapeDtypeStruct((t_total, w_c), BF16),
                   jax.ShapeDtypeStruct((n_seq, 1, w_c), F32),
                   jax.ShapeDtypeStruct((n_seq, SUBLANES, w_c), F32)],
        compiler_params=pltpu.CompilerParams(
            dimension_semantics=("parallel", "arbitrary"),
            vmem_limit_bytes=_vmem_limit(4 * _nbytes((tb, w_c), F32))),
        name="conv_rglru")(hproj, hproj, cw, cb.reshape(1, w_c), wr, br.reshape(1, w_c), wi,
                           bi.reshape(1, w_c), lam.reshape(1, w_c), h0.reshape(n_seq, 1, w_c), tail0)
    return o_c, h_new.reshape(n_seq, w_c), tail


def _mixers(hp_a, hp_b, alr, s_gla, s_conv, s_lru, w, *, n_seq, seq_len, want_v):
    w_a = s_gla.shape[1] * DV_A
    qk_a = s_gla.shape[1] * DK_A
    w_b = w_c = s_lru.shape[1]
    o_a, s_gla_new = _gla(hp_a, alr, w["w_alpha2"], w["b_alpha"], w["g_onorm"], s_gla, n_seq=n_seq,
                          seq_len=seq_len, col_q=0, col_k=qk_a, col_v=2 * qk_a, col_g=2 * qk_a + w_a)
    o_b, vg = _gmlp(hp_b, w["w_spatial"], w["b_spatial"], seq_len=seq_len, col_u=0, col_v=w_b, want_v=want_v)
    tail0 = jnp.pad(s_conv, ((0, 0), (SUBLANES - (CONV_W - 1), 0), (0, 0)))
    o_c, s_lru_new, tail = _lru(hp_b, w["conv_w"], w["conv_b"], w["w_rgate"], w["b_rgate"], w["w_igate"],
                                w["b_igate"], w["lru_lambda"], s_lru, tail0, n_seq=n_seq, seq_len=seq_len,
                                col_x=2 * w_b, col_gate=2 * w_b + w_c)
    return [o_a, o_b, o_c], (s_gla_new, tail[:, SUBLANES - (CONV_W - 1):], s_lru_new, vg)


def _small_layer_weights(l, w_in, w_alpha2, b_alpha, g_onorm, w_spatial, b_spatial, conv_w, conv_b,
                         w_rgate, b_rgate, w_igate, b_igate, lru_lambda):
    d = w_in.shape[1]
    w_a, w_b = d // 2, d // 4
    qk_a = (w_a // DV_A) * DK_A
    c_alr = 2 * qk_a + 2 * w_a
    w_alr = jnp.pad(w_in[l, :, c_alr:c_alr + GLA_RANK], ((0, 0), (0, LANES - GLA_RANK))).astype(BF16)
    w_alpha2_p = jnp.pad(w_alpha2[l], ((0, LANES - GLA_RANK), (0, 0))).astype(BF16)
    b_full = jnp.repeat(b_spatial[l].T, w_b // H_B, axis=1)
    return dict(
        w_alr=w_alr, w_alpha2=w_alpha2_p, b_alpha=b_alpha[l].reshape(1, qk_a),
        g_onorm=g_onorm[l].reshape(1, DV_A), w_spatial=w_spatial[l], b_spatial=b_full,
        conv_w=conv_w[l], conv_b=conv_b[l], w_rgate=w_rgate[l], b_rgate=b_rgate[l],
        w_igate=w_igate[l], b_igate=b_igate[l], lru_lambda=lru_lambda[l])


def kernel(x_prompt, x_sample, state_gla, state_conv, state_lru, ln1, w_in, w_alpha2, b_alpha, g_onorm, w_spatial, b_spatial, conv_w, conv_b, w_rgate, b_rgate, w_igate, b_igate, lru_lambda, w_out, ln2, w_up, w_down, ln_final):
    bp, tp, d = x_prompt.shape
    bs, ts, _ = x_sample.shape
    depth = w_in.shape[0]
    w_a, w_b, w_c = d // 2, d // 4, d // 4
    h_a = w_a // DV_A
    qk_a = h_a * DK_A
    n_a = 2 * qk_a + 2 * w_a
    n_b = 2 * w_b + 2 * w_c
    zero_gla = jnp.zeros((bp, h_a, DK_A, DV_A), F32)
    zero_conv = jnp.zeros((bp, CONV_W - 1, w_c), F32)
    zero_lru = jnp.zeros((bp, w_c), F32)
    xp = x_prompt.reshape(bp * tp, d)
    xs = x_sample.reshape(bs * ts, d)
    mm = functools.partial(_matmul, bm=1024, bn=512)
    st_p, st_s = [], []
    for l in range(depth):
        w = _small_layer_weights(l, w_in, w_alpha2, b_alpha, g_onorm, w_spatial, b_spatial, conv_w, conv_b,
                                 w_rgate, b_rgate, w_igate, b_igate, lru_lambda)
        w_in_b = w_in[l:l + 1, :, n_a + GLA_RANK:]
        xn_p, alr_p = _rmsnorm(xp, ln1[l], BF16, w_proj=w["w_alr"])
        xn_s, alr_s = _rmsnorm(xs, ln1[l], BF16, w_proj=w["w_alr"])
        hpa_p, hpa_s = mm([xn_p], [xn_s], w_in, l, n_cols=n_a, name="in_proj_a")
        hpb_p, hpb_s = mm([xn_p], [xn_s], w_in_b, 0, n_cols=n_b, name="in_proj_b")
        mix_p, new_p = _mixers(hpa_p, hpb_p, alr_p, zero_gla, zero_conv, zero_lru, w, n_seq=bp, seq_len=tp,
                               want_v=False)
        mix_s, new_s = _mixers(hpa_s, hpb_s, alr_s, state_gla[l], state_conv[l], state_lru[l], w, n_seq=bs,
                               seq_len=ts, want_v=True)
        st_p.append(new_p)
        st_s.append(new_s)
        xp, xs = mm(mix_p, mix_s, w_out, l, n_cols=d, res=(xp, xs), name="out_proj")
        xn_p = _rmsnorm(xp, ln2[l], BF16)
        xn_s = _rmsnorm(xs, ln2[l], BF16)
        hid_p, hid_s = mm([xn_p], [xn_s], w_up, l, n_cols=w_up.shape[2], act="relu2", out_dtype=BF16,
                          name="mlp_up")
        xp, xs = mm([hid_p], [hid_s], w_down, l, n_cols=d, bk=4096, res=(xp, xs), name="mlp_down")
    y_prompt = _rmsnorm(xp, ln_final, F32).reshape(bp, tp, d)
    y_sample = _rmsnorm(xs, ln_final, F32).reshape(bs, ts, d)
    stack = lambda sts, i: jnp.stack([s[i] for s in sts])
    return (y_prompt, y_sample, stack(st_p, 0), stack(st_p, 1), stack(st_p, 2),
            stack(st_s, 0), stack(st_s, 1), stack(st_s, 2),
            jnp.stack([s[3].reshape(bs, ts, w_b) for s in st_s]))
```

```python
import functools

import jax
import jax.numpy as jnp
from jax import lax
from jax.experimental import pallas as pl
from jax.experimental.pallas import tpu as pltpu

F32 = jnp.float32
BF16 = jnp.bfloat16

DV_A = 128
DK_A = 64
GLA_RANK = 16
GLA_TAU = 16.0
CHUNK = 64
GMLP_CHUNK = 128
H_B = 8
H_C = 8
CONV_W = 4
LRU_C = 8.0
EPS = 1e-6

LANES = 128
SUBLANES = 8
V7X_VMEM_BYTES = 64 * 1024 * 1024
VMEM_CAP = V7X_VMEM_BYTES * 7 // 8

GLA_PAIRS_PER_STEP = 4


def _vmem_limit(*block_bytes):
    need = 4 * sum(block_bytes)
    return int(min(max(need, 16 * 1024 * 1024), VMEM_CAP))


def _nbytes(shape, dtype):
    n = 1
    for s in shape:
        n *= s
    return n * jnp.dtype(dtype).itemsize


def _rmsnorm_body(x_ref, g_ref, o_ref):
    x = x_ref[...]
    ms = jnp.mean(jnp.square(x), axis=-1, keepdims=True)
    o_ref[...] = (x * lax.rsqrt(ms + EPS) * g_ref[...]).astype(o_ref.dtype)


def _rmsnorm_proj_body(x_ref, g_ref, w_ref, o_ref, p_ref):
    x = x_ref[...]
    ms = jnp.mean(jnp.square(x), axis=-1, keepdims=True)
    y = (x * lax.rsqrt(ms + EPS) * g_ref[...]).astype(BF16)
    o_ref[...] = y
    p_ref[...] = lax.dot_general(y, w_ref[...].astype(BF16), (((1,), (1,)), ((), ())),
                                 preferred_element_type=F32)


def _rmsnorm(x, g, out_dtype, w_proj=None):
    t, d = x.shape
    tb = min(t, 256)
    grid = (t // tb,)
    x_spec = pl.BlockSpec((tb, d), lambda i: (i, 0))
    g_spec = pl.BlockSpec((1, d), lambda i: (0, 0))
    params = pltpu.CompilerParams(
        dimension_semantics=("parallel",),
        vmem_limit_bytes=_vmem_limit(_nbytes((tb, d), F32), _nbytes((tb, d), out_dtype)))
    if w_proj is None:
        return pl.pallas_call(
            _rmsnorm_body, grid=grid, in_specs=[x_spec, g_spec], out_specs=x_spec,
            out_shape=jax.ShapeDtypeStruct((t, d), out_dtype), compiler_params=params,
            name="rmsnorm")(x, g.reshape(1, d))
    n = w_proj.shape[0]
    return pl.pallas_call(
        _rmsnorm_proj_body, grid=grid,
        in_specs=[x_spec, g_spec, pl.BlockSpec((n, d), lambda i: (0, 0))],
        out_specs=[x_spec, pl.BlockSpec((tb, n), lambda i: (i, 0))],
        out_shape=[jax.ShapeDtypeStruct((t, d), BF16), jax.ShapeDtypeStruct((t, n), F32)],
        compiler_params=params, name="rmsnorm_proj")(x, g.reshape(1, d), w_proj)


def _matmul_body(*refs, k_bounds, has_res, act, nk, n_prompt_blocks):
    n_lhs = len(k_bounds)
    lhs_p, lhs_s = refs[:n_lhs], refs[n_lhs:2 * n_lhs]
    w_ref = refs[2 * n_lhs]
    res_p, res_s = (refs[2 * n_lhs + 1], refs[2 * n_lhs + 2]) if has_res else (None, None)
    o_p, o_s = refs[-2], refs[-1]
    i = pl.program_id(0)

    def run(lhs_refs, res_ref, o_ref):
        if nk > 1:
            @pl.when(pl.program_id(2) == 0)
            def _():
                o_ref[...] = res_ref[...] if has_res else jnp.zeros(o_ref.shape, o_ref.dtype)

        acc = None
        for a_ref, (k0, k1) in zip(lhs_refs, k_bounds):
            part = jnp.dot(a_ref[...], w_ref[k0:k1, :].astype(BF16), preferred_element_type=F32)
            acc = part if acc is None else acc + part
        if nk > 1:
            o_ref[...] += acc
        else:
            if has_res:
                acc = res_ref[...] + acc
            if act == "relu2":
                acc = jnp.square(jnp.maximum(acc, 0.0))
            o_ref[...] = acc.astype(o_ref.dtype)

    @pl.when(i < n_prompt_blocks)
    def _():
        run(lhs_p, res_p, o_p)

    @pl.when(i == n_prompt_blocks)
    def _():
        run(lhs_s, res_s, o_s)


def _matmul(lhs_p, lhs_s, w, layer, *, n_cols, bm, bn, bk=None, res=None, act=None, out_dtype=F32,
            name="matmul"):
    mp, ms = lhs_p[0].shape[0], lhs_s[0].shape[0]
    k_total = w.shape[1]
    widths = [a.shape[1] for a in lhs_p]
    assert sum(widths) == k_total and mp % bm == 0 and n_cols % bn == 0
    if bk is None:
        bk = k_total
    nk = k_total // bk
    assert nk * bk == k_total and (nk == 1 or (len(lhs_p) == 1 and out_dtype == F32 and act is None))
    n_p, n_j = mp // bm, n_cols // bn

    def p_row(i):
        return jnp.minimum(i, n_p - 1)

    def p_col(i, j):
        return jnp.where(i < n_p, j, n_j - 1)

    def s_col(i, j):
        return jnp.where(i == n_p, j, 0)

    if nk == 1:
        bounds, k0 = [], 0
        for wd in widths:
            bounds.append((k0, k0 + wd))
            k0 += wd
        specs_p = [pl.BlockSpec((bm, wd), lambda i, j, k: (p_row(i), 0)) for wd in widths]
        specs_s = [pl.BlockSpec((ms, wd), lambda i, j, k: (0, 0)) for wd in widths]
    else:
        bounds = [(0, bk)]
        specs_p = [pl.BlockSpec((bm, bk), lambda i, j, k: (p_row(i), jnp.where(i < n_p, k, nk - 1)))]
        specs_s = [pl.BlockSpec((ms, bk), lambda i, j, k: (0, jnp.where(i == n_p, k, 0)))]
    w_spec = pl.BlockSpec((None, bk, bn), lambda i, j, k: (layer, k, j))
    io_p = pl.BlockSpec((bm, bn), lambda i, j, k: (p_row(i), p_col(i, j)))
    io_s = pl.BlockSpec((ms, bn), lambda i, j, k: (0, s_col(i, j)))
    in_specs = specs_p + specs_s + [w_spec]
    args = list(lhs_p) + list(lhs_s) + [w]
    if res is not None:
        in_specs += [io_p, io_s]
        args += list(res)
    body = functools.partial(_matmul_body, k_bounds=tuple(bounds), has_res=res is not None, act=act, nk=nk,
                             n_prompt_blocks=n_p)
    n_io = 2 if res is not None else 1
    limit = _vmem_limit(_nbytes((bm + ms, bk), BF16), _nbytes((bk, bn), F32),
                        n_io * _nbytes((bm + ms, bn), F32))
    return pl.pallas_call(
        body, grid=(n_p + 1, n_j, nk), in_specs=in_specs, out_specs=[io_p, io_s],
        out_shape=[jax.ShapeDtypeStruct((mp, n_cols), out_dtype), jax.ShapeDtypeStruct((ms, n_cols), out_dtype)],
        compiler_params=pltpu.CompilerParams(
            dimension_semantics=("arbitrary", "arbitrary", "arbitrary"), vmem_limit_bytes=limit),
        name=name)(*args)


def _ws_matmul_body(*refs, k_bounds, has_res, act, n_prompt_blocks, chunk_rows, transposed):
    n_lhs = len(k_bounds)
    lhs_p, lhs_s = refs[:n_lhs], refs[n_lhs:2 * n_lhs]
    wc_ref = refs[2 * n_lhs]
    res_p, res_s = (refs[2 * n_lhs + 1], refs[2 * n_lhs + 2]) if has_res else (None, None)
    o_p, o_s, w_scr = refs[-3], refs[-2], refs[-1]
    jj, i = pl.program_id(0), pl.program_id(1)
    slot_load = jj % 2
    slot_use = 1 - slot_load

    def stage():
        r0 = pl.multiple_of(jnp.minimum(i, n_prompt_blocks - 1) * chunk_rows, chunk_rows)
        w_scr[slot_load, pl.ds(r0, chunk_rows), :] = wc_ref[...].astype(BF16)

    def run(lhs_refs, res_ref, o_ref):
        stage()
        acc = None
        for a_ref, (k0, k1) in zip(lhs_refs, k_bounds):
            if transposed:
                part = lax.dot_general(a_ref[...], w_scr[slot_use, :, k0:k1], (((1,), (1,)), ((), ())),
                                       preferred_element_type=F32)
            else:
                part = jnp.dot(a_ref[...], w_scr[slot_use, k0:k1, :], preferred_element_type=F32)
            acc = part if acc is None else acc + part
        if has_res:
            acc = res_ref[...] + acc
        if act == "relu2":
            acc = jnp.square(jnp.maximum(acc, 0.0))
        o_ref[...] = acc.astype(o_ref.dtype)

    @pl.when(jj == 0)
    def _():
        stage()

    @pl.when((jj > 0) & (i < n_prompt_blocks))
    def _():
        run(lhs_p, res_p, o_p)

    @pl.when((jj > 0) & (i == n_prompt_blocks))
    def _():
        run(lhs_s, res_s, o_s)


def _ws_matmul(lhs_p, lhs_s, w, layer, *, n_cols, transposed=False, bm=512, bn=1024, res=None, act=None,
               out_dtype=F32, name="ws_matmul"):
    mp, ms = lhs_p[0].shape[0], lhs_s[0].shape[0]
    k_total = w.shape[2] if transposed else w.shape[1]
    widths = [a.shape[1] for a in lhs_p]
    assert sum(widths) == k_total and mp % bm == 0 and n_cols % bn == 0
    n_p, n_j = mp // bm, n_cols // bn
    bounds, k0 = [], 0
    for wd in widths:
        bounds.append((k0, k0 + wd))
        k0 += wd

    def row_blk(jj, i):
        return jnp.where(jj == 0, 0, jnp.minimum(i, n_p - 1))

    def col_use(jj):
        return jnp.maximum(jj - 1, 0)

    def chunk(jj, i):
        return jnp.where(jj < n_j, jnp.minimum(i, n_p - 1), n_p - 1)

    def col_load(jj):
        return jnp.minimum(jj, n_j - 1)

    if transposed:
        chunk_rows = bn // n_p
        w_spec = pl.BlockSpec((None, chunk_rows, k_total), lambda jj, i: (layer, col_load(jj) * n_p + chunk(jj, i), 0))
        scratch = pltpu.VMEM((2, bn, k_total), BF16)
    else:
        chunk_rows = k_total // n_p
        w_spec = pl.BlockSpec((None, chunk_rows, bn), lambda jj, i: (layer, chunk(jj, i), col_load(jj)))
        scratch = pltpu.VMEM((2, k_total, bn), BF16)
    assert chunk_rows * n_p == (bn if transposed else k_total) and chunk_rows % 16 == 0
    specs_p = [pl.BlockSpec((bm, wd), lambda jj, i: (row_blk(jj, i), 0)) for wd in widths]
    specs_s = [pl.BlockSpec((ms, wd), lambda jj, i: (0, 0)) for wd in widths]
    io_p = pl.BlockSpec((bm, bn), lambda jj, i: (row_blk(jj, i), col_use(jj)))
    io_s = pl.BlockSpec((ms, bn), lambda jj, i: (0, col_use(jj)))
    in_specs = specs_p + specs_s + [w_spec]
    args = list(lhs_p) + list(lhs_s) + [w]
    if res is not None:
        in_specs += [io_p, io_s]
        args += list(res)
    body = functools.partial(_ws_matmul_body, k_bounds=tuple(bounds), has_res=res is not None, act=act,
                             n_prompt_blocks=n_p, chunk_rows=chunk_rows, transposed=transposed)
    n_io = 2 if res is not None else 1
    pipelined = (_nbytes((bm + ms, k_total), BF16) + _nbytes((k_total // n_p, bn), F32)
                 + n_io * _nbytes((bm + ms, bn), F32))
    limit = min(VMEM_CAP, 2 * pipelined + _nbytes((2, k_total, bn), BF16) + 2 * _nbytes((bm, bn), F32))
    return pl.pallas_call(
        body, grid=(n_j + 1, n_p + 1), in_specs=in_specs, out_specs=[io_p, io_s],
        out_shape=[jax.ShapeDtypeStruct((mp, n_cols), out_dtype), jax.ShapeDtypeStruct((ms, n_cols), out_dtype)],
        scratch_shapes=[scratch],
        compiler_params=pltpu.CompilerParams(
            dimension_semantics=("arbitrary", "arbitrary"), vmem_limit_bytes=int(limit)),
        name=name)(*args)


def _cast_body(x_ref, o_ref):
    o_ref[...] = x_ref[...].astype(o_ref.dtype)


def _cast_bf16(w):
    l, r, c = w.shape
    rb = 512
    assert r % rb == 0
    spec = pl.BlockSpec((None, rb, c), lambda a, b: (a, b, 0))
    return pl.pallas_call(
        _cast_body, grid=(l, r // rb), in_specs=[spec], out_specs=spec,
        out_shape=jax.ShapeDtypeStruct(w.shape, BF16),
        compiler_params=pltpu.CompilerParams(
            dimension_semantics=("parallel", "parallel"),
            vmem_limit_bytes=_vmem_limit(_nbytes((rb, c), F32), _nbytes((rb, c), BF16))),
        name="cast_bf16")(w)


def _shift_rows(x, s, fill):
    row = lax.broadcasted_iota(jnp.int32, x.shape, 0)
    return jnp.where(row >= s, pltpu.roll(x, s, 0), fill)


def _softplus(x):
    return jnp.maximum(x, 0.0) + jnp.log1p(jnp.exp(-jnp.abs(x)))


def _gla_body(q_ref, k_ref, v_ref, g_ref, alr_ref, wa_ref, ba_ref, gn_ref, s0_ref,
              o_ref, s_out_ref, st_ref, *, chunk, n_chunks, n_pairs):
    n = pl.program_id(2)

    @pl.when(n == 0)
    def _():
        for p in range(n_pairs):
            st_ref[p] = s0_ref[0, p].T

    c = chunk
    qw, vw = 2 * DK_A, 2 * DV_A
    lane = lax.broadcasted_iota(jnp.int32, (c, qw), 1)
    head0 = lane < DK_A
    row2 = lax.broadcasted_iota(jnp.int32, (2 * c, 2 * c), 0)
    col2 = lax.broadcasted_iota(jnp.int32, (2 * c, 2 * c), 1)
    causal = ((row2 >= c) == (col2 >= c)) & (row2 >= col2)
    nt = (((1,), (1,)), ((), ()))
    tn = (((0,), (0,)), ((), ()))

    def stack_heads(z):
        return jnp.concatenate([jnp.where(head0, z, 0.0), jnp.where(head0, 0.0, z)], axis=0).astype(BF16)

    for ci in range(n_chunks):
        rows = pl.ds(ci * c, c)
        z = jnp.dot(alr_ref[rows, :].astype(BF16), wa_ref[...], preferred_element_type=F32) + ba_ref[...]
        b = -_softplus(-z) / GLA_TAU
        s = 1
        while s < c:
            b = b + _shift_rows(b, s, 0.0)
            s *= 2
        b_end = b[c - 1:c, :]
        q = q_ref[rows, :] * (DK_A ** -0.5)
        k = k_ref[rows, :]
        q_dec = q * jnp.exp(b)
        k_inv = k * jnp.exp(-b)
        k_end = k * jnp.exp(b_end - b)
        decay = jnp.exp(b_end)
        for p in range(n_pairs):
            ql = slice(p * qw, (p + 1) * qw)
            q_t, k_t, k_e = stack_heads(q_dec[:, ql]), stack_heads(k_inv[:, ql]), stack_heads(k_end[:, ql])
            v0 = v_ref[rows, pl.ds(p * vw, DV_A)]
            v1 = v_ref[rows, pl.ds(p * vw + DV_A, DV_A)]
            v_s = jnp.concatenate([v0, v1], axis=0).astype(BF16)
            att = lax.dot_general(q_t, k_t, nt, preferred_element_type=F32)
            att = jnp.where(causal, att, 0.0).astype(BF16)
            st = st_ref[p]
            o = (jnp.dot(att, v_s, preferred_element_type=F32)
                 + lax.dot_general(q_t, st.astype(BF16), nt, preferred_element_type=F32))
            st_ref[p] = decay[:, ql] * st + lax.dot_general(v_s, k_e, tn, preferred_element_type=F32)
            ms = jnp.mean(jnp.square(o), axis=-1, keepdims=True)
            o = (o * lax.rsqrt(ms + EPS) * gn_ref[...])
            g0 = g_ref[rows, pl.ds(p * vw, DV_A)]
            g1 = g_ref[rows, pl.ds(p * vw + DV_A, DV_A)]
            o_ref[rows, pl.ds(p * vw, DV_A)] = (o[:c] * (g0 * jax.nn.sigmoid(g0))).astype(o_ref.dtype)
            o_ref[rows, pl.ds(p * vw + DV_A, DV_A)] = (o[c:] * (g1 * jax.nn.sigmoid(g1))).astype(o_ref.dtype)

    @pl.when(n == pl.num_programs(2) - 1)
    def _():
        for p in range(n_pairs):
            s_out_ref[0, p] = st_ref[p].T


def _gla(hproj, alr, wa, ba, gn, s0, *, n_seq, seq_len, col_q, col_k, col_v, col_g):
    t_total = hproj.shape[0]
    h_a = s0.shape[1]
    pairs = h_a // 2
    pps = GLA_PAIRS_PER_STEP
    assert pairs % pps == 0
    chunk = min(CHUNK, seq_len)
    n_chunks = min(4, seq_len // chunk)
    tb = chunk * n_chunks
    nb = seq_len // tb
    qw, vw = 2 * DK_A * pps, 2 * DV_A * pps
    assert col_q % qw == 0 and col_k % qw == 0 and col_v % vw == 0 and col_g % vw == 0
    cq, ck, cv, cg = col_q // qw, col_k // qw, col_v // vw, col_g // vw
    row = lambda s, n: s * nb + n
    in_specs = [
        pl.BlockSpec((tb, qw), lambda s, p, n: (row(s, n), cq + p)),
        pl.BlockSpec((tb, qw), lambda s, p, n: (row(s, n), ck + p)),
        pl.BlockSpec((tb, vw), lambda s, p, n: (row(s, n), cv + p)),
        pl.BlockSpec((tb, vw), lambda s, p, n: (row(s, n), cg + p)),
        pl.BlockSpec((tb, LANES), lambda s, p, n: (row(s, n), 0)),
        pl.BlockSpec((LANES, qw), lambda s, p, n: (0, p)),
        pl.BlockSpec((1, qw), lambda s, p, n: (0, p)),
        pl.BlockSpec((1, DV_A), lambda s, p, n: (0, 0)),
        pl.BlockSpec((1, pps, 2 * DK_A, DV_A), lambda s, p, n: (s, p, 0, 0)),
    ]
    out_specs = [
        pl.BlockSpec((tb, vw), lambda s, p, n: (row(s, n), p)),
        pl.BlockSpec((1, pps, 2 * DK_A, DV_A), lambda s, p, n: (s, p, 0, 0)),
    ]
    o_a, s_new = pl.pallas_call(
        functools.partial(_gla_body, chunk=chunk, n_chunks=n_chunks, n_pairs=pps),
        grid=(n_seq, pairs // pps, nb), in_specs=in_specs, out_specs=out_specs,
        out_shape=[jax.ShapeDtypeStruct((t_total, h_a * DV_A), BF16),
                   jax.ShapeDtypeStruct((n_seq, pairs, 2 * DK_A, DV_A), F32)],
        scratch_shapes=[pltpu.VMEM((pps, DV_A, 2 * DK_A), F32)],
        compiler_params=pltpu.CompilerParams(
            dimension_semantics=("parallel", "parallel", "arbitrary"),
            vmem_limit_bytes=_vmem_limit(_nbytes((tb, 2 * qw + 2 * vw + LANES), F32), _nbytes((tb, vw), BF16))),
        name="gla")(hproj, hproj, hproj, hproj, alr, wa, ba, gn, s0.reshape(n_seq, pairs, 2 * DK_A, DV_A))
    return o_a, s_new.reshape(n_seq, h_a, DK_A, DV_A)


def _gmlp_body(u_ref, v_ref, w_ref, b_ref, o_ref, *vg_refs, c, dh):
    vg = jax.nn.gelu(v_ref[...])
    if vg_refs:
        vg_refs[0][...] = vg
    ug = jax.nn.gelu(u_ref[...])
    tril = (lax.broadcasted_iota(jnp.int32, (c, c), 0) >= lax.broadcasted_iota(jnp.int32, (c, c), 1))
    for h in range(w_ref.shape[0]):
        cols = pl.ds(h * dh, dh)
        w = jnp.where(tril, w_ref[h, :c, :c], 0.0).astype(BF16)
        mixed = jnp.dot(w, vg[:, h * dh:(h + 1) * dh].astype(BF16), preferred_element_type=F32) + b_ref[:c, cols]
        o_ref[:, cols] = (ug[:, h * dh:(h + 1) * dh] * mixed).astype(o_ref.dtype)


def _gmlp(hproj, w_sp, b_full, *, seq_len, col_u, col_v, want_v):
    t_total = hproj.shape[0]
    n_h, gc, _ = w_sp.shape
    w_b = b_full.shape[1]
    dh = w_b // n_h
    c = min(gc, seq_len)
    blk = lambda col: pl.BlockSpec((c, w_b), lambda r: (r, col // w_b))
    out_specs = [pl.BlockSpec((c, w_b), lambda r: (r, 0))]
    out_shape = [jax.ShapeDtypeStruct((t_total, w_b), BF16)]
    if want_v:
        out_specs.append(pl.BlockSpec((c, w_b), lambda r: (r, 0)))
        out_shape.append(jax.ShapeDtypeStruct((t_total, w_b), F32))
    outs = pl.pallas_call(
        functools.partial(_gmlp_body, c=c, dh=dh), grid=(t_total // c,),
        in_specs=[blk(col_u), blk(col_v),
                  pl.BlockSpec((n_h, gc, gc), lambda r: (0, 0, 0)),
                  pl.BlockSpec((gc, w_b), lambda r: (0, 0))],
        out_specs=out_specs, out_shape=out_shape,
        compiler_params=pltpu.CompilerParams(
            dimension_semantics=("parallel",),
            vmem_limit_bytes=_vmem_limit(4 * _nbytes((c, w_b), F32), _nbytes((n_h, gc, gc), F32))),
        name="gmlp")(hproj, hproj, w_sp, b_full)
    return (outs[0], outs[1]) if want_v else (outs[0], None)


def _lru_body(x_ref, gate_ref, cw_ref, cb_ref, wr_ref, br_ref, wi_ref, bi_ref, lam_ref, h0_ref, tail0_ref,
              o_ref, h_ref, tail_ref, *, dh):
    n = pl.program_id(1)

    @pl.when(n == 0)
    def _():
        h_ref[...] = h0_ref[...]
        tail_ref[...] = tail0_ref[...]

    tb = x_ref.shape[0]
    row8 = lax.broadcasted_iota(jnp.int32, (SUBLANES, dh), 0)
    for hb in range(x_ref.shape[1] // dh):
        cols = pl.ds(hb * dh, dh)
        x = x_ref[:, cols]
        tail = tail_ref[0, :, cols]

        def delayed(j):
            rolled = pltpu.roll(x, j, 0)
            head = jnp.where(row8 >= j, rolled[:SUBLANES], pltpu.roll(tail, j, 0))
            return jnp.concatenate([head, rolled[SUBLANES:]], axis=0)

        y = cb_ref[:, cols] + delayed(CONV_W - 1) * cw_ref[0:1, cols]
        for j in range(1, CONV_W - 1):
            y = y + delayed(CONV_W - 1 - j) * cw_ref[j:j + 1, cols]
        y = y + x * cw_ref[CONV_W - 1:CONV_W, cols]
        tail_ref[0, :, cols] = x[tb - SUBLANES:]

        yb = y.astype(BF16)
        r = jax.nn.sigmoid(jnp.dot(yb, wr_ref[hb].astype(BF16), preferred_element_type=F32) + br_ref[:, cols])
        ig = jax.nn.sigmoid(jnp.dot(yb, wi_ref[hb].astype(BF16), preferred_element_type=F32) + bi_ref[:, cols])
        log_a = -LRU_C * r * _softplus(-lam_ref[:, cols])
        a = jnp.exp(log_a)
        bt = jnp.sqrt(-jnp.tanh(log_a) * (a * a + 1.0)) * (ig * y)
        s = 1
        while s < tb:
            bt = a * _shift_rows(bt, s, 0.0) + bt
            a = a * _shift_rows(a, s, 1.0)
            s *= 2
        hseq = bt + a * h_ref[0, :, cols]
        h_ref[0, :, cols] = hseq[tb - 1:]
        o_ref[:, cols] = (hseq * jax.nn.gelu(gate_ref[:, cols])).astype(o_ref.dtype)


def _lru(hproj, cw, cb, wr, br, wi, bi, lam, h0, tail0, *, n_seq, seq_len, col_x, col_gate):
    t_total = hproj.shape[0]
    w_c = cw.shape[1]
    n_h, dh, _ = wr.shape
    tb = min(seq_len, 256)
    nb = seq_len // tb
    vec = lambda rows: pl.BlockSpec((rows, w_c), lambda s, n: (0, 0))
    wspec = pl.BlockSpec((n_h, dh, dh), lambda s, n: (0, 0, 0))
    hspec = pl.BlockSpec((1, 1, w_c), lambda s, n: (s, 0, 0))
    tspec = pl.BlockSpec((1, SUBLANES, w_c), lambda s, n: (s, 0, 0))
    o_c, h_new, tail = pl.pallas_call(
        functools.partial(_lru_body, dh=dh), grid=(n_seq, nb),
        in_specs=[pl.BlockSpec((tb, w_c), lambda s, n: (s * nb + n, col_x // w_c)),
                  pl.BlockSpec((tb, w_c), lambda s, n: (s * nb + n, col_gate // w_c)),
                  vec(CONV_W), vec(1), wspec, vec(1), wspec, vec(1), vec(1), hspec, tspec],
        out_specs=[pl.BlockSpec((tb, w_c), lambda s, n: (s * nb + n, 0)), hspec, tspec],
        out_shape=[jax.ShapeDtypeStruct((t_total, w_c), BF16),
                   jax.ShapeDtypeStruct((n_seq, 1, w_c), F32),
                   jax.ShapeDtypeStruct((n_seq, SUBLANES, w_c), F32)],
        compiler_params=pltpu.CompilerParams(
            dimension_semantics=("parallel", "arbitrary"),
            vmem_limit_bytes=_vmem_limit(4 * _nbytes((tb, w_c), F32))),
        name="conv_rglru")(hproj, hproj, cw, cb.reshape(1, w_c), wr, br.reshape(1, w_c), wi,
                           bi.reshape(1, w_c), lam.reshape(1, w_c), h0.reshape(n_seq, 1, w_c), tail0)
    return o_c, h_new.reshape(n_seq, w_c), tail


def _mixers(hp_a, hp_b, alr, s_gla, s_conv, s_lru, w, *, n_seq, seq_len, want_v):
    w_a = s_gla.shape[1] * DV_A
    qk_a = s_gla.shape[1] * DK_A
    w_b = w_c = s_lru.shape[1]
    o_a, s_gla_new = _gla(hp_a, alr, w["w_alpha2"], w["b_alpha"], w["g_onorm"], s_gla, n_seq=n_seq,
                          seq_len=seq_len, col_q=0, col_k=qk_a, col_v=2 * qk_a, col_g=2 * qk_a + w_a)
    o_b, vg = _gmlp(hp_b, w["w_spatial"], w["b_spatial"], seq_len=seq_len, col_u=0, col_v=w_b, want_v=want_v)
    tail0 = jnp.pad(s_conv, ((0, 0), (SUBLANES - (CONV_W - 1), 0), (0, 0)))
    o_c, s_lru_new, tail = _lru(hp_b, w["conv_w"], w["conv_b"], w["w_rgate"], w["b_rgate"], w["w_igate"],
                                w["b_igate"], w["lru_lambda"], s_lru, tail0, n_seq=n_seq, seq_len=seq_len,
                                col_x=2 * w_b, col_gate=2 * w_b + w_c)
    return [o_a, o_b, o_c], (s_gla_new, tail[:, SUBLANES - (CONV_W - 1):], s_lru_new, vg)


def _small_layer_weights(l, w_in_t, w_alpha2, b_alpha, g_onorm, w_spatial, b_spatial, conv_w, conv_b,
                         w_rgate, b_rgate, w_igate, b_igate, lru_lambda):
    d = w_in_t.shape[2]
    w_a, w_b = d // 2, d // 4
    qk_a = (w_a // DV_A) * DK_A
    c_alr = 2 * qk_a + 2 * w_a
    w_alr = jnp.pad(w_in_t[l, c_alr:c_alr + GLA_RANK, :], ((0, LANES - GLA_RANK), (0, 0)))
    w_alpha2_p = jnp.pad(w_alpha2[l], ((0, LANES - GLA_RANK), (0, 0))).astype(BF16)
    b_full = jnp.repeat(b_spatial[l].T, w_b // H_B, axis=1)
    return dict(
        w_alr=w_alr, w_alpha2=w_alpha2_p, b_alpha=b_alpha[l].reshape(1, qk_a),
        g_onorm=g_onorm[l].reshape(1, DV_A), w_spatial=w_spatial[l], b_spatial=b_full,
        conv_w=conv_w[l], conv_b=conv_b[l], w_rgate=w_rgate[l], b_rgate=b_rgate[l],
        w_igate=w_igate[l], b_igate=b_igate[l], lru_lambda=lru_lambda[l])


def kernel(x_prompt, x_sample, state_gla, state_conv, state_lru, ln1, w_in, w_alpha2, b_alpha, g_onorm, w_spatial, b_spatial, conv_w, conv_b, w_rgate, b_rgate, w_igate, b_igate, lru_lambda, w_out, ln2, w_up, w_down, ln_final):
    bp, tp, d = x_prompt.shape
    bs, ts, _ = x_sample.shape
    depth = w_in.shape[0]
    w_a, w_b, w_c = d // 2, d // 4, d // 4
    h_a = w_a // DV_A
    qk_a = h_a * DK_A
    n_a = 2 * qk_a + 2 * w_a
    n_b = 2 * w_b + 2 * w_c
    zero_gla = jnp.zeros((bp, h_a, DK_A, DV_A), F32)
    zero_conv = jnp.zeros((bp, CONV_W - 1, w_c), F32)
    zero_lru = jnp.zeros((bp, w_c), F32)
    xp = x_prompt.reshape(bp * tp, d)
    xs = x_sample.reshape(bs * ts, d)
    w_in_t = jnp.swapaxes(w_in, 1, 2)
    w_down_bf = _cast_bf16(w_down)
    mm = _ws_matmul
    st_p, st_s = [], []
    for l in range(depth):
        w = _small_layer_weights(l, w_in_t, w_alpha2, b_alpha, g_onorm, w_spatial, b_spatial, conv_w, conv_b,
                                 w_rgate, b_rgate, w_igate, b_igate, lru_lambda)
        w_in_b = w_in_t[l:l + 1, n_a + GLA_RANK:, :]
        xn_p, alr_p = _rmsnorm(xp, ln1[l], BF16, w_proj=w["w_alr"])
        xn_s, alr_s = _rmsnorm(xs, ln1[l], BF16, w_proj=w["w_alr"])
        hpa_p, hpa_s = mm([xn_p], [xn_s], w_in_t, l, n_cols=n_a, transposed=True, name="in_proj_a")
        hpb_p, hpb_s = mm([xn_p], [xn_s], w_in_b, 0, n_cols=n_b, transposed=True, name="in_proj_b")
        mix_p, new_p = _mixers(hpa_p, hpb_p, alr_p, zero_gla, zero_conv, zero_lru, w, n_seq=bp, seq_len=tp,
                               want_v=False)
        mix_s, new_s = _mixers(hpa_s, hpb_s, alr_s, state_gla[l], state_conv[l], state_lru[l], w, n_seq=bs,
                               seq_len=ts, want_v=True)
        st_p.append(new_p)
        st_s.append(new_s)
        xp, xs = mm(mix_p, mix_s, w_out, l, n_cols=d, res=(xp, xs), name="out_proj")
        xn_p = _rmsnorm(xp, ln2[l], BF16)
        xn_s = _rmsnorm(xs, ln2[l], BF16)
        hid_p, hid_s = mm([xn_p], [xn_s], w_up, l, n_cols=w_up.shape[2], act="relu2", out_dtype=BF16,
                          name="mlp_up")
        xp, xs = _matmul([hid_p], [hid_s], w_down_bf, l, n_cols=d, bm=1024, bn=1024, bk=2048, res=(xp, xs),
                         name="mlp_down")
    y_prompt = _rmsnorm(xp, ln_final, F32).reshape(bp, tp, d)
    y_sample = _rmsnorm(xs, ln_final, F32).reshape(bs, ts, d)
    stack = lambda sts, i: jnp.stack([s[i] for s in sts])
    return (y_prompt, y_sample, stack(st_p, 0), stack(st_p, 1), stack(st_p, 2),
            stack(st_s, 0), stack(st_s, 1), stack(st_s, 2),
            jnp.stack([s[3].reshape(bs, ts, w_b) for s in st_s]))
```

```python
import functools

import jax
import jax.numpy as jnp
from jax import lax
from jax.experimental import pallas as pl
from jax.experimental.pallas import tpu as pltpu

F32 = jnp.float32
BF16 = jnp.bfloat16

DV_A = 128
DK_A = 64
GLA_RANK = 16
GLA_TAU = 16.0
CHUNK = 64
GMLP_CHUNK = 128
H_B = 8
H_C = 8
CONV_W = 4
LRU_C = 8.0
EPS = 1e-6

LANES = 128
SUBLANES = 8
V7X_VMEM_BYTES = 64 * 1024 * 1024
VMEM_CAP = V7X_VMEM_BYTES * 7 // 8

GLA_PAIRS_PER_STEP = 4


def _vmem_limit(*block_bytes):
    need = 4 * sum(block_bytes)
    return int(min(max(need, 16 * 1024 * 1024), VMEM_CAP))


def _nbytes(shape, dtype):
    n = 1
    for s in shape:
        n *= s
    return n * jnp.dtype(dtype).itemsize


def _rmsnorm_body(x_ref, g_ref, o_ref):
    x = x_ref[...]
    ms = jnp.mean(jnp.square(x), axis=-1, keepdims=True)
    o_ref[...] = (x * lax.rsqrt(ms + EPS) * g_ref[...]).astype(o_ref.dtype)


def _rmsnorm_proj_body(x_ref, g_ref, w_ref, o_ref, p_ref):
    x = x_ref[...]
    ms = jnp.mean(jnp.square(x), axis=-1, keepdims=True)
    y = (x * lax.rsqrt(ms + EPS) * g_ref[...]).astype(BF16)
    o_ref[...] = y
    p_ref[...] = lax.dot_general(y, w_ref[...].astype(BF16), (((1,), (1,)), ((), ())),
                                 preferred_element_type=F32)


def _rmsnorm(x, g, out_dtype, w_proj=None):
    t, d = x.shape
    tb = min(t, 256)
    grid = (t // tb,)
    x_spec = pl.BlockSpec((tb, d), lambda i: (i, 0))
    g_spec = pl.BlockSpec((1, d), lambda i: (0, 0))
    params = pltpu.CompilerParams(
        dimension_semantics=("parallel",),
        vmem_limit_bytes=_vmem_limit(_nbytes((tb, d), F32), _nbytes((tb, d), out_dtype)))
    if w_proj is None:
        return pl.pallas_call(
            _rmsnorm_body, grid=grid, in_specs=[x_spec, g_spec], out_specs=x_spec,
            out_shape=jax.ShapeDtypeStruct((t, d), out_dtype), compiler_params=params,
            name="rmsnorm")(x, g.reshape(1, d))
    n = w_proj.shape[0]
    return pl.pallas_call(
        _rmsnorm_proj_body, grid=grid,
        in_specs=[x_spec, g_spec, pl.BlockSpec((n, d), lambda i: (0, 0))],
        out_specs=[x_spec, pl.BlockSpec((tb, n), lambda i: (i, 0))],
        out_shape=[jax.ShapeDtypeStruct((t, d), BF16), jax.ShapeDtypeStruct((t, n), F32)],
        compiler_params=params, name="rmsnorm_proj")(x, g.reshape(1, d), w_proj)


def _matmul_body(*refs, k_bounds, has_res, act, nk, n_prompt_blocks):
    n_lhs = len(k_bounds)
    lhs_p, lhs_s = refs[:n_lhs], refs[n_lhs:2 * n_lhs]
    w_ref = refs[2 * n_lhs]
    res_p, res_s = (refs[2 * n_lhs + 1], refs[2 * n_lhs + 2]) if has_res else (None, None)
    o_p, o_s = refs[-2], refs[-1]
    i = pl.program_id(0)

    def run(lhs_refs, res_ref, o_ref):
        if nk > 1:
            @pl.when(pl.program_id(2) == 0)
            def _():
                o_ref[...] = res_ref[...] if has_res else jnp.zeros(o_ref.shape, o_ref.dtype)

        acc = None
        for a_ref, (k0, k1) in zip(lhs_refs, k_bounds):
            part = jnp.dot(a_ref[...], w_ref[k0:k1, :].astype(BF16), preferred_element_type=F32)
            acc = part if acc is None else acc + part
        if nk > 1:
            o_ref[...] += acc
        else:
            if has_res:
                acc = res_ref[...] + acc
            if act == "relu2":
                acc = jnp.square(jnp.maximum(acc, 0.0))
            o_ref[...] = acc.astype(o_ref.dtype)

    @pl.when(i < n_prompt_blocks)
    def _():
        run(lhs_p, res_p, o_p)

    @pl.when(i == n_prompt_blocks)
    def _():
        run(lhs_s, res_s, o_s)


def _matmul(lhs_p, lhs_s, w, layer, *, n_cols, bm, bn, bk=None, res=None, act=None, out_dtype=F32,
            name="matmul"):
    mp, ms = lhs_p[0].shape[0], lhs_s[0].shape[0]
    k_total = w.shape[1]
    widths = [a.shape[1] for a in lhs_p]
    assert sum(widths) == k_total and mp % bm == 0 and n_cols % bn == 0
    if bk is None:
        bk = k_total
    nk = k_total // bk
    assert nk * bk == k_total and (nk == 1 or (len(lhs_p) == 1 and out_dtype == F32 and act is None))
    n_p, n_j = mp // bm, n_cols // bn

    def p_row(i):
        return jnp.minimum(i, n_p - 1)

    def p_col(i, j):
        return jnp.where(i < n_p, j, n_j - 1)

    def s_col(i, j):
        return jnp.where(i == n_p, j, 0)

    if nk == 1:
        bounds, k0 = [], 0
        for wd in widths:
            bounds.append((k0, k0 + wd))
            k0 += wd
        specs_p = [pl.BlockSpec((bm, wd), lambda i, j, k: (p_row(i), 0)) for wd in widths]
        specs_s = [pl.BlockSpec((ms, wd), lambda i, j, k: (0, 0)) for wd in widths]
    else:
        bounds = [(0, bk)]
        specs_p = [pl.BlockSpec((bm, bk), lambda i, j, k: (p_row(i), jnp.where(i < n_p, k, nk - 1)))]
        specs_s = [pl.BlockSpec((ms, bk), lambda i, j, k: (0, jnp.where(i == n_p, k, 0)))]
    w_spec = pl.BlockSpec((None, bk, bn), lambda i, j, k: (layer, k, j))
    io_p = pl.BlockSpec((bm, bn), lambda i, j, k: (p_row(i), p_col(i, j)))
    io_s = pl.BlockSpec((ms, bn), lambda i, j, k: (0, s_col(i, j)))
    in_specs = specs_p + specs_s + [w_spec]
    args = list(lhs_p) + list(lhs_s) + [w]
    if res is not None:
        in_specs += [io_p, io_s]
        args += list(res)
    body = functools.partial(_matmul_body, k_bounds=tuple(bounds), has_res=res is not None, act=act, nk=nk,
                             n_prompt_blocks=n_p)
    n_io = 2 if res is not None else 1
    limit = _vmem_limit(_nbytes((bm + ms, bk), BF16), _nbytes((bk, bn), F32),
                        n_io * _nbytes((bm + ms, bn), F32))
    return pl.pallas_call(
        body, grid=(n_p + 1, n_j, nk), in_specs=in_specs, out_specs=[io_p, io_s],
        out_shape=[jax.ShapeDtypeStruct((mp, n_cols), out_dtype), jax.ShapeDtypeStruct((ms, n_cols), out_dtype)],
        compiler_params=pltpu.CompilerParams(
            dimension_semantics=("arbitrary", "arbitrary", "arbitrary"), vmem_limit_bytes=limit),
        name=name)(*args)


def _ws_matmul_body(*refs, k_bounds, has_res, has_side, act, n_prompt_blocks, chunk_rows, transposed):
    n_lhs = len(k_bounds)
    lhs_p, lhs_s = refs[:n_lhs], refs[n_lhs:2 * n_lhs]
    wc_ref = refs[2 * n_lhs]
    res_p, res_s = (refs[2 * n_lhs + 1], refs[2 * n_lhs + 2]) if has_res else (None, None)
    w_scr = refs[-1]
    if has_side:
        side_in, side_out = refs[-5], refs[-2]
        o_p, o_s = refs[-4], refs[-3]
    else:
        o_p, o_s = refs[-3], refs[-2]
    jj, i = pl.program_id(0), pl.program_id(1)
    slot_load = jj % 2
    slot_use = 1 - slot_load

    def stage():
        r0 = pl.multiple_of(jnp.minimum(i, n_prompt_blocks - 1) * chunk_rows, chunk_rows)
        chunk = wc_ref[0] if len(wc_ref.shape) == 3 else wc_ref[...]
        w_scr[slot_load, pl.ds(r0, chunk_rows), :] = chunk.astype(BF16)

    def run(lhs_refs, res_ref, o_ref):
        stage()
        acc = None
        for a_ref, (k0, k1) in zip(lhs_refs, k_bounds):
            if transposed:
                part = lax.dot_general(a_ref[...], w_scr[slot_use, :, k0:k1], (((1,), (1,)), ((), ())),
                                       preferred_element_type=F32)
            else:
                part = jnp.dot(a_ref[...], w_scr[slot_use, k0:k1, :], preferred_element_type=F32)
            acc = part if acc is None else acc + part
        if has_res:
            acc = res_ref[...] + acc
        if act == "relu2":
            acc = jnp.square(jnp.maximum(acc, 0.0))
        o_ref[...] = acc.astype(o_ref.dtype)

    @pl.when(jj == 0)
    def _():
        stage()

    @pl.when((jj > 0) & (i < n_prompt_blocks))
    def _():
        if has_side:
            side_out[...] = side_in[...].astype(side_out.dtype)
        run(lhs_p, res_p, o_p)

    @pl.when((jj > 0) & (i == n_prompt_blocks))
    def _():
        run(lhs_s, res_s, o_s)


def _ws_matmul(lhs_p, lhs_s, w, layer, *, n_cols, transposed=False, w_row0=0, bm=512, bn=1024, res=None,
               act=None, out_dtype=F32, side_cast=None, name="ws_matmul"):
    mp, ms = lhs_p[0].shape[0], lhs_s[0].shape[0]
    k_total = w.shape[2] if transposed else w.shape[1]
    widths = [a.shape[1] for a in lhs_p]
    assert sum(widths) == k_total and mp % bm == 0 and n_cols % bn == 0
    n_p, n_j = mp // bm, n_cols // bn
    bounds, k0 = [], 0
    for wd in widths:
        bounds.append((k0, k0 + wd))
        k0 += wd

    def row_blk(jj, i):
        return jnp.where(jj == 0, 0, jnp.minimum(i, n_p - 1))

    def col_use(jj):
        return jnp.maximum(jj - 1, 0)

    def chunk(jj, i):
        return jnp.where(jj < n_j, jnp.minimum(i, n_p - 1), n_p - 1)

    def col_load(jj):
        return jnp.minimum(jj, n_j - 1)

    if transposed:
        chunk_rows = bn // n_p
        w_spec = pl.BlockSpec(
            (pl.Element(1), pl.Element(chunk_rows), pl.Element(k_total)),
            lambda jj, i: (layer, pl.multiple_of(
                w_row0 + (col_load(jj) * n_p + chunk(jj, i)) * chunk_rows, SUBLANES), 0))
        scratch = pltpu.VMEM((2, bn, k_total), BF16)
    else:
        chunk_rows = k_total // n_p
        w_spec = pl.BlockSpec((None, chunk_rows, bn), lambda jj, i: (layer, chunk(jj, i), col_load(jj)))
        scratch = pltpu.VMEM((2, k_total, bn), BF16)
    assert chunk_rows * n_p == (bn if transposed else k_total) and chunk_rows % 16 == 0
    specs_p = [pl.BlockSpec((bm, wd), lambda jj, i: (row_blk(jj, i), 0)) for wd in widths]
    specs_s = [pl.BlockSpec((ms, wd), lambda jj, i: (0, 0)) for wd in widths]
    io_p = pl.BlockSpec((bm, bn), lambda jj, i: (row_blk(jj, i), col_use(jj)))
    io_s = pl.BlockSpec((ms, bn), lambda jj, i: (0, col_use(jj)))
    in_specs = specs_p + specs_s + [w_spec]
    args = list(lhs_p) + list(lhs_s) + [w]
    if res is not None:
        in_specs += [io_p, io_s]
        args += list(res)
    out_specs = [io_p, io_s]
    out_shape = [jax.ShapeDtypeStruct((mp, n_cols), out_dtype), jax.ShapeDtypeStruct((ms, n_cols), out_dtype)]
    side_bytes = 0
    if side_cast is not None:
        side, side_layer = side_cast
        side_rows, side_cols = side.shape[1], side.shape[2]
        rb = side_rows // (n_j * n_p)
        assert rb * n_j * n_p == side_rows and rb % 16 == 0

        def side_blk(jj, i):
            return jnp.where(jj == 0, 0, (jj - 1) * n_p + jnp.minimum(i, n_p - 1))

        in_specs.append(pl.BlockSpec((None, rb, side_cols), lambda jj, i: (side_layer, side_blk(jj, i), 0)))
        args.append(side)
        out_specs.append(pl.BlockSpec((rb, side_cols), lambda jj, i: (side_blk(jj, i), 0)))
        out_shape.append(jax.ShapeDtypeStruct((side_rows, side_cols), BF16))
        side_bytes = _nbytes((rb, side_cols), F32) + _nbytes((rb, side_cols), BF16)
    body = functools.partial(_ws_matmul_body, k_bounds=tuple(bounds), has_res=res is not None,
                             has_side=side_cast is not None, act=act, n_prompt_blocks=n_p,
                             chunk_rows=chunk_rows, transposed=transposed)
    n_io = 2 if res is not None else 1
    pipelined = (_nbytes((bm + ms, k_total), BF16) + _nbytes((k_total // n_p, bn), F32)
                 + n_io * _nbytes((bm + ms, bn), F32) + side_bytes)
    limit = min(VMEM_CAP, 2 * pipelined + _nbytes((2, k_total, bn), BF16) + 2 * _nbytes((bm, bn), F32))
    return pl.pallas_call(
        body, grid=(n_j + 1, n_p + 1), in_specs=in_specs, out_specs=out_specs, out_shape=out_shape,
        scratch_shapes=[scratch],
        compiler_params=pltpu.CompilerParams(
            dimension_semantics=("arbitrary", "arbitrary"), vmem_limit_bytes=int(limit)),
        name=name)(*args)


def _shift_rows(x, s, fill):
    row = lax.broadcasted_iota(jnp.int32, x.shape, 0)
    return jnp.where(row >= s, pltpu.roll(x, s, 0), fill)


def _softplus(x):
    return jnp.maximum(x, 0.0) + jnp.log1p(jnp.exp(-jnp.abs(x)))


def _gla_body(q_ref, k_ref, v_ref, g_ref, alr_ref, wa_ref, ba_ref, gn_ref, s0_ref,
              o_ref, s_out_ref, st_ref, *, chunk, n_chunks, n_pairs):
    n = pl.program_id(2)

    @pl.when(n == 0)
    def _():
        for p in range(n_pairs):
            st_ref[p] = s0_ref[0, p].T

    c = chunk
    qw, vw = 2 * DK_A, 2 * DV_A
    lane = lax.broadcasted_iota(jnp.int32, (c, qw), 1)
    head0 = lane < DK_A
    row2 = lax.broadcasted_iota(jnp.int32, (2 * c, 2 * c), 0)
    col2 = lax.broadcasted_iota(jnp.int32, (2 * c, 2 * c), 1)
    causal = ((row2 >= c) == (col2 >= c)) & (row2 >= col2)
    nt = (((1,), (1,)), ((), ()))
    tn = (((0,), (0,)), ((), ()))

    def stack_heads(z):
        return jnp.concatenate([jnp.where(head0, z, 0.0), jnp.where(head0, 0.0, z)], axis=0).astype(BF16)

    for ci in range(n_chunks):
        rows = pl.ds(ci * c, c)
        z = jnp.dot(alr_ref[rows, :].astype(BF16), wa_ref[...], preferred_element_type=F32) + ba_ref[...]
        b = -_softplus(-z) / GLA_TAU
        s = 1
        while s < c:
            b = b + _shift_rows(b, s, 0.0)
            s *= 2
        b_end = b[c - 1:c, :]
        q = q_ref[rows, :] * (DK_A ** -0.5)
        k = k_ref[rows, :]
        q_dec = q * jnp.exp(b)
        k_inv = k * jnp.exp(-b)
        k_end = k * jnp.exp(b_end - b)
        decay = jnp.exp(b_end)
        for p in range(n_pairs):
            ql = slice(p * qw, (p + 1) * qw)
            q_t, k_t, k_e = stack_heads(q_dec[:, ql]), stack_heads(k_inv[:, ql]), stack_heads(k_end[:, ql])
            v0 = v_ref[rows, pl.ds(p * vw, DV_A)]
            v1 = v_ref[rows, pl.ds(p * vw + DV_A, DV_A)]
            v_s = jnp.concatenate([v0, v1], axis=0).astype(BF16)
            att = lax.dot_general(q_t, k_t, nt, preferred_element_type=F32)
            att = jnp.where(causal, att, 0.0).astype(BF16)
            st = st_ref[p]
            o = (jnp.dot(att, v_s, preferred_element_type=F32)
                 + lax.dot_general(q_t, st.astype(BF16), nt, preferred_element_type=F32))
            st_ref[p] = decay[:, ql] * st + lax.dot_general(v_s, k_e, tn, preferred_element_type=F32)
            ms = jnp.mean(jnp.square(o), axis=-1, keepdims=True)
            o = (o * lax.rsqrt(ms + EPS) * gn_ref[...])
            g0 = g_ref[rows, pl.ds(p * vw, DV_A)]
            g1 = g_ref[rows, pl.ds(p * vw + DV_A, DV_A)]
            o_ref[rows, pl.ds(p * vw, DV_A)] = (o[:c] * (g0 * jax.nn.sigmoid(g0))).astype(o_ref.dtype)
            o_ref[rows, pl.ds(p * vw + DV_A, DV_A)] = (o[c:] * (g1 * jax.nn.sigmoid(g1))).astype(o_ref.dtype)

    @pl.when(n == pl.num_programs(2) - 1)
    def _():
        for p in range(n_pairs):
            s_out_ref[0, p] = st_ref[p].T


def _gla(hproj, alr, wa, ba, gn, s0, *, n_seq, seq_len, col_q, col_k, col_v, col_g):
    t_total = hproj.shape[0]
    h_a = s0.shape[1]
    pairs = h_a // 2
    pps = GLA_PAIRS_PER_STEP
    assert pairs % pps == 0
    chunk = min(CHUNK, seq_len)
    n_chunks = min(4, seq_len // chunk)
    tb = chunk * n_chunks
    nb = seq_len // tb
    qw, vw = 2 * DK_A * pps, 2 * DV_A * pps
    assert col_q % qw == 0 and col_k % qw == 0 and col_v % vw == 0 and col_g % vw == 0
    cq, ck, cv, cg = col_q // qw, col_k // qw, col_v // vw, col_g // vw
    row = lambda s, n: s * nb + n
    in_specs = [
        pl.BlockSpec((tb, qw), lambda s, p, n: (row(s, n), cq + p)),
        pl.BlockSpec((tb, qw), lambda s, p, n: (row(s, n), ck + p)),
        pl.BlockSpec((tb, vw), lambda s, p, n: (row(s, n), cv + p)),
        pl.BlockSpec((tb, vw), lambda s, p, n: (row(s, n), cg + p)),
        pl.BlockSpec((tb, LANES), lambda s, p, n: (row(s, n), 0)),
        pl.BlockSpec((LANES, qw), lambda s, p, n: (0, p)),
        pl.BlockSpec((1, qw), lambda s, p, n: (0, p)),
        pl.BlockSpec((1, DV_A), lambda s, p, n: (0, 0)),
        pl.BlockSpec((1, pps, 2 * DK_A, DV_A), lambda s, p, n: (s, p, 0, 0)),
    ]
    out_specs = [
        pl.BlockSpec((tb, vw), lambda s, p, n: (row(s, n), p)),
        pl.BlockSpec((1, pps, 2 * DK_A, DV_A), lambda s, p, n: (s, p, 0, 0)),
    ]
    o_a, s_new = pl.pallas_call(
        functools.partial(_gla_body, chunk=chunk, n_chunks=n_chunks, n_pairs=pps),
        grid=(n_seq, pairs // pps, nb), in_specs=in_specs, out_specs=out_specs,
        out_shape=[jax.ShapeDtypeStruct((t_total, h_a * DV_A), BF16),
                   jax.ShapeDtypeStruct((n_seq, pairs, 2 * DK_A, DV_A), F32)],
        scratch_shapes=[pltpu.VMEM((pps, DV_A, 2 * DK_A), F32)],
        compiler_params=pltpu.CompilerParams(
            dimension_semantics=("parallel", "parallel", "arbitrary"),
            vmem_limit_bytes=_vmem_limit(_nbytes((tb, 2 * qw + 2 * vw + LANES), F32), _nbytes((tb, vw), BF16))),
        name="gla")(hproj, hproj, hproj, hproj, alr, wa, ba, gn, s0.reshape(n_seq, pairs, 2 * DK_A, DV_A))
    return o_a, s_new.reshape(n_seq, h_a, DK_A, DV_A)


def _gmlp_body(u_ref, v_ref, w_ref, b_ref, o_ref, *vg_refs, c, dh):
    vg = jax.nn.gelu(v_ref[...])
    if vg_refs:
        vg_refs[0][...] = vg
    ug = jax.nn.gelu(u_ref[...])
    tril = (lax.broadcasted_iota(jnp.int32, (c, c), 0) >= lax.broadcasted_iota(jnp.int32, (c, c), 1))
    for h in range(w_ref.shape[0]):
        cols = pl.ds(h * dh, dh)
        w = jnp.where(tril, w_ref[h, :c, :c], 0.0).astype(BF16)
        mixed = jnp.dot(w, vg[:, h * dh:(h + 1) * dh].astype(BF16), preferred_element_type=F32) + b_ref[:c, cols]
        o_ref[:, cols] = (ug[:, h * dh:(h + 1) * dh] * mixed).astype(o_ref.dtype)


def _gmlp(hproj, w_sp, b_full, *, seq_len, col_u, col_v, want_v):
    t_total = hproj.shape[0]
    n_h, gc, _ = w_sp.shape
    w_b = b_full.shape[1]
    dh = w_b // n_h
    c = min(gc, seq_len)
    blk = lambda col: pl.BlockSpec((c, w_b), lambda r: (r, col // w_b))
    out_specs = [pl.BlockSpec((c, w_b), lambda r: (r, 0))]
    out_shape = [jax.ShapeDtypeStruct((t_total, w_b), BF16)]
    if want_v:
        out_specs.append(pl.BlockSpec((c, w_b), lambda r: (r, 0)))
        out_shape.append(jax.ShapeDtypeStruct((t_total, w_b), F32))
    outs = pl.pallas_call(
        functools.partial(_gmlp_body, c=c, dh=dh), grid=(t_total // c,),
        in_specs=[blk(col_u), blk(col_v),
                  pl.BlockSpec((n_h, gc, gc), lambda r: (0, 0, 0)),
                  pl.BlockSpec((gc, w_b), lambda r: (0, 0))],
        out_specs=out_specs, out_shape=out_shape,
        compiler_params=pltpu.CompilerParams(
            dimension_semantics=("parallel",),
            vmem_limit_bytes=_vmem_limit(4 * _nbytes((c, w_b), F32), _nbytes((n_h, gc, gc), F32))),
        name="gmlp")(hproj, hproj, w_sp, b_full)
    return (outs[0], outs[1]) if want_v else (outs[0], None)


def _lru_body(x_ref, gate_ref, cw_ref, cb_ref, wr_ref, br_ref, wi_ref, bi_ref, lam_ref, h0_ref, tail0_ref,
              o_ref, h_ref, tail_ref, *, dh):
    n = pl.program_id(1)

    @pl.when(n == 0)
    def _():
        h_ref[...] = h0_ref[...]
        tail_ref[...] = tail0_ref[...]

    tb = x_ref.shape[0]
    row8 = lax.broadcasted_iota(jnp.int32, (SUBLANES, dh), 0)
    for hb in range(x_ref.shape[1] // dh):
        cols = pl.ds(hb * dh, dh)
        x = x_ref[:, cols]
        tail = tail_ref[0, :, cols]

        def delayed(j):
            rolled = pltpu.roll(x, j, 0)
            head = jnp.where(row8 >= j, rolled[:SUBLANES], pltpu.roll(tail, j, 0))
            return jnp.concatenate([head, rolled[SUBLANES:]], axis=0)

        y = cb_ref[:, cols] + delayed(CONV_W - 1) * cw_ref[0:1, cols]
        for j in range(1, CONV_W - 1):
            y = y + delayed(CONV_W - 1 - j) * cw_ref[j:j + 1, cols]
        y = y + x * cw_ref[CONV_W - 1:CONV_W, cols]
        tail_ref[0, :, cols] = x[tb - SUBLANES:]

        yb = y.astype(BF16)
        r = jax.nn.sigmoid(jnp.dot(yb, wr_ref[hb].astype(BF16), preferred_element_type=F32) + br_ref[:, cols])
        ig = jax.nn.sigmoid(jnp.dot(yb, wi_ref[hb].astype(BF16), preferred_element_type=F32) + bi_ref[:, cols])
        log_a = -LRU_C * r * _softplus(-lam_ref[:, cols])
        a = jnp.exp(log_a)
        bt = jnp.sqrt(-jnp.tanh(log_a) * (a * a + 1.0)) * (ig * y)
        s = 1
        while s < tb:
            bt = a * _shift_rows(bt, s, 0.0) + bt
            a = a * _shift_rows(a, s, 1.0)
            s *= 2
        hseq = bt + a * h_ref[0, :, cols]
        h_ref[0, :, cols] = hseq[tb - 1:]
        o_ref[:, cols] = (hseq * jax.nn.gelu(gate_ref[:, cols])).astype(o_ref.dtype)


def _lru(hproj, cw, cb, wr, br, wi, bi, lam, h0, tail0, *, n_seq, seq_len, col_x, col_gate):
    t_total = hproj.shape[0]
    w_c = cw.shape[1]
    n_h, dh, _ = wr.shape
    tb = min(seq_len, 256)
    nb = seq_len // tb
    vec = lambda rows: pl.BlockSpec((rows, w_c), lambda s, n: (0, 0))
    wspec = pl.BlockSpec((n_h, dh, dh), lambda s, n: (0, 0, 0))
    hspec = pl.BlockSpec((1, 1, w_c), lambda s, n: (s, 0, 0))
    tspec = pl.BlockSpec((1, SUBLANES, w_c), lambda s, n: (s, 0, 0))
    o_c, h_new, tail = pl.pallas_call(
        functools.partial(_lru_body, dh=dh), grid=(n_seq, nb),
        in_specs=[pl.BlockSpec((tb, w_c), lambda s, n: (s * nb + n, col_x // w_c)),
                  pl.BlockSpec((tb, w_c), lambda s, n: (s * nb + n, col_gate // w_c)),
                  vec(CONV_W), vec(1), wspec, vec(1), wspec, vec(1), vec(1), hspec, tspec],
        out_specs=[pl.BlockSpec((tb, w_c), lambda s, n: (s * nb + n, 0)), hspec, tspec],
        out_shape=[jax.ShapeDtypeStruct((t_total, w_c), BF16),
                   jax.ShapeDtypeStruct((n_seq, 1, w_c), F32),
                   jax.ShapeDtypeStruct((n_seq, SUBLANES, w_c), F32)],
        compiler_params=pltpu.CompilerParams(
            dimension_semantics=("parallel", "arbitrary"),
            vmem_limit_bytes=_vmem_limit(4 * _nbytes((tb, w_c), F32))),
        name="conv_rglru")(hproj, hproj, cw, cb.reshape(1, w_c), wr, br.reshape(1, w_c), wi,
                           bi.reshape(1, w_c), lam.reshape(1, w_c), h0.reshape(n_seq, 1, w_c), tail0)
    return o_c, h_new.reshape(n_seq, w_c), tail


def _mixers(hp_a, hp_b, alr, s_gla, s_conv, s_lru, w, *, n_seq, seq_len, want_v):
    w_a = s_gla.shape[1] * DV_A
    qk_a = s_gla.shape[1] * DK_A
    w_b = w_c = s_lru.shape[1]
    o_a, s_gla_new = _gla(hp_a, alr, w["w_alpha2"], w["b_alpha"], w["g_onorm"], s_gla, n_seq=n_seq,
                          seq_len=seq_len, col_q=0, col_k=qk_a, col_v=2 * qk_a, col_g=2 * qk_a + w_a)
    o_b, vg = _gmlp(hp_b, w["w_spatial"], w["b_spatial"], seq_len=seq_len, col_u=0, col_v=w_b, want_v=want_v)
    tail0 = jnp.pad(s_conv, ((0, 0), (SUBLANES - (CONV_W - 1), 0), (0, 0)))
    o_c, s_lru_new, tail = _lru(hp_b, w["conv_w"], w["conv_b"], w["w_rgate"], w["b_rgate"], w["w_igate"],
                                w["b_igate"], w["lru_lambda"], s_lru, tail0, n_seq=n_seq, seq_len=seq_len,
                                col_x=2 * w_b, col_gate=2 * w_b + w_c)
    return [o_a, o_b, o_c], (s_gla_new, tail[:, SUBLANES - (CONV_W - 1):], s_lru_new, vg)


def _small_layer_weights(l, w_in_t, w_alpha2, b_alpha, g_onorm, w_spatial, b_spatial, conv_w, conv_b,
                         w_rgate, b_rgate, w_igate, b_igate, lru_lambda):
    d = w_in_t.shape[2]
    w_a, w_b = d // 2, d // 4
    qk_a = (w_a // DV_A) * DK_A
    c_alr = 2 * qk_a + 2 * w_a
    w_alr = jnp.pad(w_in_t[l, c_alr:c_alr + GLA_RANK, :], ((0, LANES - GLA_RANK), (0, 0)))
    w_alpha2_p = jnp.pad(w_alpha2[l], ((0, LANES - GLA_RANK), (0, 0))).astype(BF16)
    b_full = jnp.repeat(b_spatial[l].T, w_b // H_B, axis=1)
    return dict(
        w_alr=w_alr, w_alpha2=w_alpha2_p, b_alpha=b_alpha[l].reshape(1, qk_a),
        g_onorm=g_onorm[l].reshape(1, DV_A), w_spatial=w_spatial[l], b_spatial=b_full,
        conv_w=conv_w[l], conv_b=conv_b[l], w_rgate=w_rgate[l], b_rgate=b_rgate[l],
        w_igate=w_igate[l], b_igate=b_igate[l], lru_lambda=lru_lambda[l])


def kernel(x_prompt, x_sample, state_gla, state_conv, state_lru, ln1, w_in, w_alpha2, b_alpha, g_onorm, w_spatial, b_spatial, conv_w, conv_b, w_rgate, b_rgate, w_igate, b_igate, lru_lambda, w_out, ln2, w_up, w_down, ln_final):
    bp, tp, d = x_prompt.shape
    bs, ts, _ = x_sample.shape
    depth = w_in.shape[0]
    w_a, w_b, w_c = d // 2, d // 4, d // 4
    h_a = w_a // DV_A
    qk_a = h_a * DK_A
    n_a = 2 * qk_a + 2 * w_a
    n_b = 2 * w_b + 2 * w_c
    zero_gla = jnp.zeros((bp, h_a, DK_A, DV_A), F32)
    zero_conv = jnp.zeros((bp, CONV_W - 1, w_c), F32)
    zero_lru = jnp.zeros((bp, w_c), F32)
    xp = x_prompt.reshape(bp * tp, d)
    xs = x_sample.reshape(bs * ts, d)
    w_in_t = jnp.swapaxes(w_in, 1, 2)
    mm = _ws_matmul
    st_p, st_s = [], []
    for l in range(depth):
        w = _small_layer_weights(l, w_in_t, w_alpha2, b_alpha, g_onorm, w_spatial, b_spatial, conv_w, conv_b,
                                 w_rgate, b_rgate, w_igate, b_igate, lru_lambda)
        xn_p, alr_p = _rmsnorm(xp, ln1[l], BF16, w_proj=w["w_alr"])
        xn_s, alr_s = _rmsnorm(xs, ln1[l], BF16, w_proj=w["w_alr"])
        hpa_p, hpa_s = mm([xn_p], [xn_s], w_in_t, l, n_cols=n_a, transposed=True, name="in_proj_a")
        hpb_p, hpb_s = mm([xn_p], [xn_s], w_in_t, l, n_cols=n_b, transposed=True, w_row0=n_a + GLA_RANK,
                          name="in_proj_b")
        mix_p, new_p = _mixers(hpa_p, hpb_p, alr_p, zero_gla, zero_conv, zero_lru, w, n_seq=bp, seq_len=tp,
                               want_v=False)
        mix_s, new_s = _mixers(hpa_s, hpb_s, alr_s, state_gla[l], state_conv[l], state_lru[l], w, n_seq=bs,
                               seq_len=ts, want_v=True)
        st_p.append(new_p)
        st_s.append(new_s)
        xp, xs = mm(mix_p, mix_s, w_out, l, n_cols=d, res=(xp, xs), name="out_proj")
        xn_p = _rmsnorm(xp, ln2[l], BF16)
        xn_s = _rmsnorm(xs, ln2[l], BF16)
        hid_p, hid_s, w_down_bf = mm([xn_p], [xn_s], w_up, l, n_cols=w_up.shape[2], act="relu2",
                                     out_dtype=BF16, side_cast=(w_down, l), name="mlp_up")
        xp, xs = _matmul([hid_p], [hid_s], w_down_bf[None], 0, n_cols=d, bm=1024, bn=1024, bk=2048,
                         res=(xp, xs), name="mlp_down")
    y_prompt = _rmsnorm(xp, ln_final, F32).reshape(bp, tp, d)
    y_sample = _rmsnorm(xs, ln_final, F32).reshape(bs, ts, d)
    stack = lambda sts, i: jnp.stack([s[i] for s in sts])
    return (y_prompt, y_sample, stack(st_p, 0), stack(st_p, 1), stack(st_p, 2),
            stack(st_s, 0), stack(st_s, 1), stack(st_s, 2),
            jnp.stack([s[3].reshape(bs, ts, w_b) for s in st_s]))
```

```python
import functools

import jax
import jax.numpy as jnp
from jax import lax
from jax.experimental import pallas as pl
from jax.experimental.pallas import tpu as pltpu

F32 = jnp.float32
BF16 = jnp.bfloat16

DV_A = 128
DK_A = 64
GLA_RANK = 16
GLA_TAU = 16.0
CHUNK = 64
GMLP_CHUNK = 128
H_B = 8
H_C = 8
CONV_W = 4
LRU_C = 8.0
EPS = 1e-6

LANES = 128
SUBLANES = 8
V7X_VMEM_BYTES = 64 * 1024 * 1024
VMEM_CAP = V7X_VMEM_BYTES * 7 // 8

GLA_PAIRS_PER_STEP = 4


def _vmem_limit(*block_bytes):
    need = 4 * sum(block_bytes)
    return int(min(max(need, 16 * 1024 * 1024), VMEM_CAP))


def _nbytes(shape, dtype):
    n = 1
    for s in shape:
        n *= s
    return n * jnp.dtype(dtype).itemsize


def _rmsnorm_body(x_ref, g_ref, o_ref):
    x = x_ref[...]
    ms = jnp.mean(jnp.square(x), axis=-1, keepdims=True)
    o_ref[...] = (x * lax.rsqrt(ms + EPS) * g_ref[...]).astype(o_ref.dtype)


def _rmsnorm_proj_body(x_ref, g_ref, w_ref, o_ref, p_ref):
    x = x_ref[...]
    ms = jnp.mean(jnp.square(x), axis=-1, keepdims=True)
    y = (x * lax.rsqrt(ms + EPS) * g_ref[...]).astype(BF16)
    o_ref[...] = y
    p_ref[...] = lax.dot_general(y, w_ref[...].astype(BF16), (((1,), (1,)), ((), ())),
                                 preferred_element_type=F32)


def _rmsnorm(x, g, out_dtype, w_proj=None):
    t, d = x.shape
    tb = min(t, 256)
    grid = (t // tb,)
    x_spec = pl.BlockSpec((tb, d), lambda i: (i, 0))
    g_spec = pl.BlockSpec((1, d), lambda i: (0, 0))
    params = pltpu.CompilerParams(
        dimension_semantics=("parallel",),
        vmem_limit_bytes=_vmem_limit(_nbytes((tb, d), F32), _nbytes((tb, d), out_dtype)))
    if w_proj is None:
        return pl.pallas_call(
            _rmsnorm_body, grid=grid, in_specs=[x_spec, g_spec], out_specs=x_spec,
            out_shape=jax.ShapeDtypeStruct((t, d), out_dtype), compiler_params=params,
            name="rmsnorm")(x, g.reshape(1, d))
    n = w_proj.shape[0]
    return pl.pallas_call(
        _rmsnorm_proj_body, grid=grid,
        in_specs=[x_spec, g_spec, pl.BlockSpec((n, d), lambda i: (0, 0))],
        out_specs=[x_spec, pl.BlockSpec((tb, n), lambda i: (i, 0))],
        out_shape=[jax.ShapeDtypeStruct((t, d), BF16), jax.ShapeDtypeStruct((t, n), F32)],
        compiler_params=params, name="rmsnorm_proj")(x, g.reshape(1, d), w_proj)


def _matmul_body(*refs, k_bounds, has_res, act, nk, n_prompt_blocks):
    n_lhs = len(k_bounds)
    lhs_p, lhs_s = refs[:n_lhs], refs[n_lhs:2 * n_lhs]
    w_ref = refs[2 * n_lhs]
    res_p, res_s = (refs[2 * n_lhs + 1], refs[2 * n_lhs + 2]) if has_res else (None, None)
    o_p, o_s = refs[-2], refs[-1]
    i = pl.program_id(0)

    def run(lhs_refs, res_ref, o_ref):
        if nk > 1:
            @pl.when(pl.program_id(2) == 0)
            def _():
                o_ref[...] = res_ref[...] if has_res else jnp.zeros(o_ref.shape, o_ref.dtype)

        acc = None
        for a_ref, (k0, k1) in zip(lhs_refs, k_bounds):
            part = jnp.dot(a_ref[...], w_ref[k0:k1, :].astype(BF16), preferred_element_type=F32)
            acc = part if acc is None else acc + part
        if nk > 1:
            o_ref[...] += acc
        else:
            if has_res:
                acc = res_ref[...] + acc
            if act == "relu2":
                acc = jnp.square(jnp.maximum(acc, 0.0))
            o_ref[...] = acc.astype(o_ref.dtype)

    @pl.when(i < n_prompt_blocks)
    def _():
        run(lhs_p, res_p, o_p)

    @pl.when(i == n_prompt_blocks)
    def _():
        run(lhs_s, res_s, o_s)


def _matmul(lhs_p, lhs_s, w, layer, *, n_cols, bm, bn, bk=None, res=None, act=None, out_dtype=F32,
            name="matmul"):
    mp, ms = lhs_p[0].shape[0], lhs_s[0].shape[0]
    k_total = w.shape[1]
    widths = [a.shape[1] for a in lhs_p]
    assert sum(widths) == k_total and mp % bm == 0 and n_cols % bn == 0
    if bk is None:
        bk = k_total
    nk = k_total // bk
    assert nk * bk == k_total and (nk == 1 or (len(lhs_p) == 1 and out_dtype == F32 and act is None))
    n_p, n_j = mp // bm, n_cols // bn

    def p_row(i):
        return jnp.minimum(i, n_p - 1)

    def p_col(i, j):
        return jnp.where(i < n_p, j, n_j - 1)

    def s_col(i, j):
        return jnp.where(i == n_p, j, 0)

    if nk == 1:
        bounds, k0 = [], 0
        for wd in widths:
            bounds.append((k0, k0 + wd))
            k0 += wd
        specs_p = [pl.BlockSpec((bm, wd), lambda i, j, k: (p_row(i), 0)) for wd in widths]
        specs_s = [pl.BlockSpec((ms, wd), lambda i, j, k: (0, 0)) for wd in widths]
    else:
        bounds = [(0, bk)]
        specs_p = [pl.BlockSpec((bm, bk), lambda i, j, k: (p_row(i), jnp.where(i < n_p, k, nk - 1)))]
        specs_s = [pl.BlockSpec((ms, bk), lambda i, j, k: (0, jnp.where(i == n_p, k, 0)))]
    w_spec = pl.BlockSpec((None, bk, bn), lambda i, j, k: (layer, k, j))
    io_p = pl.BlockSpec((bm, bn), lambda i, j, k: (p_row(i), p_col(i, j)))
    io_s = pl.BlockSpec((ms, bn), lambda i, j, k: (0, s_col(i, j)))
    in_specs = specs_p + specs_s + [w_spec]
    args = list(lhs_p) + list(lhs_s) + [w]
    if res is not None:
        in_specs += [io_p, io_s]
        args += list(res)
    body = functools.partial(_matmul_body, k_bounds=tuple(bounds), has_res=res is not None, act=act, nk=nk,
                             n_prompt_blocks=n_p)
    n_io = 2 if res is not None else 1
    limit = _vmem_limit(_nbytes((bm + ms, bk), BF16), _nbytes((bk, bn), F32),
                        n_io * _nbytes((bm + ms, bn), F32))
    return pl.pallas_call(
        body, grid=(n_p + 1, n_j, nk), in_specs=in_specs, out_specs=[io_p, io_s],
        out_shape=[jax.ShapeDtypeStruct((mp, n_cols), out_dtype), jax.ShapeDtypeStruct((ms, n_cols), out_dtype)],
        compiler_params=pltpu.CompilerParams(
            dimension_semantics=("arbitrary", "arbitrary", "arbitrary"), vmem_limit_bytes=limit),
        name=name)(*args)


def _ws_matmul_body(*refs, k_bounds, has_res, has_side, act, n_prompt_blocks, chunk_rows, transposed):
    n_lhs = len(k_bounds)
    lhs_p, lhs_s = refs[:n_lhs], refs[n_lhs:2 * n_lhs]
    wc_ref = refs[2 * n_lhs]
    res_p, res_s = (refs[2 * n_lhs + 1], refs[2 * n_lhs + 2]) if has_res else (None, None)
    w_scr = refs[-1]
    if has_side:
        side_in, side_out = refs[-5], refs[-2]
        o_p, o_s = refs[-4], refs[-3]
    else:
        o_p, o_s = refs[-3], refs[-2]
    jj, i = pl.program_id(0), pl.program_id(1)
    slot_load = jj % 2
    slot_use = 1 - slot_load

    def stage():
        r0 = pl.multiple_of(jnp.minimum(i, n_prompt_blocks - 1) * chunk_rows, chunk_rows)
        chunk = wc_ref[0] if len(wc_ref.shape) == 3 else wc_ref[...]
        w_scr[slot_load, pl.ds(r0, chunk_rows), :] = chunk.astype(BF16)

    def run(lhs_refs, res_ref, o_ref):
        stage()
        acc = None
        for a_ref, (k0, k1) in zip(lhs_refs, k_bounds):
            if transposed:
                part = lax.dot_general(a_ref[...], w_scr[slot_use, :, k0:k1], (((1,), (1,)), ((), ())),
                                       preferred_element_type=F32)
            else:
                part = jnp.dot(a_ref[...], w_scr[slot_use, k0:k1, :], preferred_element_type=F32)
            acc = part if acc is None else acc + part
        if has_res:
            acc = res_ref[...] + acc
        if act == "relu2":
            acc = jnp.square(jnp.maximum(acc, 0.0))
        o_ref[...] = acc.astype(o_ref.dtype)

    @pl.when(jj == 0)
    def _():
        stage()

    @pl.when((jj > 0) & (i < n_prompt_blocks))
    def _():
        if has_side:
            side_out[...] = side_in[...].astype(side_out.dtype)
        run(lhs_p, res_p, o_p)

    @pl.when((jj > 0) & (i == n_prompt_blocks))
    def _():
        run(lhs_s, res_s, o_s)


def _ws_matmul(lhs_p, lhs_s, w, layer, *, n_cols, transposed=False, w_row0=0, bm=512, bn=1024, res=None,
               act=None, out_dtype=F32, side_cast=None, name="ws_matmul"):
    mp, ms = lhs_p[0].shape[0], lhs_s[0].shape[0]
    k_total = w.shape[2] if transposed else w.shape[1]
    widths = [a.shape[1] for a in lhs_p]
    assert sum(widths) == k_total and mp % bm == 0 and n_cols % bn == 0
    n_p, n_j = mp // bm, n_cols // bn
    bounds, k0 = [], 0
    for wd in widths:
        bounds.append((k0, k0 + wd))
        k0 += wd

    def row_blk(jj, i):
        return jnp.where(jj == 0, 0, jnp.minimum(i, n_p - 1))

    def col_use(jj):
        return jnp.maximum(jj - 1, 0)

    def chunk(jj, i):
        return jnp.where(jj < n_j, jnp.minimum(i, n_p - 1), n_p - 1)

    def col_load(jj):
        return jnp.minimum(jj, n_j - 1)

    if transposed:
        chunk_rows = bn // n_p
        w_spec = pl.BlockSpec(
            (pl.Element(1), pl.Element(chunk_rows), pl.Element(k_total)),
            lambda jj, i: (layer, pl.multiple_of(
                w_row0 + (col_load(jj) * n_p + chunk(jj, i)) * chunk_rows, SUBLANES), 0))
        scratch = pltpu.VMEM((2, bn, k_total), BF16)
    else:
        chunk_rows = k_total // n_p
        w_spec = pl.BlockSpec((None, chunk_rows, bn), lambda jj, i: (layer, chunk(jj, i), col_load(jj)))
        scratch = pltpu.VMEM((2, k_total, bn), BF16)
    assert chunk_rows * n_p == (bn if transposed else k_total) and chunk_rows % 16 == 0
    specs_p = [pl.BlockSpec((bm, wd), lambda jj, i: (row_blk(jj, i), 0)) for wd in widths]
    specs_s = [pl.BlockSpec((ms, wd), lambda jj, i: (0, 0)) for wd in widths]
    io_p = pl.BlockSpec((bm, bn), lambda jj, i: (row_blk(jj, i), col_use(jj)))
    io_s = pl.BlockSpec((ms, bn), lambda jj, i: (0, col_use(jj)))
    in_specs = specs_p + specs_s + [w_spec]
    args = list(lhs_p) + list(lhs_s) + [w]
    if res is not None:
        in_specs += [io_p, io_s]
        args += list(res)
    out_specs = [io_p, io_s]
    out_shape = [jax.ShapeDtypeStruct((mp, n_cols), out_dtype), jax.ShapeDtypeStruct((ms, n_cols), out_dtype)]
    side_bytes = 0
    if side_cast is not None:
        side, side_layer = side_cast
        side_rows, side_cols = side.shape[1], side.shape[2]
        rb = side_rows // (n_j * n_p)
        assert rb * n_j * n_p == side_rows and rb % 16 == 0

        def side_blk(jj, i):
            return jnp.where(jj == 0, 0, (jj - 1) * n_p + jnp.minimum(i, n_p - 1))

        in_specs.append(pl.BlockSpec((None, rb, side_cols), lambda jj, i: (side_layer, side_blk(jj, i), 0)))
        args.append(side)
        out_specs.append(pl.BlockSpec((rb, side_cols), lambda jj, i: (side_blk(jj, i), 0)))
        out_shape.append(jax.ShapeDtypeStruct((side_rows, side_cols), BF16))
        side_bytes = _nbytes((rb, side_cols), F32) + _nbytes((rb, side_cols), BF16)
    body = functools.partial(_ws_matmul_body, k_bounds=tuple(bounds), has_res=res is not None,
                             has_side=side_cast is not None, act=act, n_prompt_blocks=n_p,
                             chunk_rows=chunk_rows, transposed=transposed)
    n_io = 2 if res is not None else 1
    pipelined = (_nbytes((bm + ms, k_total), BF16) + _nbytes((k_total // n_p, bn), F32)
                 + n_io * _nbytes((bm + ms, bn), F32) + side_bytes)
    limit = min(VMEM_CAP, 2 * pipelined + _nbytes((2, k_total, bn), BF16) + 2 * _nbytes((bm, bn), F32))
    return pl.pallas_call(
        body, grid=(n_j + 1, n_p + 1), in_specs=in_specs, out_specs=out_specs, out_shape=out_shape,
        scratch_shapes=[scratch],
        compiler_params=pltpu.CompilerParams(
            dimension_semantics=("arbitrary", "arbitrary"), vmem_limit_bytes=int(limit)),
        name=name)(*args)


def _linear_scan_rows(a, b, carry):
    t, lanes = b.shape
    g = t // SUBLANES
    b3 = b.reshape(g, SUBLANES, lanes)
    a3 = None if a is None else a.reshape(g, SUBLANES, lanes)
    sub = lax.broadcasted_iota(jnp.int32, b3.shape, 1)
    s = 1
    while s < SUBLANES:
        keep = sub >= s
        b_prev = jnp.where(keep, pltpu.roll(b3, s, 1), 0.0)
        if a3 is None:
            b3 = b3 + b_prev
        else:
            b3 = a3 * b_prev + b3
            a3 = a3 * jnp.where(keep, pltpu.roll(a3, s, 1), 1.0)
        s *= 2
    outs = []
    for j in range(g):
        h = b3[j] + carry if a3 is None else b3[j] + a3[j] * carry
        outs.append(h)
        carry = h[SUBLANES - 1:, :]
    return jnp.concatenate(outs, axis=0), carry


def _softplus(x):
    return jnp.maximum(x, 0.0) + jnp.log1p(jnp.exp(-jnp.abs(x)))


def _gla_body(q_ref, k_ref, v_ref, g_ref, alr_ref, wa_ref, ba_ref, gn_ref, s0_ref,
              o_ref, s_out_ref, st_ref, *, chunk, n_chunks, n_pairs):
    n = pl.program_id(2)

    @pl.when(n == 0)
    def _():
        for p in range(n_pairs):
            st_ref[p] = s0_ref[0, p].T

    c = chunk
    qw, vw = 2 * DK_A, 2 * DV_A
    lane = lax.broadcasted_iota(jnp.int32, (c, qw), 1)
    head0 = lane < DK_A
    row2 = lax.broadcasted_iota(jnp.int32, (2 * c, 2 * c), 0)
    col2 = lax.broadcasted_iota(jnp.int32, (2 * c, 2 * c), 1)
    causal = ((row2 >= c) == (col2 >= c)) & (row2 >= col2)
    nt = (((1,), (1,)), ((), ()))
    tn = (((0,), (0,)), ((), ()))

    def stack_heads(z):
        return jnp.concatenate([jnp.where(head0, z, 0.0), jnp.where(head0, 0.0, z)], axis=0).astype(BF16)

    for ci in range(n_chunks):
        rows = pl.ds(ci * c, c)
        z = jnp.dot(alr_ref[rows, :].astype(BF16), wa_ref[...], preferred_element_type=F32) + ba_ref[...]
        log_alpha = -(jnp.maximum(-z, 0.0) + jnp.log(1.0 + jnp.exp(-jnp.abs(z)))) / GLA_TAU
        b, b_end = _linear_scan_rows(None, log_alpha, jnp.zeros((1, log_alpha.shape[1]), F32))
        q = q_ref[rows, :] * (DK_A ** -0.5)
        k = k_ref[rows, :]
        q_dec = q * jnp.exp(b)
        k_inv = k * jnp.exp(-b)
        k_end = k * jnp.exp(b_end - b)
        decay = jnp.exp(b_end)
        for p in range(n_pairs):
            ql = slice(p * qw, (p + 1) * qw)
            q_t, k_t, k_e = stack_heads(q_dec[:, ql]), stack_heads(k_inv[:, ql]), stack_heads(k_end[:, ql])
            v0 = v_ref[rows, pl.ds(p * vw, DV_A)]
            v1 = v_ref[rows, pl.ds(p * vw + DV_A, DV_A)]
            v_s = jnp.concatenate([v0, v1], axis=0).astype(BF16)
            att = lax.dot_general(q_t, k_t, nt, preferred_element_type=F32)
            att = jnp.where(causal, att, 0.0).astype(BF16)
            st = st_ref[p]
            o = (jnp.dot(att, v_s, preferred_element_type=F32)
                 + lax.dot_general(q_t, st.astype(BF16), nt, preferred_element_type=F32))
            st_ref[p] = decay[:, ql] * st + lax.dot_general(v_s, k_e, tn, preferred_element_type=F32)
            ms = jnp.mean(jnp.square(o), axis=-1, keepdims=True)
            o = (o * lax.rsqrt(ms + EPS) * gn_ref[...])
            g0 = g_ref[rows, pl.ds(p * vw, DV_A)]
            g1 = g_ref[rows, pl.ds(p * vw + DV_A, DV_A)]
            o_ref[rows, pl.ds(p * vw, DV_A)] = (o[:c] * (g0 * jax.nn.sigmoid(g0))).astype(o_ref.dtype)
            o_ref[rows, pl.ds(p * vw + DV_A, DV_A)] = (o[c:] * (g1 * jax.nn.sigmoid(g1))).astype(o_ref.dtype)

    @pl.when(n == pl.num_programs(2) - 1)
    def _():
        for p in range(n_pairs):
            s_out_ref[0, p] = st_ref[p].T


def _gla(hproj, alr, wa, ba, gn, s0, *, n_seq, seq_len, col_q, col_k, col_v, col_g):
    t_total = hproj.shape[0]
    h_a = s0.shape[1]
    pairs = h_a // 2
    pps = GLA_PAIRS_PER_STEP
    assert pairs % pps == 0
    chunk = min(CHUNK, seq_len)
    n_chunks = min(4, seq_len // chunk)
    tb = chunk * n_chunks
    nb = seq_len // tb
    qw, vw = 2 * DK_A * pps, 2 * DV_A * pps
    assert col_q % qw == 0 and col_k % qw == 0 and col_v % vw == 0 and col_g % vw == 0
    cq, ck, cv, cg = col_q // qw, col_k // qw, col_v // vw, col_g // vw
    row = lambda s, n: s * nb + n
    in_specs = [
        pl.BlockSpec((tb, qw), lambda s, p, n: (row(s, n), cq + p)),
        pl.BlockSpec((tb, qw), lambda s, p, n: (row(s, n), ck + p)),
        pl.BlockSpec((tb, vw), lambda s, p, n: (row(s, n), cv + p)),
        pl.BlockSpec((tb, vw), lambda s, p, n: (row(s, n), cg + p)),
        pl.BlockSpec((tb, LANES), lambda s, p, n: (row(s, n), 0)),
        pl.BlockSpec((LANES, qw), lambda s, p, n: (0, p)),
        pl.BlockSpec((1, qw), lambda s, p, n: (0, p)),
        pl.BlockSpec((1, DV_A), lambda s, p, n: (0, 0)),
        pl.BlockSpec((1, pps, 2 * DK_A, DV_A), lambda s, p, n: (s, p, 0, 0)),
    ]
    out_specs = [
        pl.BlockSpec((tb, vw), lambda s, p, n: (row(s, n), p)),
        pl.BlockSpec((1, pps, 2 * DK_A, DV_A), lambda s, p, n: (s, p, 0, 0)),
    ]
    o_a, s_new = pl.pallas_call(
        functools.partial(_gla_body, chunk=chunk, n_chunks=n_chunks, n_pairs=pps),
        grid=(n_seq, pairs // pps, nb), in_specs=in_specs, out_specs=out_specs,
        out_shape=[jax.ShapeDtypeStruct((t_total, h_a * DV_A), BF16),
                   jax.ShapeDtypeStruct((n_seq, pairs, 2 * DK_A, DV_A), F32)],
        scratch_shapes=[pltpu.VMEM((pps, DV_A, 2 * DK_A), F32)],
        compiler_params=pltpu.CompilerParams(
            dimension_semantics=("parallel", "parallel", "arbitrary"),
            vmem_limit_bytes=_vmem_limit(_nbytes((tb, 2 * qw + 2 * vw + LANES), F32), _nbytes((tb, vw), BF16))),
        name="gla")(hproj, hproj, hproj, hproj, alr, wa, ba, gn, s0.reshape(n_seq, pairs, 2 * DK_A, DV_A))
    return o_a, s_new.reshape(n_seq, h_a, DK_A, DV_A)


def _gmlp_body(u_ref, v_ref, w_ref, b_ref, o_ref, *vg_refs, c, dh):
    vg = jax.nn.gelu(v_ref[...])
    if vg_refs:
        vg_refs[0][...] = vg
    ug = jax.nn.gelu(u_ref[...])
    tril = (lax.broadcasted_iota(jnp.int32, (c, c), 0) >= lax.broadcasted_iota(jnp.int32, (c, c), 1))
    for h in range(w_ref.shape[0]):
        cols = pl.ds(h * dh, dh)
        w = jnp.where(tril, w_ref[h, :c, :c], 0.0).astype(BF16)
        mixed = jnp.dot(w, vg[:, h * dh:(h + 1) * dh].astype(BF16), preferred_element_type=F32) + b_ref[:c, cols]
        o_ref[:, cols] = (ug[:, h * dh:(h + 1) * dh] * mixed).astype(o_ref.dtype)


def _gmlp(hproj, w_sp, b_full, *, seq_len, col_u, col_v, want_v):
    t_total = hproj.shape[0]
    n_h, gc, _ = w_sp.shape
    w_b = b_full.shape[1]
    dh = w_b // n_h
    c = min(gc, seq_len)
    blk = lambda col: pl.BlockSpec((c, w_b), lambda r: (r, col // w_b))
    out_specs = [pl.BlockSpec((c, w_b), lambda r: (r, 0))]
    out_shape = [jax.ShapeDtypeStruct((t_total, w_b), BF16)]
    if want_v:
        out_specs.append(pl.BlockSpec((c, w_b), lambda r: (r, 0)))
        out_shape.append(jax.ShapeDtypeStruct((t_total, w_b), F32))
    outs = pl.pallas_call(
        functools.partial(_gmlp_body, c=c, dh=dh), grid=(t_total // c,),
        in_specs=[blk(col_u), blk(col_v),
                  pl.BlockSpec((n_h, gc, gc), lambda r: (0, 0, 0)),
                  pl.BlockSpec((gc, w_b), lambda r: (0, 0))],
        out_specs=out_specs, out_shape=out_shape,
        compiler_params=pltpu.CompilerParams(
            dimension_semantics=("parallel",),
            vmem_limit_bytes=_vmem_limit(4 * _nbytes((c, w_b), F32), _nbytes((n_h, gc, gc), F32))),
        name="gmlp")(hproj, hproj, w_sp, b_full)
    return (outs[0], outs[1]) if want_v else (outs[0], None)


def _lru_body(x_ref, gate_ref, cw_ref, cb_ref, wr_ref, br_ref, wi_ref, bi_ref, lam_ref, h0_ref, tail0_ref,
              o_ref, h_ref, tail_ref, *, dh):
    n = pl.program_id(1)

    @pl.when(n == 0)
    def _():
        h_ref[...] = h0_ref[...]
        tail_ref[...] = tail0_ref[...]

    tb = x_ref.shape[0]
    row8 = lax.broadcasted_iota(jnp.int32, (SUBLANES, dh), 0)
    for hb in range(x_ref.shape[1] // dh):
        cols = pl.ds(hb * dh, dh)
        x = x_ref[:, cols]
        tail = tail_ref[0, :, cols]

        def delayed(j):
            rolled = pltpu.roll(x, j, 0)
            head = jnp.where(row8 >= j, rolled[:SUBLANES], pltpu.roll(tail, j, 0))
            return jnp.concatenate([head, rolled[SUBLANES:]], axis=0)

        y = cb_ref[:, cols] + delayed(CONV_W - 1) * cw_ref[0:1, cols]
        for j in range(1, CONV_W - 1):
            y = y + delayed(CONV_W - 1 - j) * cw_ref[j:j + 1, cols]
        y = y + x * cw_ref[CONV_W - 1:CONV_W, cols]
        tail_ref[0, :, cols] = x[tb - SUBLANES:]

        yb = y.astype(BF16)
        r = jax.nn.sigmoid(jnp.dot(yb, wr_ref[hb].astype(BF16), preferred_element_type=F32) + br_ref[:, cols])
        ig = jax.nn.sigmoid(jnp.dot(yb, wi_ref[hb].astype(BF16), preferred_element_type=F32) + bi_ref[:, cols])
        log_a = -LRU_C * r * _softplus(-lam_ref[:, cols])
        a = jnp.exp(log_a)
        bt = jnp.sqrt(-jnp.tanh(log_a) * (a * a + 1.0)) * (ig * y)
        hseq, h_last = _linear_scan_rows(a, bt, h_ref[0, :, cols])
        h_ref[0, :, cols] = h_last
        o_ref[:, cols] = (hseq * jax.nn.gelu(gate_ref[:, cols])).astype(o_ref.dtype)


def _lru(hproj, cw, cb, wr, br, wi, bi, lam, h0, tail0, *, n_seq, seq_len, col_x, col_gate):
    t_total = hproj.shape[0]
    w_c = cw.shape[1]
    n_h, dh, _ = wr.shape
    tb = min(seq_len, 256)
    nb = seq_len // tb
    vec = lambda rows: pl.BlockSpec((rows, w_c), lambda s, n: (0, 0))
    wspec = pl.BlockSpec((n_h, dh, dh), lambda s, n: (0, 0, 0))
    hspec = pl.BlockSpec((1, 1, w_c), lambda s, n: (s, 0, 0))
    tspec = pl.BlockSpec((1, SUBLANES, w_c), lambda s, n: (s, 0, 0))
    o_c, h_new, tail = pl.pallas_call(
        functools.partial(_lru_body, dh=dh), grid=(n_seq, nb),
        in_specs=[pl.BlockSpec((tb, w_c), lambda s, n: (s * nb + n, col_x // w_c)),
                  pl.BlockSpec((tb, w_c), lambda s, n: (s * nb + n, col_gate // w_c)),
                  vec(CONV_W), vec(1), wspec, vec(1), wspec, vec(1), vec(1), hspec, tspec],
        out_specs=[pl.BlockSpec((tb, w_c), lambda s, n: (s * nb + n, 0)), hspec, tspec],
        out_shape=[jax.ShapeDtypeStruct((t_total, w_c), BF16),
                   jax.ShapeDtypeStruct((n_seq, 1, w_c), F32),
                   jax.ShapeDtypeStruct((n_seq, SUBLANES, w_c), F32)],
        compiler_params=pltpu.CompilerParams(
            dimension_semantics=("parallel", "arbitrary"),
            vmem_limit_bytes=_vmem_limit(4 * _nbytes((tb, w_c), F32))),
        name="conv_rglru")(hproj, hproj, cw, cb.reshape(1, w_c), wr, br.reshape(1, w_c), wi,
                           bi.reshape(1, w_c), lam.reshape(1, w_c), h0.reshape(n_seq, 1, w_c), tail0)
    return o_c, h_new.reshape(n_seq, w_c), tail


def _mixers(hp_a, hp_b, alr, s_gla, s_conv, s_lru, w, *, n_seq, seq_len, want_v):
    w_a = s_gla.shape[1] * DV_A
    qk_a = s_gla.shape[1] * DK_A
    w_b = w_c = s_lru.shape[1]
    o_a, s_gla_new = _gla(hp_a, alr, w["w_alpha2"], w["b_alpha"], w["g_onorm"], s_gla, n_seq=n_seq,
                          seq_len=seq_len, col_q=0, col_k=qk_a, col_v=2 * qk_a, col_g=2 * qk_a + w_a)
    o_b, vg = _gmlp(hp_b, w["w_spatial"], w["b_spatial"], seq_len=seq_len, col_u=0, col_v=w_b, want_v=want_v)
    tail0 = jnp.pad(s_conv, ((0, 0), (SUBLANES - (CONV_W - 1), 0), (0, 0)))
    o_c, s_lru_new, tail = _lru(hp_b, w["conv_w"], w["conv_b"], w["w_rgate"], w["b_rgate"], w["w_igate"],
                                w["b_igate"], w["lru_lambda"], s_lru, tail0, n_seq=n_seq, seq_len=seq_len,
                                col_x=2 * w_b, col_gate=2 * w_b + w_c)
    return [o_a, o_b, o_c], (s_gla_new, tail[:, SUBLANES - (CONV_W - 1):], s_lru_new, vg)


def _small_layer_weights(l, w_in_t, w_alpha2, b_alpha, g_onorm, w_spatial, b_spatial, conv_w, conv_b,
                         w_rgate, b_rgate, w_igate, b_igate, lru_lambda):
    d = w_in_t.shape[2]
    w_a, w_b = d // 2, d // 4
    qk_a = (w_a // DV_A) * DK_A
    c_alr = 2 * qk_a + 2 * w_a
    w_alr = jnp.pad(w_in_t[l, c_alr:c_alr + GLA_RANK, :], ((0, LANES - GLA_RANK), (0, 0)))
    w_alpha2_p = jnp.pad(w_alpha2[l], ((0, LANES - GLA_RANK), (0, 0))).astype(BF16)
    b_full = jnp.repeat(b_spatial[l].T, w_b // H_B, axis=1)
    return dict(
        w_alr=w_alr, w_alpha2=w_alpha2_p, b_alpha=b_alpha[l].reshape(1, qk_a),
        g_onorm=g_onorm[l].reshape(1, DV_A), w_spatial=w_spatial[l], b_spatial=b_full,
        conv_w=conv_w[l], conv_b=conv_b[l], w_rgate=w_rgate[l], b_rgate=b_rgate[l],
        w_igate=w_igate[l], b_igate=b_igate[l], lru_lambda=lru_lambda[l])


def kernel(x_prompt, x_sample, state_gla, state_conv, state_lru, ln1, w_in, w_alpha2, b_alpha, g_onorm, w_spatial, b_spatial, conv_w, conv_b, w_rgate, b_rgate, w_igate, b_igate, lru_lambda, w_out, ln2, w_up, w_down, ln_final):
    bp, tp, d = x_prompt.shape
    bs, ts, _ = x_sample.shape
    depth = w_in.shape[0]
    w_a, w_b, w_c = d // 2, d // 4, d // 4
    h_a = w_a // DV_A
    qk_a = h_a * DK_A
    n_a = 2 * qk_a + 2 * w_a
    n_b = 2 * w_b + 2 * w_c
    zero_gla = jnp.zeros((bp, h_a, DK_A, DV_A), F32)
    zero_conv = jnp.zeros((bp, CONV_W - 1, w_c), F32)
    zero_lru = jnp.zeros((bp, w_c), F32)
    xp = x_prompt.reshape(bp * tp, d)
    xs = x_sample.reshape(bs * ts, d)
    w_in_t = jnp.swapaxes(w_in, 1, 2)
    mm = _ws_matmul
    st_p, st_s = [], []
    for l in range(depth):
        w = _small_layer_weights(l, w_in_t, w_alpha2, b_alpha, g_onorm, w_spatial, b_spatial, conv_w, conv_b,
                                 w_rgate, b_rgate, w_igate, b_igate, lru_lambda)
        xn_p, alr_p = _rmsnorm(xp, ln1[l], BF16, w_proj=w["w_alr"])
        xn_s, alr_s = _rmsnorm(xs, ln1[l], BF16, w_proj=w["w_alr"])
        hpa_p, hpa_s = mm([xn_p], [xn_s], w_in_t, l, n_cols=n_a, transposed=True, bm=1024, name="in_proj_a")
        hpb_p, hpb_s = mm([xn_p], [xn_s], w_in_t, l, n_cols=n_b, transposed=True, w_row0=n_a + GLA_RANK,
                          bm=1024, name="in_proj_b")
        mix_p, new_p = _mixers(hpa_p, hpb_p, alr_p, zero_gla, zero_conv, zero_lru, w, n_seq=bp, seq_len=tp,
                               want_v=False)
        mix_s, new_s = _mixers(hpa_s, hpb_s, alr_s, state_gla[l], state_conv[l], state_lru[l], w, n_seq=bs,
                               seq_len=ts, want_v=True)
        st_p.append(new_p)
        st_s.append(new_s)
        xp, xs = mm(mix_p, mix_s, w_out, l, n_cols=d, res=(xp, xs), name="out_proj")
        xn_p = _rmsnorm(xp, ln2[l], BF16)
        xn_s = _rmsnorm(xs, ln2[l], BF16)
        hid_p, hid_s, w_down_bf = mm([xn_p], [xn_s], w_up, l, n_cols=w_up.shape[2], act="relu2",
                                     out_dtype=BF16, side_cast=(w_down, l), name="mlp_up")
        xp, xs = _matmul([hid_p], [hid_s], w_down_bf[None], 0, n_cols=d, bm=1024, bn=1024, bk=2048,
                         res=(xp, xs), name="mlp_down")
    y_prompt = _rmsnorm(xp, ln_final, F32).reshape(bp, tp, d)
    y_sample = _rmsnorm(xs, ln_final, F32).reshape(bs, ts, d)
    stack = lambda sts, i: jnp.stack([s[i] for s in sts])
    return (y_prompt, y_sample, stack(st_p, 0), stack(st_p, 1), stack(st_p, 2),
            stack(st_s, 0), stack(st_s, 1), stack(st_s, 2),
            jnp.stack([s[3].reshape(bs, ts, w_b) for s in st_s]))
```

```python
import functools

import jax
import jax.numpy as jnp
from jax import lax
from jax.experimental import pallas as pl
from jax.experimental.pallas import tpu as pltpu

F32 = jnp.float32
BF16 = jnp.bfloat16

DV_A = 128
DK_A = 64
GLA_RANK = 16
GLA_TAU = 16.0
CHUNK = 64
GMLP_CHUNK = 128
H_B = 8
H_C = 8
CONV_W = 4
LRU_C = 8.0
EPS = 1e-6

LANES = 128
SUBLANES = 8
V7X_VMEM_BYTES = 64 * 1024 * 1024
VMEM_CAP = V7X_VMEM_BYTES * 7 // 8

GLA_PAIRS_PER_STEP = 8


def _vmem_limit(*block_bytes):
    need = 4 * sum(block_bytes)
    return int(min(max(need, 16 * 1024 * 1024), VMEM_CAP))


def _nbytes(shape, dtype):
    n = 1
    for s in shape:
        n *= s
    return n * jnp.dtype(dtype).itemsize


def _rmsnorm_body(x_ref, g_ref, o_ref):
    x = x_ref[...]
    ms = jnp.mean(jnp.square(x), axis=-1, keepdims=True)
    o_ref[...] = (x * lax.rsqrt(ms + EPS) * g_ref[...]).astype(o_ref.dtype)


def _rmsnorm_proj_body(x_ref, g_ref, w_ref, o_ref, p_ref):
    x = x_ref[...]
    ms = jnp.mean(jnp.square(x), axis=-1, keepdims=True)
    y = (x * lax.rsqrt(ms + EPS) * g_ref[...]).astype(BF16)
    o_ref[...] = y
    p_ref[...] = lax.dot_general(y, w_ref[...].astype(BF16), (((1,), (1,)), ((), ())),
                                 preferred_element_type=F32)


def _rmsnorm(x, g, out_dtype, w_proj=None):
    t, d = x.shape
    tb = min(t, 512)
    grid = (t // tb,)
    x_spec = pl.BlockSpec((tb, d), lambda i: (i, 0))
    g_spec = pl.BlockSpec((1, d), lambda i: (0, 0))
    params = pltpu.CompilerParams(
        dimension_semantics=("parallel",),
        vmem_limit_bytes=_vmem_limit(_nbytes((tb, d), F32), _nbytes((tb, d), out_dtype)))
    if w_proj is None:
        return pl.pallas_call(
            _rmsnorm_body, grid=grid, in_specs=[x_spec, g_spec], out_specs=x_spec,
            out_shape=jax.ShapeDtypeStruct((t, d), out_dtype), compiler_params=params,
            name="rmsnorm")(x, g.reshape(1, d))
    n = w_proj.shape[0]
    return pl.pallas_call(
        _rmsnorm_proj_body, grid=grid,
        in_specs=[x_spec, g_spec, pl.BlockSpec((n, d), lambda i: (0, 0))],
        out_specs=[x_spec, pl.BlockSpec((tb, n), lambda i: (i, 0))],
        out_shape=[jax.ShapeDtypeStruct((t, d), BF16), jax.ShapeDtypeStruct((t, n), F32)],
        compiler_params=params, name="rmsnorm_proj")(x, g.reshape(1, d), w_proj)


def _matmul_body(*refs, k_bounds, has_res, act, nk, n_prompt_blocks):
    n_lhs = len(k_bounds)
    lhs_p, lhs_s = refs[:n_lhs], refs[n_lhs:2 * n_lhs]
    w_ref = refs[2 * n_lhs]
    res_p, res_s = (refs[2 * n_lhs + 1], refs[2 * n_lhs + 2]) if has_res else (None, None)
    o_p, o_s = refs[-2], refs[-1]
    i = pl.program_id(0)

    def run(lhs_refs, res_ref, o_ref):
        if nk > 1:
            @pl.when(pl.program_id(2) == 0)
            def _():
                o_ref[...] = res_ref[...] if has_res else jnp.zeros(o_ref.shape, o_ref.dtype)

        acc = None
        for a_ref, (k0, k1) in zip(lhs_refs, k_bounds):
            part = jnp.dot(a_ref[...], w_ref[k0:k1, :].astype(BF16), preferred_element_type=F32)
            acc = part if acc is None else acc + part
        if nk > 1:
            o_ref[...] += acc
        else:
            if has_res:
                acc = res_ref[...] + acc
            if act == "relu2":
                acc = jnp.square(jnp.maximum(acc, 0.0))
            o_ref[...] = acc.astype(o_ref.dtype)

    @pl.when(i < n_prompt_blocks)
    def _():
        run(lhs_p, res_p, o_p)

    @pl.when(i == n_prompt_blocks)
    def _():
        run(lhs_s, res_s, o_s)


def _matmul(lhs_p, lhs_s, w, layer, *, n_cols, bm, bn, bk=None, res=None, act=None, out_dtype=F32,
            name="matmul"):
    mp, ms = lhs_p[0].shape[0], lhs_s[0].shape[0]
    k_total = w.shape[1]
    widths = [a.shape[1] for a in lhs_p]
    assert sum(widths) == k_total and mp % bm == 0 and n_cols % bn == 0
    if bk is None:
        bk = k_total
    nk = k_total // bk
    assert nk * bk == k_total and (nk == 1 or (len(lhs_p) == 1 and out_dtype == F32 and act is None))
    n_p, n_j = mp // bm, n_cols // bn

    def p_row(i):
        return jnp.minimum(i, n_p - 1)

    def p_col(i, j):
        return jnp.where(i < n_p, j, n_j - 1)

    def s_col(i, j):
        return jnp.where(i == n_p, j, 0)

    if nk == 1:
        bounds, k0 = [], 0
        for wd in widths:
            bounds.append((k0, k0 + wd))
            k0 += wd
        specs_p = [pl.BlockSpec((bm, wd), lambda i, j, k: (p_row(i), 0)) for wd in widths]
        specs_s = [pl.BlockSpec((ms, wd), lambda i, j, k: (0, 0)) for wd in widths]
    else:
        bounds = [(0, bk)]
        specs_p = [pl.BlockSpec((bm, bk), lambda i, j, k: (p_row(i), jnp.where(i < n_p, k, nk - 1)))]
        specs_s = [pl.BlockSpec((ms, bk), lambda i, j, k: (0, jnp.where(i == n_p, k, 0)))]
    w_spec = pl.BlockSpec((None, bk, bn), lambda i, j, k: (layer, k, j))
    io_p = pl.BlockSpec((bm, bn), lambda i, j, k: (p_row(i), p_col(i, j)))
    io_s = pl.BlockSpec((ms, bn), lambda i, j, k: (0, s_col(i, j)))
    in_specs = specs_p + specs_s + [w_spec]
    args = list(lhs_p) + list(lhs_s) + [w]
    if res is not None:
        in_specs += [io_p, io_s]
        args += list(res)
    body = functools.partial(_matmul_body, k_bounds=tuple(bounds), has_res=res is not None, act=act, nk=nk,
                             n_prompt_blocks=n_p)
    n_io = 2 if res is not None else 1
    limit = _vmem_limit(_nbytes((bm + ms, bk), BF16), _nbytes((bk, bn), F32),
                        n_io * _nbytes((bm + ms, bn), F32))
    return pl.pallas_call(
        body, grid=(n_p + 1, n_j, nk), in_specs=in_specs, out_specs=[io_p, io_s],
        out_shape=[jax.ShapeDtypeStruct((mp, n_cols), out_dtype), jax.ShapeDtypeStruct((ms, n_cols), out_dtype)],
        compiler_params=pltpu.CompilerParams(
            dimension_semantics=("arbitrary", "arbitrary", "arbitrary"), vmem_limit_bytes=limit),
        name=name)(*args)


def _ws_matmul_body(*refs, k_bounds, has_res, has_side, act, n_prompt_blocks, chunk_rows, transposed):
    n_lhs = len(k_bounds)
    lhs_p, lhs_s = refs[:n_lhs], refs[n_lhs:2 * n_lhs]
    wc_ref = refs[2 * n_lhs]
    res_p, res_s = (refs[2 * n_lhs + 1], refs[2 * n_lhs + 2]) if has_res else (None, None)
    w_scr = refs[-1]
    if has_side:
        side_in, side_out = refs[-5], refs[-2]
        o_p, o_s = refs[-4], refs[-3]
    else:
        o_p, o_s = refs[-3], refs[-2]
    jj, i = pl.program_id(0), pl.program_id(1)
    slot_load = jj % 2
    slot_use = 1 - slot_load

    def stage():
        r0 = pl.multiple_of(jnp.minimum(i, n_prompt_blocks - 1) * chunk_rows, chunk_rows)
        chunk = wc_ref[0] if len(wc_ref.shape) == 3 else wc_ref[...]
        w_scr[slot_load, pl.ds(r0, chunk_rows), :] = chunk.astype(BF16)

    def run(lhs_refs, res_ref, o_ref):
        stage()
        acc = None
        for a_ref, (k0, k1) in zip(lhs_refs, k_bounds):
            if transposed:
                part = lax.dot_general(a_ref[...], w_scr[slot_use, :, k0:k1], (((1,), (1,)), ((), ())),
                                       preferred_element_type=F32)
            else:
                part = jnp.dot(a_ref[...], w_scr[slot_use, k0:k1, :], preferred_element_type=F32)
            acc = part if acc is None else acc + part
        if has_res:
            acc = res_ref[...] + acc
        if act == "relu2":
            acc = jnp.square(jnp.maximum(acc, 0.0))
        o_ref[...] = acc.astype(o_ref.dtype)

    @pl.when(jj == 0)
    def _():
        stage()

    @pl.when((jj > 0) & (i < n_prompt_blocks))
    def _():
        if has_side:
            side_out[...] = side_in[...].astype(side_out.dtype)
        run(lhs_p, res_p, o_p)

    @pl.when((jj > 0) & (i == n_prompt_blocks))
    def _():
        run(lhs_s, res_s, o_s)


def _ws_matmul(lhs_p, lhs_s, w, layer, *, n_cols, transposed=False, w_row0=0, bm=512, bn=1024, res=None,
               act=None, out_dtype=F32, side_cast=None, name="ws_matmul"):
    mp, ms = lhs_p[0].shape[0], lhs_s[0].shape[0]
    k_total = w.shape[2] if transposed else w.shape[1]
    widths = [a.shape[1] for a in lhs_p]
    assert sum(widths) == k_total and mp % bm == 0 and n_cols % bn == 0
    n_p, n_j = mp // bm, n_cols // bn
    bounds, k0 = [], 0
    for wd in widths:
        bounds.append((k0, k0 + wd))
        k0 += wd

    def row_blk(jj, i):
        return jnp.where(jj == 0, 0, jnp.minimum(i, n_p - 1))

    def col_use(jj):
        return jnp.maximum(jj - 1, 0)

    def chunk(jj, i):
        return jnp.where(jj < n_j, jnp.minimum(i, n_p - 1), n_p - 1)

    def col_load(jj):
        return jnp.minimum(jj, n_j - 1)

    if transposed:
        chunk_rows = bn // n_p
        w_spec = pl.BlockSpec(
            (pl.Element(1), pl.Element(chunk_rows), pl.Element(k_total)),
            lambda jj, i: (layer, pl.multiple_of(
                w_row0 + (col_load(jj) * n_p + chunk(jj, i)) * chunk_rows, SUBLANES), 0))
        scratch = pltpu.VMEM((2, bn, k_total), BF16)
    else:
        chunk_rows = k_total // n_p
        w_spec = pl.BlockSpec((None, chunk_rows, bn), lambda jj, i: (layer, chunk(jj, i), col_load(jj)))
        scratch = pltpu.VMEM((2, k_total, bn), BF16)
    assert chunk_rows * n_p == (bn if transposed else k_total) and chunk_rows % 16 == 0
    specs_p = [pl.BlockSpec((bm, wd), lambda jj, i: (row_blk(jj, i), 0)) for wd in widths]
    specs_s = [pl.BlockSpec((ms, wd), lambda jj, i: (0, 0)) for wd in widths]
    io_p = pl.BlockSpec((bm, bn), lambda jj, i: (row_blk(jj, i), col_use(jj)))
    io_s = pl.BlockSpec((ms, bn), lambda jj, i: (0, col_use(jj)))
    in_specs = specs_p + specs_s + [w_spec]
    args = list(lhs_p) + list(lhs_s) + [w]
    if res is not None:
        in_specs += [io_p, io_s]
        args += list(res)
    out_specs = [io_p, io_s]
    out_shape = [jax.ShapeDtypeStruct((mp, n_cols), out_dtype), jax.ShapeDtypeStruct((ms, n_cols), out_dtype)]
    side_bytes = 0
    if side_cast is not None:
        side, side_layer = side_cast
        side_rows, side_cols = side.shape[1], side.shape[2]
        rb = side_rows // (n_j * n_p)
        assert rb * n_j * n_p == side_rows and rb % 16 == 0

        def side_blk(jj, i):
            return jnp.where(jj == 0, 0, (jj - 1) * n_p + jnp.minimum(i, n_p - 1))

        in_specs.append(pl.BlockSpec((None, rb, side_cols), lambda jj, i: (side_layer, side_blk(jj, i), 0)))
        args.append(side)
        out_specs.append(pl.BlockSpec((rb, side_cols), lambda jj, i: (side_blk(jj, i), 0)))
        out_shape.append(jax.ShapeDtypeStruct((side_rows, side_cols), BF16))
        side_bytes = _nbytes((rb, side_cols), F32) + _nbytes((rb, side_cols), BF16)
    body = functools.partial(_ws_matmul_body, k_bounds=tuple(bounds), has_res=res is not None,
                             has_side=side_cast is not None, act=act, n_prompt_blocks=n_p,
                             chunk_rows=chunk_rows, transposed=transposed)
    n_io = 2 if res is not None else 1
    pipelined = (_nbytes((bm + ms, k_total), BF16) + _nbytes((k_total // n_p, bn), F32)
                 + n_io * _nbytes((bm + ms, bn), F32) + side_bytes)
    limit = min(VMEM_CAP, 2 * pipelined + _nbytes((2, k_total, bn), BF16) + 2 * _nbytes((bm, bn), F32))
    return pl.pallas_call(
        body, grid=(n_j + 1, n_p + 1), in_specs=in_specs, out_specs=out_specs, out_shape=out_shape,
        scratch_shapes=[scratch],
        compiler_params=pltpu.CompilerParams(
            dimension_semantics=("arbitrary", "arbitrary"), vmem_limit_bytes=int(limit)),
        name=name)(*args)


def _linear_scan_rows(a, b, carry):
    t, lanes = b.shape
    g = t // SUBLANES
    b3 = b.reshape(g, SUBLANES, lanes)
    a3 = None if a is None else a.reshape(g, SUBLANES, lanes)
    sub = lax.broadcasted_iota(jnp.int32, b3.shape, 1)
    s = 1
    while s < SUBLANES:
        keep = sub >= s
        b_prev = jnp.where(keep, pltpu.roll(b3, s, 1), 0.0)
        if a3 is None:
            b3 = b3 + b_prev
        else:
            b3 = a3 * b_prev + b3
            a3 = a3 * jnp.where(keep, pltpu.roll(a3, s, 1), 1.0)
        s *= 2
    outs = []
    for j in range(g):
        h = b3[j] + carry if a3 is None else b3[j] + a3[j] * carry
        outs.append(h)
        carry = h[SUBLANES - 1:, :]
    return jnp.concatenate(outs, axis=0), carry


def _softplus(x):
    return jnp.maximum(x, 0.0) + jnp.log1p(jnp.exp(-jnp.abs(x)))


def _gla_body(q_ref, k_ref, v_ref, g_ref, alr_ref, wa_ref, ba_ref, gn_ref, s0_ref,
              o_ref, s_out_ref, st_ref, *, chunk, n_chunks, n_pairs):
    n = pl.program_id(2)

    @pl.when(n == 0)
    def _():
        for p in range(n_pairs):
            st_ref[p] = s0_ref[0, p].T

    c = chunk
    qw, vw = 2 * DK_A, 2 * DV_A
    lane = lax.broadcasted_iota(jnp.int32, (c, qw), 1)
    head0 = lane < DK_A
    row2 = lax.broadcasted_iota(jnp.int32, (2 * c, 2 * c), 0)
    col2 = lax.broadcasted_iota(jnp.int32, (2 * c, 2 * c), 1)
    causal = ((row2 >= c) == (col2 >= c)) & (row2 >= col2)
    nt = (((1,), (1,)), ((), ()))
    tn = (((0,), (0,)), ((), ()))

    def stack_heads(z):
        return jnp.concatenate([jnp.where(head0, z, 0.0), jnp.where(head0, 0.0, z)], axis=0).astype(BF16)

    for ci in range(n_chunks):
        rows = pl.ds(ci * c, c)
        z = jnp.dot(alr_ref[rows, :].astype(BF16), wa_ref[...], preferred_element_type=F32) + ba_ref[...]
        log_alpha = -(jnp.maximum(-z, 0.0) + jnp.log(1.0 + jnp.exp(-jnp.abs(z)))) / GLA_TAU
        b, b_end = _linear_scan_rows(None, log_alpha, jnp.zeros((1, log_alpha.shape[1]), F32))
        q = q_ref[rows, :] * (DK_A ** -0.5)
        k = k_ref[rows, :]
        q_dec = q * jnp.exp(b)
        k_inv = k * jnp.exp(-b)
        k_end = k * jnp.exp(b_end - b)
        decay = jnp.exp(b_end)
        for p in range(n_pairs):
            ql = slice(p * qw, (p + 1) * qw)
            q_t, k_t, k_e = stack_heads(q_dec[:, ql]), stack_heads(k_inv[:, ql]), stack_heads(k_end[:, ql])
            v0 = v_ref[rows, pl.ds(p * vw, DV_A)]
            v1 = v_ref[rows, pl.ds(p * vw + DV_A, DV_A)]
            v_s = jnp.concatenate([v0, v1], axis=0).astype(BF16)
            att = lax.dot_general(q_t, k_t, nt, preferred_element_type=F32)
            att = jnp.where(causal, att, 0.0).astype(BF16)
            st = st_ref[p]
            o = (jnp.dot(att, v_s, preferred_element_type=F32)
                 + lax.dot_general(q_t, st.astype(BF16), nt, preferred_element_type=F32))
            st_ref[p] = decay[:, ql] * st + lax.dot_general(v_s, k_e, tn, preferred_element_type=F32)
            ms = jnp.mean(jnp.square(o), axis=-1, keepdims=True)
            o = (o * lax.rsqrt(ms + EPS) * gn_ref[...])
            g0 = g_ref[rows, pl.ds(p * vw, DV_A)]
            g1 = g_ref[rows, pl.ds(p * vw + DV_A, DV_A)]
            o_ref[rows, pl.ds(p * vw, DV_A)] = (o[:c] * (g0 * jax.nn.sigmoid(g0))).astype(o_ref.dtype)
            o_ref[rows, pl.ds(p * vw + DV_A, DV_A)] = (o[c:] * (g1 * jax.nn.sigmoid(g1))).astype(o_ref.dtype)

    @pl.when(n == pl.num_programs(2) - 1)
    def _():
        for p in range(n_pairs):
            s_out_ref[0, p] = st_ref[p].T


def _gla(hproj, alr, wa, ba, gn, s0, *, n_seq, seq_len, col_q, col_k, col_v, col_g):
    t_total = hproj.shape[0]
    h_a = s0.shape[1]
    pairs = h_a // 2
    pps = GLA_PAIRS_PER_STEP
    assert pairs % pps == 0
    chunk = min(CHUNK, seq_len)
    n_chunks = min(8, seq_len // chunk)
    tb = chunk * n_chunks
    nb = seq_len // tb
    qw, vw = 2 * DK_A * pps, 2 * DV_A * pps
    assert col_q % qw == 0 and col_k % qw == 0 and col_v % vw == 0 and col_g % vw == 0
    cq, ck, cv, cg = col_q // qw, col_k // qw, col_v // vw, col_g // vw
    row = lambda s, n: s * nb + n
    in_specs = [
        pl.BlockSpec((tb, qw), lambda s, p, n: (row(s, n), cq + p)),
        pl.BlockSpec((tb, qw), lambda s, p, n: (row(s, n), ck + p)),
        pl.BlockSpec((tb, vw), lambda s, p, n: (row(s, n), cv + p)),
        pl.BlockSpec((tb, vw), lambda s, p, n: (row(s, n), cg + p)),
        pl.BlockSpec((tb, LANES), lambda s, p, n: (row(s, n), 0)),
        pl.BlockSpec((LANES, qw), lambda s, p, n: (0, p)),
        pl.BlockSpec((1, qw), lambda s, p, n: (0, p)),
        pl.BlockSpec((1, DV_A), lambda s, p, n: (0, 0)),
        pl.BlockSpec((1, pps, 2 * DK_A, DV_A), lambda s, p, n: (s, p, 0, 0)),
    ]
    out_specs = [
        pl.BlockSpec((tb, vw), lambda s, p, n: (row(s, n), p)),
        pl.BlockSpec((1, pps, 2 * DK_A, DV_A), lambda s, p, n: (s, p, 0, 0)),
    ]
    o_a, s_new = pl.pallas_call(
        functools.partial(_gla_body, chunk=chunk, n_chunks=n_chunks, n_pairs=pps),
        grid=(n_seq, pairs // pps, nb), in_specs=in_specs, out_specs=out_specs,
        out_shape=[jax.ShapeDtypeStruct((t_total, h_a * DV_A), BF16),
                   jax.ShapeDtypeStruct((n_seq, pairs, 2 * DK_A, DV_A), F32)],
        scratch_shapes=[pltpu.VMEM((pps, DV_A, 2 * DK_A), F32)],
        compiler_params=pltpu.CompilerParams(
            dimension_semantics=("parallel", "parallel", "arbitrary"),
            vmem_limit_bytes=_vmem_limit(_nbytes((tb, 2 * qw + 2 * vw + LANES), F32), _nbytes((tb, vw), BF16))),
        name="gla")(hproj, hproj, hproj, hproj, alr, wa, ba, gn, s0.reshape(n_seq, pairs, 2 * DK_A, DV_A))
    return o_a, s_new.reshape(n_seq, h_a, DK_A, DV_A)


def _gmlp_body(u_ref, v_ref, w_ref, b_ref, o_ref, *vg_refs, c, dh):
    tril = (lax.broadcasted_iota(jnp.int32, (c, c), 0) >= lax.broadcasted_iota(jnp.int32, (c, c), 1))
    w_tril = [jnp.where(tril, w_ref[h, :c, :c], 0.0).astype(BF16) for h in range(w_ref.shape[0])]
    for ci in range(u_ref.shape[0] // c):
        rows = pl.ds(ci * c, c)
        vg = jax.nn.gelu(v_ref[rows, :])
        if vg_refs:
            vg_refs[0][rows, :] = vg
        ug = jax.nn.gelu(u_ref[rows, :])
        for h, w in enumerate(w_tril):
            cols = pl.ds(h * dh, dh)
            mixed = (jnp.dot(w, vg[:, h * dh:(h + 1) * dh].astype(BF16), preferred_element_type=F32)
                     + b_ref[:c, cols])
            o_ref[rows, cols] = (ug[:, h * dh:(h + 1) * dh] * mixed).astype(o_ref.dtype)


def _gmlp(hproj, w_sp, b_full, *, seq_len, col_u, col_v, want_v):
    t_total = hproj.shape[0]
    n_h, gc, _ = w_sp.shape
    w_b = b_full.shape[1]
    dh = w_b // n_h
    c = min(gc, seq_len)
    tb = c * min(4, seq_len // c)
    blk = lambda col: pl.BlockSpec((tb, w_b), lambda r: (r, col // w_b))
    out_specs = [pl.BlockSpec((tb, w_b), lambda r: (r, 0))]
    out_shape = [jax.ShapeDtypeStruct((t_total, w_b), BF16)]
    if want_v:
        out_specs.append(pl.BlockSpec((tb, w_b), lambda r: (r, 0)))
        out_shape.append(jax.ShapeDtypeStruct((t_total, w_b), F32))
    outs = pl.pallas_call(
        functools.partial(_gmlp_body, c=c, dh=dh), grid=(t_total // tb,),
        in_specs=[blk(col_u), blk(col_v),
                  pl.BlockSpec((n_h, gc, gc), lambda r: (0, 0, 0)),
                  pl.BlockSpec((gc, w_b), lambda r: (0, 0))],
        out_specs=out_specs, out_shape=out_shape,
        compiler_params=pltpu.CompilerParams(
            dimension_semantics=("parallel",),
            vmem_limit_bytes=_vmem_limit(4 * _nbytes((tb, w_b), F32), _nbytes((n_h, gc, gc), F32))),
        name="gmlp")(hproj, hproj, w_sp, b_full)
    return (outs[0], outs[1]) if want_v else (outs[0], None)


def _lru_body(x_ref, gate_ref, cw_ref, cb_ref, wr_ref, br_ref, wi_ref, bi_ref, lam_ref, h0_ref, tail0_ref,
              o_ref, h_ref, tail_ref, *, dh):
    n = pl.program_id(1)

    @pl.when(n == 0)
    def _():
        h_ref[...] = h0_ref[...]
        tail_ref[...] = tail0_ref[...]

    tb = x_ref.shape[0]
    row8 = lax.broadcasted_iota(jnp.int32, (SUBLANES, dh), 0)
    for hb in range(x_ref.shape[1] // dh):
        cols = pl.ds(hb * dh, dh)
        x = x_ref[:, cols]
        tail = tail_ref[0, :, cols]

        def delayed(j):
            rolled = pltpu.roll(x, j, 0)
            head = jnp.where(row8 >= j, rolled[:SUBLANES], pltpu.roll(tail, j, 0))
            return jnp.concatenate([head, rolled[SUBLANES:]], axis=0)

        y = cb_ref[:, cols] + delayed(CONV_W - 1) * cw_ref[0:1, cols]
        for j in range(1, CONV_W - 1):
            y = y + delayed(CONV_W - 1 - j) * cw_ref[j:j + 1, cols]
        y = y + x * cw_ref[CONV_W - 1:CONV_W, cols]
        tail_ref[0, :, cols] = x[tb - SUBLANES:]

        yb = y.astype(BF16)
        r = jax.nn.sigmoid(jnp.dot(yb, wr_ref[hb].astype(BF16), preferred_element_type=F32) + br_ref[:, cols])
        ig = jax.nn.sigmoid(jnp.dot(yb, wi_ref[hb].astype(BF16), preferred_element_type=F32) + bi_ref[:, cols])
        log_a = -LRU_C * r * _softplus(-lam_ref[:, cols])
        a = jnp.exp(log_a)
        bt = jnp.sqrt(-jnp.tanh(log_a) * (a * a + 1.0)) * (ig * y)
        hseq, h_last = _linear_scan_rows(a, bt, h_ref[0, :, cols])
        h_ref[0, :, cols] = h_last
        o_ref[:, cols] = (hseq * jax.nn.gelu(gate_ref[:, cols])).astype(o_ref.dtype)


def _lru(hproj, cw, cb, wr, br, wi, bi, lam, h0, tail0, *, n_seq, seq_len, col_x, col_gate):
    t_total = hproj.shape[0]
    w_c = cw.shape[1]
    n_h, dh, _ = wr.shape
    tb = min(seq_len, 256)
    nb = seq_len // tb
    vec = lambda rows: pl.BlockSpec((rows, w_c), lambda s, n: (0, 0))
    wspec = pl.BlockSpec((n_h, dh, dh), lambda s, n: (0, 0, 0))
    hspec = pl.BlockSpec((1, 1, w_c), lambda s, n: (s, 0, 0))
    tspec = pl.BlockSpec((1, SUBLANES, w_c), lambda s, n: (s, 0, 0))
    o_c, h_new, tail = pl.pallas_call(
        functools.partial(_lru_body, dh=dh), grid=(n_seq, nb),
        in_specs=[pl.BlockSpec((tb, w_c), lambda s, n: (s * nb + n, col_x // w_c)),
                  pl.BlockSpec((tb, w_c), lambda s, n: (s * nb + n, col_gate // w_c)),
                  vec(CONV_W), vec(1), wspec, vec(1), wspec, vec(1), vec(1), hspec, tspec],
        out_specs=[pl.BlockSpec((tb, w_c), lambda s, n: (s * nb + n, 0)), hspec, tspec],
        out_shape=[jax.ShapeDtypeStruct((t_total, w_c), BF16),
                   jax.ShapeDtypeStruct((n_seq, 1, w_c), F32),
                   jax.ShapeDtypeStruct((n_seq, SUBLANES, w_c), F32)],
        compiler_params=pltpu.CompilerParams(
            dimension_semantics=("parallel", "arbitrary"),
            vmem_limit_bytes=_vmem_limit(4 * _nbytes((tb, w_c), F32))),
        name="conv_rglru")(hproj, hproj, cw, cb.reshape(1, w_c), wr, br.reshape(1, w_c), wi,
                           bi.reshape(1, w_c), lam.reshape(1, w_c), h0.reshape(n_seq, 1, w_c), tail0)
    return o_c, h_new.reshape(n_seq, w_c), tail


def _mixers(hp_a, hp_b, alr, s_gla, s_conv, s_lru, w, *, n_seq, seq_len, want_v):
    w_a = s_gla.shape[1] * DV_A
    qk_a = s_gla.shape[1] * DK_A
    w_b = w_c = s_lru.shape[1]
    o_a, s_gla_new = _gla(hp_a, alr, w["w_alpha2"], w["b_alpha"], w["g_onorm"], s_gla, n_seq=n_seq,
                          seq_len=seq_len, col_q=0, col_k=qk_a, col_v=2 * qk_a, col_g=2 * qk_a + w_a)
    o_b, vg = _gmlp(hp_b, w["w_spatial"], w["b_spatial"], seq_len=seq_len, col_u=0, col_v=w_b, want_v=want_v)
    tail0 = jnp.pad(s_conv, ((0, 0), (SUBLANES - (CONV_W - 1), 0), (0, 0)))
    o_c, s_lru_new, tail = _lru(hp_b, w["conv_w"], w["conv_b"], w["w_rgate"], w["b_rgate"], w["w_igate"],
                                w["b_igate"], w["lru_lambda"], s_lru, tail0, n_seq=n_seq, seq_len=seq_len,
                                col_x=2 * w_b, col_gate=2 * w_b + w_c)
    return [o_a, o_b, o_c], (s_gla_new, tail[:, SUBLANES - (CONV_W - 1):], s_lru_new, vg)


def _small_layer_weights(l, w_in_t, w_alpha2, b_alpha, g_onorm, w_spatial, b_spatial, conv_w, conv_b,
                         w_rgate, b_rgate, w_igate, b_igate, lru_lambda):
    d = w_in_t.shape[2]
    w_a, w_b = d // 2, d // 4
    qk_a = (w_a // DV_A) * DK_A
    c_alr = 2 * qk_a + 2 * w_a
    w_alr = jnp.pad(w_in_t[l, c_alr:c_alr + GLA_RANK, :], ((0, LANES - GLA_RANK), (0, 0)))
    w_alpha2_p = jnp.pad(w_alpha2[l], ((0, LANES - GLA_RANK), (0, 0))).astype(BF16)
    b_full = jnp.repeat(b_spatial[l].T, w_b // H_B, axis=1)
    return dict(
        w_alr=w_alr, w_alpha2=w_alpha2_p, b_alpha=b_alpha[l].reshape(1, qk_a),
        g_onorm=g_onorm[l].reshape(1, DV_A), w_spatial=w_spatial[l], b_spatial=b_full,
        conv_w=conv_w[l], conv_b=conv_b[l], w_rgate=w_rgate[l], b_rgate=b_rgate[l],
        w_igate=w_igate[l], b_igate=b_igate[l], lru_lambda=lru_lambda[l])


def kernel(x_prompt, x_sample, state_gla, state_conv, state_lru, ln1, w_in, w_alpha2, b_alpha, g_onorm, w_spatial, b_spatial, conv_w, conv_b, w_rgate, b_rgate, w_igate, b_igate, lru_lambda, w_out, ln2, w_up, w_down, ln_final):
    bp, tp, d = x_prompt.shape
    bs, ts, _ = x_sample.shape
    depth = w_in.shape[0]
    w_a, w_b, w_c = d // 2, d // 4, d // 4
    h_a = w_a // DV_A
    qk_a = h_a * DK_A
    n_a = 2 * qk_a + 2 * w_a
    n_b = 2 * w_b + 2 * w_c
    zero_gla = jnp.zeros((bp, h_a, DK_A, DV_A), F32)
    zero_conv = jnp.zeros((bp, CONV_W - 1, w_c), F32)
    zero_lru = jnp.zeros((bp, w_c), F32)
    xp = x_prompt.reshape(bp * tp, d)
    xs = x_sample.reshape(bs * ts, d)
    w_in_t = jnp.swapaxes(w_in, 1, 2)
    mm = _ws_matmul
    st_p, st_s = [], []
    for l in range(depth):
        w = _small_layer_weights(l, w_in_t, w_alpha2, b_alpha, g_onorm, w_spatial, b_spatial, conv_w, conv_b,
                                 w_rgate, b_rgate, w_igate, b_igate, lru_lambda)
        xn_p, alr_p = _rmsnorm(xp, ln1[l], BF16, w_proj=w["w_alr"])
        xn_s, alr_s = _rmsnorm(xs, ln1[l], BF16, w_proj=w["w_alr"])
        hpa_p, hpa_s = mm([xn_p], [xn_s], w_in_t, l, n_cols=n_a, transposed=True, bm=1024, name="in_proj_a")
        hpb_p, hpb_s = mm([xn_p], [xn_s], w_in_t, l, n_cols=n_b, transposed=True, w_row0=n_a + GLA_RANK,
                          bm=1024, name="in_proj_b")
        mix_p, new_p = _mixers(hpa_p, hpb_p, alr_p, zero_gla, zero_conv, zero_lru, w, n_seq=bp, seq_len=tp,
                               want_v=False)
        mix_s, new_s = _mixers(hpa_s, hpb_s, alr_s, state_gla[l], state_conv[l], state_lru[l], w, n_seq=bs,
                               seq_len=ts, want_v=True)
        st_p.append(new_p)
        st_s.append(new_s)
        xp, xs = mm(mix_p, mix_s, w_out, l, n_cols=d, res=(xp, xs), name="out_proj")
        xn_p = _rmsnorm(xp, ln2[l], BF16)
        xn_s = _rmsnorm(xs, ln2[l], BF16)
        hid_p, hid_s, w_down_bf = mm([xn_p], [xn_s], w_up, l, n_cols=w_up.shape[2], act="relu2",
                                     out_dtype=BF16, side_cast=(w_down, l), name="mlp_up")
        xp, xs = _matmul([hid_p], [hid_s], w_down_bf[None], 0, n_cols=d, bm=1024, bn=1024, bk=2048,
                         res=(xp, xs), name="mlp_down")
    y_prompt = _rmsnorm(xp, ln_final, F32).reshape(bp, tp, d)
    y_sample = _rmsnorm(xs, ln_final, F32).reshape(bs, ts, d)
    stack = lambda sts, i: jnp.stack([s[i] for s in sts])
    return (y_prompt, y_sample, stack(st_p, 0), stack(st_p, 1), stack(st_p, 2),
            stack(st_s, 0), stack(st_s, 1), stack(st_s, 2),
            jnp.stack([s[3].reshape(bs, ts, w_b) for s in st_s]))
```

```python
import functools

import jax
import jax.numpy as jnp
from jax import lax
from jax.experimental import pallas as pl
from jax.experimental.pallas import tpu as pltpu

F32 = jnp.float32
BF16 = jnp.bfloat16

DV_A = 128
DK_A = 64
GLA_RANK = 16
GLA_TAU = 16.0
CHUNK = 64
GMLP_CHUNK = 128
H_B = 8
H_C = 8
CONV_W = 4
LRU_C = 8.0
EPS = 1e-6

LANES = 128
SUBLANES = 8
V7X_VMEM_BYTES = 64 * 1024 * 1024
VMEM_CAP = V7X_VMEM_BYTES - 6 * 1024 * 1024

GLA_PAIRS_PER_STEP = 8


def _vmem_limit(*block_bytes):
    need = 4 * sum(block_bytes)
    return int(min(max(need, 16 * 1024 * 1024), VMEM_CAP))


def _nbytes(shape, dtype):
    n = 1
    for s in shape:
        n *= s
    return n * jnp.dtype(dtype).itemsize


def _rmsnorm_body(x_ref, g_ref, o_ref):
    x = x_ref[...]
    ms = jnp.mean(jnp.square(x), axis=-1, keepdims=True)
    o_ref[...] = (x * lax.rsqrt(ms + EPS) * g_ref[...]).astype(o_ref.dtype)


def _rmsnorm_proj_body(x_ref, g_ref, w_ref, o_ref, p_ref):
    x = x_ref[...]
    ms = jnp.mean(jnp.square(x), axis=-1, keepdims=True)
    y = (x * lax.rsqrt(ms + EPS) * g_ref[...]).astype(BF16)
    o_ref[...] = y
    p_ref[...] = lax.dot_general(y, w_ref[...].astype(BF16), (((1,), (1,)), ((), ())),
                                 preferred_element_type=F32)


def _rmsnorm(x, g, out_dtype, w_proj=None):
    t, d = x.shape
    tb = min(t, 512)
    grid = (t // tb,)
    x_spec = pl.BlockSpec((tb, d), lambda i: (i, 0))
    g_spec = pl.BlockSpec((1, d), lambda i: (0, 0))
    params = pltpu.CompilerParams(
        dimension_semantics=("parallel",),
        vmem_limit_bytes=_vmem_limit(_nbytes((tb, d), F32), _nbytes((tb, d), out_dtype)))
    if w_proj is None:
        return pl.pallas_call(
            _rmsnorm_body, grid=grid, in_specs=[x_spec, g_spec], out_specs=x_spec,
            out_shape=jax.ShapeDtypeStruct((t, d), out_dtype), compiler_params=params,
            name="rmsnorm")(x, g.reshape(1, d))
    n = w_proj.shape[0]
    return pl.pallas_call(
        _rmsnorm_proj_body, grid=grid,
        in_specs=[x_spec, g_spec, pl.BlockSpec((n, d), lambda i: (0, 0))],
        out_specs=[x_spec, pl.BlockSpec((tb, n), lambda i: (i, 0))],
        out_shape=[jax.ShapeDtypeStruct((t, d), BF16), jax.ShapeDtypeStruct((t, n), F32)],
        compiler_params=params, name="rmsnorm_proj")(x, g.reshape(1, d), w_proj)


def _matmul_body(*refs, k_bounds, has_res, act, nk, n_prompt_blocks):
    n_lhs = len(k_bounds)
    lhs_p, lhs_s = refs[:n_lhs], refs[n_lhs:2 * n_lhs]
    w_ref = refs[2 * n_lhs]
    res_p, res_s = (refs[2 * n_lhs + 1], refs[2 * n_lhs + 2]) if has_res else (None, None)
    o_p, o_s = refs[-2], refs[-1]
    i = pl.program_id(0)

    def run(lhs_refs, res_ref, o_ref):
        if nk > 1:
            @pl.when(pl.program_id(2) == 0)
            def _():
                o_ref[...] = res_ref[...] if has_res else jnp.zeros(o_ref.shape, o_ref.dtype)

        acc = None
        for a_ref, (k0, k1) in zip(lhs_refs, k_bounds):
            part = jnp.dot(a_ref[...], w_ref[k0:k1, :].astype(BF16), preferred_element_type=F32)
            acc = part if acc is None else acc + part
        if nk > 1:
            o_ref[...] += acc
        else:
            if has_res:
                acc = res_ref[...] + acc
            if act == "relu2":
                acc = jnp.square(jnp.maximum(acc, 0.0))
            o_ref[...] = acc.astype(o_ref.dtype)

    @pl.when(i < n_prompt_blocks)
    def _():
        run(lhs_p, res_p, o_p)

    @pl.when(i == n_prompt_blocks)
    def _():
        run(lhs_s, res_s, o_s)


def _matmul(lhs_p, lhs_s, w, layer, *, n_cols, bm, bn, bk=None, res=None, act=None, out_dtype=F32,
            name="matmul"):
    mp, ms = lhs_p[0].shape[0], lhs_s[0].shape[0]
    k_total = w.shape[1]
    widths = [a.shape[1] for a in lhs_p]
    assert sum(widths) == k_total and mp % bm == 0 and n_cols % bn == 0
    if bk is None:
        bk = k_total
    nk = k_total // bk
    assert nk * bk == k_total and (nk == 1 or (len(lhs_p) == 1 and out_dtype == F32 and act is None))
    n_p, n_j = mp // bm, n_cols // bn

    def p_row(i):
        return jnp.minimum(i, n_p - 1)

    def p_col(i, j):
        return jnp.where(i < n_p, j, n_j - 1)

    def s_col(i, j):
        return jnp.where(i == n_p, j, 0)

    if nk == 1:
        bounds, k0 = [], 0
        for wd in widths:
            bounds.append((k0, k0 + wd))
            k0 += wd
        specs_p = [pl.BlockSpec((bm, wd), lambda i, j, k: (p_row(i), 0)) for wd in widths]
        specs_s = [pl.BlockSpec((ms, wd), lambda i, j, k: (0, 0)) for wd in widths]
    else:
        bounds = [(0, bk)]
        specs_p = [pl.BlockSpec((bm, bk), lambda i, j, k: (p_row(i), jnp.where(i < n_p, k, nk - 1)))]
        specs_s = [pl.BlockSpec((ms, bk), lambda i, j, k: (0, jnp.where(i == n_p, k, 0)))]
    w_spec = pl.BlockSpec((None, bk, bn), lambda i, j, k: (layer, k, j))
    io_p = pl.BlockSpec((bm, bn), lambda i, j, k: (p_row(i), p_col(i, j)))
    io_s = pl.BlockSpec((ms, bn), lambda i, j, k: (0, s_col(i, j)))
    in_specs = specs_p + specs_s + [w_spec]
    args = list(lhs_p) + list(lhs_s) + [w]
    if res is not None:
        in_specs += [io_p, io_s]
        args += list(res)
    body = functools.partial(_matmul_body, k_bounds=tuple(bounds), has_res=res is not None, act=act, nk=nk,
                             n_prompt_blocks=n_p)
    n_io = 2 if res is not None else 1
    limit = _vmem_limit(_nbytes((bm + ms, bk), BF16), _nbytes((bk, bn), F32),
                        n_io * _nbytes((bm + ms, bn), F32))
    return pl.pallas_call(
        body, grid=(n_p + 1, n_j, nk), in_specs=in_specs, out_specs=[io_p, io_s],
        out_shape=[jax.ShapeDtypeStruct((mp, n_cols), out_dtype), jax.ShapeDtypeStruct((ms, n_cols), out_dtype)],
        compiler_params=pltpu.CompilerParams(
            dimension_semantics=("arbitrary", "arbitrary", "arbitrary"), vmem_limit_bytes=limit),
        name=name)(*args)


def _ws_matmul_body(*refs, k_bounds, has_res, has_side, act, n_prompt_blocks, chunk_rows, transposed):
    n_lhs = len(k_bounds)
    lhs_p, lhs_s = refs[:n_lhs], refs[n_lhs:2 * n_lhs]
    wc_ref = refs[2 * n_lhs]
    res_p, res_s = (refs[2 * n_lhs + 1], refs[2 * n_lhs + 2]) if has_res else (None, None)
    w_scr = refs[-1]
    if has_side:
        side_in, side_out = refs[-5], refs[-2]
        o_p, o_s = refs[-4], refs[-3]
    else:
        o_p, o_s = refs[-3], refs[-2]
    jj, i = pl.program_id(0), pl.program_id(1)
    slot_load = jj % 2
    slot_use = 1 - slot_load

    def stage():
        r0 = pl.multiple_of(jnp.minimum(i, n_prompt_blocks - 1) * chunk_rows, chunk_rows)
        chunk = wc_ref[0] if len(wc_ref.shape) == 3 else wc_ref[...]
        w_scr[slot_load, pl.ds(r0, chunk_rows), :] = chunk.astype(BF16)

    def run(lhs_refs, res_ref, o_ref):
        stage()
        acc = None
        for a_ref, (k0, k1) in zip(lhs_refs, k_bounds):
            if transposed:
                part = lax.dot_general(a_ref[...], w_scr[slot_use, :, k0:k1], (((1,), (1,)), ((), ())),
                                       preferred_element_type=F32)
            else:
                part = jnp.dot(a_ref[...], w_scr[slot_use, k0:k1, :], preferred_element_type=F32)
            acc = part if acc is None else acc + part
        if has_res:
            acc = res_ref[...] + acc
        if act == "relu2":
            acc = jnp.square(jnp.maximum(acc, 0.0))
        o_ref[...] = acc.astype(o_ref.dtype)

    @pl.when(jj == 0)
    def _():
        stage()

    @pl.when((jj > 0) & (i < n_prompt_blocks))
    def _():
        if has_side:
            side_out[...] = side_in[...].astype(side_out.dtype)
        run(lhs_p, res_p, o_p)

    @pl.when((jj > 0) & (i == n_prompt_blocks))
    def _():
        run(lhs_s, res_s, o_s)


def _ws_matmul(lhs_p, lhs_s, w, layer, *, n_cols, transposed=False, w_row_gap=(0, 0), bm=512, bn=1024,
               res=None, act=None, out_dtype=F32, side_cast=None, name="ws_matmul"):
    mp, ms = lhs_p[0].shape[0], lhs_s[0].shape[0]
    k_total = w.shape[2] if transposed else w.shape[1]
    widths = [a.shape[1] for a in lhs_p]
    assert sum(widths) == k_total and mp % bm == 0 and n_cols % bn == 0
    n_p, n_j = mp // bm, n_cols // bn
    bounds, k0 = [], 0
    for wd in widths:
        bounds.append((k0, k0 + wd))
        k0 += wd

    def row_blk(jj, i):
        return jnp.where(jj == 0, 0, jnp.minimum(i, n_p - 1))

    def col_use(jj):
        return jnp.maximum(jj - 1, 0)

    def chunk(jj, i):
        return jnp.where(jj < n_j, jnp.minimum(i, n_p - 1), n_p - 1)

    def col_load(jj):
        return jnp.minimum(jj, n_j - 1)

    if transposed:
        chunk_rows = bn // n_p
        gap_row, gap_rows = w_row_gap
        assert gap_row % bn == 0 and gap_rows % SUBLANES == 0

        def w_row(jj, i):
            row = (col_load(jj) * n_p + chunk(jj, i)) * chunk_rows
            return pl.multiple_of(row + jnp.where(row >= gap_row, gap_rows, 0), SUBLANES)

        w_spec = pl.BlockSpec((pl.Element(1), pl.Element(chunk_rows), pl.Element(k_total)),
                              lambda jj, i: (layer, w_row(jj, i), 0))
        scratch = pltpu.VMEM((2, bn, k_total), BF16)
    else:
        chunk_rows = k_total // n_p
        w_spec = pl.BlockSpec((None, chunk_rows, bn), lambda jj, i: (layer, chunk(jj, i), col_load(jj)))
        scratch = pltpu.VMEM((2, k_total, bn), BF16)
    assert chunk_rows * n_p == (bn if transposed else k_total) and chunk_rows % 16 == 0
    specs_p = [pl.BlockSpec((bm, wd), lambda jj, i: (row_blk(jj, i), 0)) for wd in widths]
    specs_s = [pl.BlockSpec((ms, wd), lambda jj, i: (0, 0)) for wd in widths]
    io_p = pl.BlockSpec((bm, bn), lambda jj, i: (row_blk(jj, i), col_use(jj)))
    io_s = pl.BlockSpec((ms, bn), lambda jj, i: (0, col_use(jj)))
    in_specs = specs_p + specs_s + [w_spec]
    args = list(lhs_p) + list(lhs_s) + [w]
    if res is not None:
        in_specs += [io_p, io_s]
        args += list(res)
    out_specs = [io_p, io_s]
    out_shape = [jax.ShapeDtypeStruct((mp, n_cols), out_dtype), jax.ShapeDtypeStruct((ms, n_cols), out_dtype)]
    side_bytes = 0
    if side_cast is not None:
        side, side_layer = side_cast
        side_rows, side_cols = side.shape[1], side.shape[2]
        rb = side_rows // (n_j * n_p)
        assert rb * n_j * n_p == side_rows and rb % 16 == 0

        def side_blk(jj, i):
            return jnp.where(jj == 0, 0, (jj - 1) * n_p + jnp.minimum(i, n_p - 1))

        in_specs.append(pl.BlockSpec((None, rb, side_cols), lambda jj, i: (side_layer, side_blk(jj, i), 0)))
        args.append(side)
        out_specs.append(pl.BlockSpec((rb, side_cols), lambda jj, i: (side_blk(jj, i), 0)))
        out_shape.append(jax.ShapeDtypeStruct((side_rows, side_cols), BF16))
        side_bytes = _nbytes((rb, side_cols), F32) + _nbytes((rb, side_cols), BF16)
    body = functools.partial(_ws_matmul_body, k_bounds=tuple(bounds), has_res=res is not None,
                             has_side=side_cast is not None, act=act, n_prompt_blocks=n_p,
                             chunk_rows=chunk_rows, transposed=transposed)
    n_io = 2 if res is not None else 1
    pipelined = (_nbytes((bm + ms, k_total), BF16) + _nbytes((k_total // n_p, bn), F32)
                 + n_io * _nbytes((bm + ms, bn), F32) + side_bytes)
    limit = min(VMEM_CAP, 2 * pipelined + _nbytes((2, k_total, bn), BF16) + 2 * _nbytes((bm, bn), F32))
    return pl.pallas_call(
        body, grid=(n_j + 1, n_p + 1), in_specs=in_specs, out_specs=out_specs, out_shape=out_shape,
        scratch_shapes=[scratch],
        compiler_params=pltpu.CompilerParams(
            dimension_semantics=("arbitrary", "arbitrary"), vmem_limit_bytes=int(limit)),
        name=name)(*args)


def _linear_scan_rows(a, b, carry):
    t, lanes = b.shape
    g = t // SUBLANES
    b3 = b.reshape(g, SUBLANES, lanes)
    a3 = None if a is None else a.reshape(g, SUBLANES, lanes)
    sub = lax.broadcasted_iota(jnp.int32, b3.shape, 1)
    s = 1
    while s < SUBLANES:
        keep = sub >= s
        b_prev = jnp.where(keep, pltpu.roll(b3, s, 1), 0.0)
        if a3 is None:
            b3 = b3 + b_prev
        else:
            b3 = a3 * b_prev + b3
            a3 = a3 * jnp.where(keep, pltpu.roll(a3, s, 1), 1.0)
        s *= 2
    outs = []
    for j in range(g):
        h = b3[j] + carry if a3 is None else b3[j] + a3[j] * carry
        outs.append(h)
        carry = h[SUBLANES - 1:, :]
    return jnp.concatenate(outs, axis=0), carry


def _softplus(x):
    return jnp.maximum(x, 0.0) + jnp.log1p(jnp.exp(-jnp.abs(x)))


def _gla_body(q_ref, k_ref, v_ref, g_ref, alr_ref, wa_ref, ba_ref, gn_ref, s0_ref,
              o_ref, s_out_ref, st_ref, *, chunk, n_chunks, n_pairs):
    n = pl.program_id(2)

    @pl.when(n == 0)
    def _():
        for p in range(n_pairs):
            st_ref[p] = s0_ref[0, p].T

    c = chunk
    qw, vw = 2 * DK_A, 2 * DV_A
    lane = lax.broadcasted_iota(jnp.int32, (c, qw), 1)
    head0 = lane < DK_A
    row2 = lax.broadcasted_iota(jnp.int32, (2 * c, 2 * c), 0)
    col2 = lax.broadcasted_iota(jnp.int32, (2 * c, 2 * c), 1)
    causal = ((row2 >= c) == (col2 >= c)) & (row2 >= col2)
    nt = (((1,), (1,)), ((), ()))
    tn = (((0,), (0,)), ((), ()))

    def stack_heads(z):
        return jnp.concatenate([jnp.where(head0, z, 0.0), jnp.where(head0, 0.0, z)], axis=0).astype(BF16)

    for ci in range(n_chunks):
        rows = pl.ds(ci * c, c)
        z = jnp.dot(alr_ref[rows, :].astype(BF16), wa_ref[...], preferred_element_type=F32) + ba_ref[...]
        log_alpha = -(jnp.maximum(-z, 0.0) + jnp.log(1.0 + jnp.exp(-jnp.abs(z)))) / GLA_TAU
        b, b_end = _linear_scan_rows(None, log_alpha, jnp.zeros((1, log_alpha.shape[1]), F32))
        q = q_ref[rows, :] * (DK_A ** -0.5)
        k = k_ref[rows, :]
        q_dec = q * jnp.exp(b)
        k_inv = k * jnp.exp(-b)
        k_end = k * jnp.exp(b_end - b)
        decay = jnp.exp(b_end)
        for p in range(n_pairs):
            ql = slice(p * qw, (p + 1) * qw)
            q_t, k_t, k_e = stack_heads(q_dec[:, ql]), stack_heads(k_inv[:, ql]), stack_heads(k_end[:, ql])
            v0 = v_ref[rows, pl.ds(p * vw, DV_A)]
            v1 = v_ref[rows, pl.ds(p * vw + DV_A, DV_A)]
            v_s = jnp.concatenate([v0, v1], axis=0).astype(BF16)
            att = lax.dot_general(q_t, k_t, nt, preferred_element_type=F32)
            att = jnp.where(causal, att, 0.0).astype(BF16)
            st = st_ref[p]
            o = (jnp.dot(att, v_s, preferred_element_type=F32)
                 + lax.dot_general(q_t, st.astype(BF16), nt, preferred_element_type=F32))
            st_ref[p] = decay[:, ql] * st + lax.dot_general(v_s, k_e, tn, preferred_element_type=F32)
            ms = jnp.mean(jnp.square(o), axis=-1, keepdims=True)
            o = (o * lax.rsqrt(ms + EPS) * gn_ref[...])
            g0 = g_ref[rows, pl.ds(p * vw, DV_A)]
            g1 = g_ref[rows, pl.ds(p * vw + DV_A, DV_A)]
            o_ref[rows, pl.ds(p * vw, DV_A)] = (o[:c] * (g0 * jax.nn.sigmoid(g0))).astype(o_ref.dtype)
            o_ref[rows, pl.ds(p * vw + DV_A, DV_A)] = (o[c:] * (g1 * jax.nn.sigmoid(g1))).astype(o_ref.dtype)

    @pl.when(n == pl.num_programs(2) - 1)
    def _():
        for p in range(n_pairs):
            s_out_ref[0, p] = st_ref[p].T


def _gla(hproj, alr, wa, ba, gn, s0, *, n_seq, seq_len, col_q, col_k, col_v, col_g):
    t_total = hproj.shape[0]
    h_a = s0.shape[1]
    pairs = h_a // 2
    pps = GLA_PAIRS_PER_STEP
    assert pairs % pps == 0
    chunk = min(CHUNK, seq_len)
    n_chunks = min(8, seq_len // chunk)
    tb = chunk * n_chunks
    nb = seq_len // tb
    qw, vw = 2 * DK_A * pps, 2 * DV_A * pps
    assert col_q % qw == 0 and col_k % qw == 0 and col_v % vw == 0 and col_g % vw == 0
    cq, ck, cv, cg = col_q // qw, col_k // qw, col_v // vw, col_g // vw
    row = lambda s, n: s * nb + n
    in_specs = [
        pl.BlockSpec((tb, qw), lambda s, p, n: (row(s, n), cq + p)),
        pl.BlockSpec((tb, qw), lambda s, p, n: (row(s, n), ck + p)),
        pl.BlockSpec((tb, vw), lambda s, p, n: (row(s, n), cv + p)),
        pl.BlockSpec((tb, vw), lambda s, p, n: (row(s, n), cg + p)),
        pl.BlockSpec((tb, LANES), lambda s, p, n: (row(s, n), 0)),
        pl.BlockSpec((LANES, qw), lambda s, p, n: (0, p)),
        pl.BlockSpec((1, qw), lambda s, p, n: (0, p)),
        pl.BlockSpec((1, DV_A), lambda s, p, n: (0, 0)),
        pl.BlockSpec((1, pps, 2 * DK_A, DV_A), lambda s, p, n: (s, p, 0, 0)),
    ]
    out_specs = [
        pl.BlockSpec((tb, vw), lambda s, p, n: (row(s, n), p)),
        pl.BlockSpec((1, pps, 2 * DK_A, DV_A), lambda s, p, n: (s, p, 0, 0)),
    ]
    o_a, s_new = pl.pallas_call(
        functools.partial(_gla_body, chunk=chunk, n_chunks=n_chunks, n_pairs=pps),
        grid=(n_seq, pairs // pps, nb), in_specs=in_specs, out_specs=out_specs,
        out_shape=[jax.ShapeDtypeStruct((t_total, h_a * DV_A), BF16),
                   jax.ShapeDtypeStruct((n_seq, pairs, 2 * DK_A, DV_A), F32)],
        scratch_shapes=[pltpu.VMEM((pps, DV_A, 2 * DK_A), F32)],
        compiler_params=pltpu.CompilerParams(
            dimension_semantics=("parallel", "parallel", "arbitrary"),
            vmem_limit_bytes=_vmem_limit(_nbytes((tb, 2 * qw + 2 * vw + LANES), F32), _nbytes((tb, vw), BF16))),
        name="gla")(hproj, hproj, hproj, hproj, alr, wa, ba, gn, s0.reshape(n_seq, pairs, 2 * DK_A, DV_A))
    return o_a, s_new.reshape(n_seq, h_a, DK_A, DV_A)


def _gmlp_body(u_ref, v_ref, w_ref, b_ref, o_ref, *vg_refs, c, dh):
    tril = (lax.broadcasted_iota(jnp.int32, (c, c), 0) >= lax.broadcasted_iota(jnp.int32, (c, c), 1))
    w_tril = [jnp.where(tril, w_ref[h, :c, :c], 0.0).astype(BF16) for h in range(w_ref.shape[0])]
    for ci in range(u_ref.shape[0] // c):
        rows = pl.ds(ci * c, c)
        vg = jax.nn.gelu(v_ref[rows, :])
        if vg_refs:
            vg_refs[0][rows, :] = vg
        ug = jax.nn.gelu(u_ref[rows, :])
        for h, w in enumerate(w_tril):
            cols = pl.ds(h * dh, dh)
            mixed = (jnp.dot(w, vg[:, h * dh:(h + 1) * dh].astype(BF16), preferred_element_type=F32)
                     + b_ref[:c, cols])
            o_ref[rows, cols] = (ug[:, h * dh:(h + 1) * dh] * mixed).astype(o_ref.dtype)


def _gmlp(hproj, w_sp, b_full, *, seq_len, col_u, col_v, want_v):
    t_total = hproj.shape[0]
    n_h, gc, _ = w_sp.shape
    w_b = b_full.shape[1]
    dh = w_b // n_h
    c = min(gc, seq_len)
    tb = c * min(4, seq_len // c)
    blk = lambda col: pl.BlockSpec((tb, w_b), lambda r: (r, col // w_b))
    out_specs = [pl.BlockSpec((tb, w_b), lambda r: (r, 0))]
    out_shape = [jax.ShapeDtypeStruct((t_total, w_b), BF16)]
    if want_v:
        out_specs.append(pl.BlockSpec((tb, w_b), lambda r: (r, 0)))
        out_shape.append(jax.ShapeDtypeStruct((t_total, w_b), F32))
    outs = pl.pallas_call(
        functools.partial(_gmlp_body, c=c, dh=dh), grid=(t_total // tb,),
        in_specs=[blk(col_u), blk(col_v),
                  pl.BlockSpec((n_h, gc, gc), lambda r: (0, 0, 0)),
                  pl.BlockSpec((gc, w_b), lambda r: (0, 0))],
        out_specs=out_specs, out_shape=out_shape,
        compiler_params=pltpu.CompilerParams(
            dimension_semantics=("parallel",),
            vmem_limit_bytes=_vmem_limit(4 * _nbytes((tb, w_b), F32), _nbytes((n_h, gc, gc), F32))),
        name="gmlp")(hproj, hproj, w_sp, b_full)
    return (outs[0], outs[1]) if want_v else (outs[0], None)


def _lru_body(x_ref, gate_ref, cw_ref, cb_ref, wr_ref, br_ref, wi_ref, bi_ref, lam_ref, h0_ref, tail0_ref,
              o_ref, h_ref, tail_ref, *, dh):
    n = pl.program_id(1)

    @pl.when(n == 0)
    def _():
        h_ref[...] = h0_ref[...]
        tail_ref[...] = tail0_ref[...]

    tb = x_ref.shape[0]
    row8 = lax.broadcasted_iota(jnp.int32, (SUBLANES, dh), 0)
    for hb in range(x_ref.shape[1] // dh):
        cols = pl.ds(hb * dh, dh)
        x = x_ref[:, cols]
        tail = tail_ref[0, :, cols]

        def delayed(j):
            rolled = pltpu.roll(x, j, 0)
            head = jnp.where(row8 >= j, rolled[:SUBLANES], pltpu.roll(tail, j, 0))
            return jnp.concatenate([head, rolled[SUBLANES:]], axis=0)

        y = cb_ref[:, cols] + delayed(CONV_W - 1) * cw_ref[0:1, cols]
        for j in range(1, CONV_W - 1):
            y = y + delayed(CONV_W - 1 - j) * cw_ref[j:j + 1, cols]
        y = y + x * cw_ref[CONV_W - 1:CONV_W, cols]
        tail_ref[0, :, cols] = x[tb - SUBLANES:]

        yb = y.astype(BF16)
        r = jax.nn.sigmoid(jnp.dot(yb, wr_ref[hb].astype(BF16), preferred_element_type=F32) + br_ref[:, cols])
        ig = jax.nn.sigmoid(jnp.dot(yb, wi_ref[hb].astype(BF16), preferred_element_type=F32) + bi_ref[:, cols])
        log_a = -LRU_C * r * _softplus(-lam_ref[:, cols])
        a = jnp.exp(log_a)
        bt = jnp.sqrt(-jnp.tanh(log_a) * (a * a + 1.0)) * (ig * y)
        hseq, h_last = _linear_scan_rows(a, bt, h_ref[0, :, cols])
        h_ref[0, :, cols] = h_last
        o_ref[:, cols] = (hseq * jax.nn.gelu(gate_ref[:, cols])).astype(o_ref.dtype)


def _lru(hproj, cw, cb, wr, br, wi, bi, lam, h0, tail0, *, n_seq, seq_len, col_x, col_gate):
    t_total = hproj.shape[0]
    w_c = cw.shape[1]
    n_h, dh, _ = wr.shape
    tb = min(seq_len, 256)
    nb = seq_len // tb
    vec = lambda rows: pl.BlockSpec((rows, w_c), lambda s, n: (0, 0))
    wspec = pl.BlockSpec((n_h, dh, dh), lambda s, n: (0, 0, 0))
    hspec = pl.BlockSpec((1, 1, w_c), lambda s, n: (s, 0, 0))
    tspec = pl.BlockSpec((1, SUBLANES, w_c), lambda s, n: (s, 0, 0))
    o_c, h_new, tail = pl.pallas_call(
        functools.partial(_lru_body, dh=dh), grid=(n_seq, nb),
        in_specs=[pl.BlockSpec((tb, w_c), lambda s, n: (s * nb + n, col_x // w_c)),
                  pl.BlockSpec((tb, w_c), lambda s, n: (s * nb + n, col_gate // w_c)),
                  vec(CONV_W), vec(1), wspec, vec(1), wspec, vec(1), vec(1), hspec, tspec],
        out_specs=[pl.BlockSpec((tb, w_c), lambda s, n: (s * nb + n, 0)), hspec, tspec],
        out_shape=[jax.ShapeDtypeStruct((t_total, w_c), BF16),
                   jax.ShapeDtypeStruct((n_seq, 1, w_c), F32),
                   jax.ShapeDtypeStruct((n_seq, SUBLANES, w_c), F32)],
        compiler_params=pltpu.CompilerParams(
            dimension_semantics=("parallel", "arbitrary"),
            vmem_limit_bytes=_vmem_limit(4 * _nbytes((tb, w_c), F32))),
        name="conv_rglru")(hproj, hproj, cw, cb.reshape(1, w_c), wr, br.reshape(1, w_c), wi,
                           bi.reshape(1, w_c), lam.reshape(1, w_c), h0.reshape(n_seq, 1, w_c), tail0)
    return o_c, h_new.reshape(n_seq, w_c), tail


def _mixers(hproj, alr, s_gla, s_conv, s_lru, w, *, n_seq, seq_len, want_v):
    w_a = s_gla.shape[1] * DV_A
    qk_a = s_gla.shape[1] * DK_A
    w_b = w_c = s_lru.shape[1]
    col_u = 2 * qk_a + 2 * w_a
    o_a, s_gla_new = _gla(hproj, alr, w["w_alpha2"], w["b_alpha"], w["g_onorm"], s_gla, n_seq=n_seq,
                          seq_len=seq_len, col_q=0, col_k=qk_a, col_v=2 * qk_a, col_g=2 * qk_a + w_a)
    o_b, vg = _gmlp(hproj, w["w_spatial"], w["b_spatial"], seq_len=seq_len, col_u=col_u, col_v=col_u + w_b,
                    want_v=want_v)
    tail0 = jnp.pad(s_conv, ((0, 0), (SUBLANES - (CONV_W - 1), 0), (0, 0)))
    o_c, s_lru_new, tail = _lru(hproj, w["conv_w"], w["conv_b"], w["w_rgate"], w["b_rgate"], w["w_igate"],
                                w["b_igate"], w["lru_lambda"], s_lru, tail0, n_seq=n_seq, seq_len=seq_len,
                                col_x=col_u + 2 * w_b, col_gate=col_u + 2 * w_b + w_c)
    return [o_a, o_b, o_c], (s_gla_new, tail[:, SUBLANES - (CONV_W - 1):], s_lru_new, vg)


def _small_layer_weights(l, w_in_t, w_alpha2, b_alpha, g_onorm, w_spatial, b_spatial, conv_w, conv_b,
                         w_rgate, b_rgate, w_igate, b_igate, lru_lambda):
    d = w_in_t.shape[2]
    w_a, w_b = d // 2, d // 4
    qk_a = (w_a // DV_A) * DK_A
    c_alr = 2 * qk_a + 2 * w_a
    w_alr = jnp.pad(w_in_t[l, c_alr:c_alr + GLA_RANK, :], ((0, LANES - GLA_RANK), (0, 0)))
    w_alpha2_p = jnp.pad(w_alpha2[l], ((0, LANES - GLA_RANK), (0, 0))).astype(BF16)
    b_full = jnp.repeat(b_spatial[l].T, w_b // H_B, axis=1)
    return dict(
        w_alr=w_alr, w_alpha2=w_alpha2_p, b_alpha=b_alpha[l].reshape(1, qk_a),
        g_onorm=g_onorm[l].reshape(1, DV_A), w_spatial=w_spatial[l], b_spatial=b_full,
        conv_w=conv_w[l], conv_b=conv_b[l], w_rgate=w_rgate[l], b_rgate=b_rgate[l],
        w_igate=w_igate[l], b_igate=b_igate[l], lru_lambda=lru_lambda[l])


def kernel(x_prompt, x_sample, state_gla, state_conv, state_lru, ln1, w_in, w_alpha2, b_alpha, g_onorm, w_spatial, b_spatial, conv_w, conv_b, w_rgate, b_rgate, w_igate, b_igate, lru_lambda, w_out, ln2, w_up, w_down, ln_final):
    bp, tp, d = x_prompt.shape
    bs, ts, _ = x_sample.shape
    depth = w_in.shape[0]
    w_a, w_b, w_c = d // 2, d // 4, d // 4
    h_a = w_a // DV_A
    qk_a = h_a * DK_A
    n_a = 2 * qk_a + 2 * w_a
    n_b = 2 * w_b + 2 * w_c
    zero_gla = jnp.zeros((bp, h_a, DK_A, DV_A), F32)
    zero_conv = jnp.zeros((bp, CONV_W - 1, w_c), F32)
    zero_lru = jnp.zeros((bp, w_c), F32)
    xp = x_prompt.reshape(bp * tp, d)
    xs = x_sample.reshape(bs * ts, d)
    w_in_t = jnp.swapaxes(w_in, 1, 2)
    mm = _ws_matmul
    st_p, st_s = [], []
    for l in range(depth):
        w = _small_layer_weights(l, w_in_t, w_alpha2, b_alpha, g_onorm, w_spatial, b_spatial, conv_w, conv_b,
                                 w_rgate, b_rgate, w_igate, b_igate, lru_lambda)
        xn_p, alr_p = _rmsnorm(xp, ln1[l], BF16, w_proj=w["w_alr"])
        xn_s, alr_s = _rmsnorm(xs, ln1[l], BF16, w_proj=w["w_alr"])
        hp_p, hp_s = mm([xn_p], [xn_s], w_in_t, l, n_cols=n_a + n_b, transposed=True,
                        w_row_gap=(n_a, GLA_RANK), bm=1024, name="in_proj")
        mix_p, new_p = _mixers(hp_p, alr_p, zero_gla, zero_conv, zero_lru, w, n_seq=bp, seq_len=tp,
                               want_v=False)
        mix_s, new_s = _mixers(hp_s, alr_s, state_gla[l], state_conv[l], state_lru[l], w, n_seq=bs,
                               seq_len=ts, want_v=True)
        st_p.append(new_p)
        st_s.append(new_s)
        xp, xs = mm(mix_p, mix_s, w_out, l, n_cols=d, res=(xp, xs), name="out_proj")
        xn_p = _rmsnorm(xp, ln2[l], BF16)
        xn_s = _rmsnorm(xs, ln2[l], BF16)
        hid_p, hid_s, w_down_bf = mm([xn_p], [xn_s], w_up, l, n_cols=w_up.shape[2], act="relu2",
                                     out_dtype=BF16, side_cast=(w_down, l), bm=1024, name="mlp_up")
        xp, xs = _matmul([hid_p], [hid_s], w_down_bf[None], 0, n_cols=d, bm=1024, bn=1024, bk=2048,
                         res=(xp, xs), name="mlp_down")
    y_prompt = _rmsnorm(xp, ln_final, F32).reshape(bp, tp, d)
    y_sample = _rmsnorm(xs, ln_final, F32).reshape(bs, ts, d)
    stack = lambda sts, i: jnp.stack([s[i] for s in sts])
    return (y_prompt, y_sample, stack(st_p, 0), stack(st_p, 1), stack(st_p, 2),
            stack(st_s, 0), stack(st_s, 1), stack(st_s, 2),
            jnp.stack([s[3].reshape(bs, ts, w_b) for s in st_s]))
```

```python
import functools

import jax
import jax.numpy as jnp
from jax import lax
from jax.experimental import pallas as pl
from jax.experimental.pallas import tpu as pltpu

F32 = jnp.float32
BF16 = jnp.bfloat16

DV_A = 128
DK_A = 64
GLA_RANK = 16
GLA_TAU = 16.0
CHUNK = 64
GMLP_CHUNK = 128
H_B = 8
H_C = 8
CONV_W = 4
LRU_C = 8.0
EPS = 1e-6

LANES = 128
SUBLANES = 8
V7X_VMEM_BYTES = 64 * 1024 * 1024
VMEM_CAP = V7X_VMEM_BYTES - 6 * 1024 * 1024

GLA_PAIRS_PER_STEP = 8


def _vmem_limit(*block_bytes):
    need = 4 * sum(block_bytes)
    return int(min(max(need, 16 * 1024 * 1024), VMEM_CAP))


def _nbytes(shape, dtype):
    n = 1
    for s in shape:
        n *= s
    return n * jnp.dtype(dtype).itemsize


def _rmsnorm_body(x_ref, g_ref, o_ref):
    x = x_ref[...]
    ms = jnp.mean(jnp.square(x), axis=-1, keepdims=True)
    o_ref[...] = (x * lax.rsqrt(ms + EPS) * g_ref[...]).astype(o_ref.dtype)


def _rmsnorm_proj_body(x_ref, g_ref, w_ref, o_ref, p_ref):
    x = x_ref[...]
    ms = jnp.mean(jnp.square(x), axis=-1, keepdims=True)
    y = (x * lax.rsqrt(ms + EPS) * g_ref[...]).astype(BF16)
    o_ref[...] = y
    p_ref[...] = lax.dot_general(y, w_ref[...].astype(BF16), (((1,), (1,)), ((), ())),
                                 preferred_element_type=F32)


def _rmsnorm(x, g, out_dtype, w_proj=None):
    t, d = x.shape
    tb = min(t, 512)
    grid = (t // tb,)
    x_spec = pl.BlockSpec((tb, d), lambda i: (i, 0))
    g_spec = pl.BlockSpec((1, d), lambda i: (0, 0))
    params = pltpu.CompilerParams(
        dimension_semantics=("parallel",),
        vmem_limit_bytes=_vmem_limit(_nbytes((tb, d), F32), _nbytes((tb, d), out_dtype)))
    if w_proj is None:
        return pl.pallas_call(
            _rmsnorm_body, grid=grid, in_specs=[x_spec, g_spec], out_specs=x_spec,
            out_shape=jax.ShapeDtypeStruct((t, d), out_dtype), compiler_params=params,
            name="rmsnorm")(x, g.reshape(1, d))
    n = w_proj.shape[0]
    return pl.pallas_call(
        _rmsnorm_proj_body, grid=grid,
        in_specs=[x_spec, g_spec, pl.BlockSpec((n, d), lambda i: (0, 0))],
        out_specs=[x_spec, pl.BlockSpec((tb, n), lambda i: (i, 0))],
        out_shape=[jax.ShapeDtypeStruct((t, d), BF16), jax.ShapeDtypeStruct((t, n), F32)],
        compiler_params=params, name="rmsnorm_proj")(x, g.reshape(1, d), w_proj)


def _matmul_body(*refs, k_bounds, has_res, act, nk, n_prompt_blocks):
    n_lhs = len(k_bounds)
    lhs_p, lhs_s = refs[:n_lhs], refs[n_lhs:2 * n_lhs]
    w_ref = refs[2 * n_lhs]
    res_p, res_s = (refs[2 * n_lhs + 1], refs[2 * n_lhs + 2]) if has_res else (None, None)
    o_p, o_s = refs[-2], refs[-1]
    i = pl.program_id(0)

    def run(lhs_refs, res_ref, o_ref):
        if nk > 1:
            @pl.when(pl.program_id(2) == 0)
            def _():
                o_ref[...] = res_ref[...] if has_res else jnp.zeros(o_ref.shape, o_ref.dtype)

        acc = None
        for a_ref, (k0, k1) in zip(lhs_refs, k_bounds):
            part = jnp.dot(a_ref[...], w_ref[k0:k1, :].astype(BF16), preferred_element_type=F32)
            acc = part if acc is None else acc + part
        if nk > 1:
            o_ref[...] += acc
        else:
            if has_res:
                acc = res_ref[...] + acc
            if act == "relu2":
                acc = jnp.square(jnp.maximum(acc, 0.0))
            o_ref[...] = acc.astype(o_ref.dtype)

    @pl.when(i < n_prompt_blocks)
    def _():
        run(lhs_p, res_p, o_p)

    @pl.when(i == n_prompt_blocks)
    def _():
        run(lhs_s, res_s, o_s)


def _matmul(lhs_p, lhs_s, w, layer, *, n_cols, bm, bn, bk=None, res=None, act=None, out_dtype=F32,
            name="matmul"):
    mp, ms = lhs_p[0].shape[0], lhs_s[0].shape[0]
    k_total = w.shape[1]
    widths = [a.shape[1] for a in lhs_p]
    assert sum(widths) == k_total and mp % bm == 0 and n_cols % bn == 0
    if bk is None:
        bk = k_total
    nk = k_total // bk
    assert nk * bk == k_total and (nk == 1 or (len(lhs_p) == 1 and out_dtype == F32 and act is None))
    n_p, n_j = mp // bm, n_cols // bn

    def p_row(i):
        return jnp.minimum(i, n_p - 1)

    def p_col(i, j):
        return jnp.where(i < n_p, j, n_j - 1)

    def s_col(i, j):
        return jnp.where(i == n_p, j, 0)

    if nk == 1:
        bounds, k0 = [], 0
        for wd in widths:
            bounds.append((k0, k0 + wd))
            k0 += wd
        specs_p = [pl.BlockSpec((bm, wd), lambda i, j, k: (p_row(i), 0)) for wd in widths]
        specs_s = [pl.BlockSpec((ms, wd), lambda i, j, k: (0, 0)) for wd in widths]
    else:
        bounds = [(0, bk)]
        specs_p = [pl.BlockSpec((bm, bk), lambda i, j, k: (p_row(i), jnp.where(i < n_p, k, nk - 1)))]
        specs_s = [pl.BlockSpec((ms, bk), lambda i, j, k: (0, jnp.where(i == n_p, k, 0)))]
    w_spec = pl.BlockSpec((None, bk, bn), lambda i, j, k: (layer, k, j))
    io_p = pl.BlockSpec((bm, bn), lambda i, j, k: (p_row(i), p_col(i, j)))
    io_s = pl.BlockSpec((ms, bn), lambda i, j, k: (0, s_col(i, j)))
    in_specs = specs_p + specs_s + [w_spec]
    args = list(lhs_p) + list(lhs_s) + [w]
    if res is not None:
        in_specs += [io_p, io_s]
        args += list(res)
    body = functools.partial(_matmul_body, k_bounds=tuple(bounds), has_res=res is not None, act=act, nk=nk,
                             n_prompt_blocks=n_p)
    n_io = 2 if res is not None else 1
    limit = _vmem_limit(_nbytes((bm + ms, bk), BF16), _nbytes((bk, bn), F32),
                        n_io * _nbytes((bm + ms, bn), F32))
    return pl.pallas_call(
        body, grid=(n_p + 1, n_j, nk), in_specs=in_specs, out_specs=[io_p, io_s],
        out_shape=[jax.ShapeDtypeStruct((mp, n_cols), out_dtype), jax.ShapeDtypeStruct((ms, n_cols), out_dtype)],
        compiler_params=pltpu.CompilerParams(
            dimension_semantics=("arbitrary", "arbitrary", "arbitrary"), vmem_limit_bytes=limit),
        name=name)(*args)


def _ws_matmul_body(*refs, k_bounds, has_res, has_side, act, n_prompt_blocks, chunk_rows, transposed):
    n_lhs = len(k_bounds)
    lhs_p, lhs_s = refs[:n_lhs], refs[n_lhs:2 * n_lhs]
    wc_ref = refs[2 * n_lhs]
    res_p, res_s = (refs[2 * n_lhs + 1], refs[2 * n_lhs + 2]) if has_res else (None, None)
    w_scr = refs[-1]
    if has_side:
        side_in, side_out = refs[-5], refs[-2]
        o_p, o_s = refs[-4], refs[-3]
    else:
        o_p, o_s = refs[-3], refs[-2]
    jj, i = pl.program_id(0), pl.program_id(1)
    slot_load = jj % 2
    slot_use = 1 - slot_load

    def stage():
        r0 = pl.multiple_of(jnp.minimum(i, n_prompt_blocks - 1) * chunk_rows, chunk_rows)
        chunk = wc_ref[0] if len(wc_ref.shape) == 3 else wc_ref[...]
        w_scr[slot_load, pl.ds(r0, chunk_rows), :] = chunk.astype(BF16)

    def run(lhs_refs, res_ref, o_ref):
        stage()
        acc = None
        for a_ref, (k0, k1) in zip(lhs_refs, k_bounds):
            if transposed:
                part = lax.dot_general(a_ref[...], w_scr[slot_use, :, k0:k1], (((1,), (1,)), ((), ())),
                                       preferred_element_type=F32)
            else:
                part = jnp.dot(a_ref[...], w_scr[slot_use, k0:k1, :], preferred_element_type=F32)
            acc = part if acc is None else acc + part
        if has_res:
            acc = res_ref[...] + acc
        if act == "relu2":
            acc = jnp.square(jnp.maximum(acc, 0.0))
        o_ref[...] = acc.astype(o_ref.dtype)

    @pl.when(jj == 0)
    def _():
        stage()

    @pl.when((jj > 0) & (i < n_prompt_blocks))
    def _():
        if has_side:
            side_out[...] = side_in[...].astype(side_out.dtype)
        run(lhs_p, res_p, o_p)

    @pl.when((jj > 0) & (i == n_prompt_blocks))
    def _():
        run(lhs_s, res_s, o_s)


def _ws_matmul(lhs_p, lhs_s, w, layer, *, n_cols, transposed=False, w_row_gap=(0, 0), bm=512, bn=1024,
               res=None, act=None, out_dtype=F32, side_cast=None, name="ws_matmul"):
    mp, ms = lhs_p[0].shape[0], lhs_s[0].shape[0]
    k_total = w.shape[2] if transposed else w.shape[1]
    widths = [a.shape[1] for a in lhs_p]
    assert sum(widths) == k_total and mp % bm == 0 and n_cols % bn == 0
    n_p, n_j = mp // bm, n_cols // bn
    bounds, k0 = [], 0
    for wd in widths:
        bounds.append((k0, k0 + wd))
        k0 += wd

    def row_blk(jj, i):
        return jnp.where(jj == 0, 0, jnp.minimum(i, n_p - 1))

    def lhs_row_blk(jj, i):
        return jnp.where((jj == 0) | (i == n_p), 0, i)

    def col_use(jj):
        return jnp.maximum(jj - 1, 0)

    def chunk(jj, i):
        return jnp.where(jj < n_j, jnp.minimum(i, n_p - 1), n_p - 1)

    def col_load(jj):
        return jnp.minimum(jj, n_j - 1)

    if transposed:
        chunk_rows = bn // n_p
        gap_row, gap_rows = w_row_gap
        assert gap_row % bn == 0 and gap_rows % SUBLANES == 0

        def w_row(jj, i):
            row = (col_load(jj) * n_p + chunk(jj, i)) * chunk_rows
            return pl.multiple_of(row + jnp.where(row >= gap_row, gap_rows, 0), SUBLANES)

        w_spec = pl.BlockSpec((pl.Element(1), pl.Element(chunk_rows), pl.Element(k_total)),
                              lambda jj, i: (layer, w_row(jj, i), 0))
        scratch = pltpu.VMEM((2, bn, k_total), BF16)
    else:
        chunk_rows = k_total // n_p
        w_spec = pl.BlockSpec((None, chunk_rows, bn), lambda jj, i: (layer, chunk(jj, i), col_load(jj)))
        scratch = pltpu.VMEM((2, k_total, bn), BF16)
    assert chunk_rows * n_p == (bn if transposed else k_total) and chunk_rows % 16 == 0
    specs_p = [pl.BlockSpec((bm, wd), lambda jj, i: (lhs_row_blk(jj, i), 0)) for wd in widths]
    specs_s = [pl.BlockSpec((ms, wd), lambda jj, i: (0, 0)) for wd in widths]
    io_p = pl.BlockSpec((bm, bn), lambda jj, i: (row_blk(jj, i), col_use(jj)))
    io_s = pl.BlockSpec((ms, bn), lambda jj, i: (0, col_use(jj)))
    in_specs = specs_p + specs_s + [w_spec]
    args = list(lhs_p) + list(lhs_s) + [w]
    if res is not None:
        in_specs += [io_p, io_s]
        args += list(res)
    out_specs = [io_p, io_s]
    out_shape = [jax.ShapeDtypeStruct((mp, n_cols), out_dtype), jax.ShapeDtypeStruct((ms, n_cols), out_dtype)]
    side_bytes = 0
    if side_cast is not None:
        side, side_layer = side_cast
        side_rows, side_cols = side.shape[1], side.shape[2]
        rb = side_rows // (n_j * n_p)
        assert rb * n_j * n_p == side_rows and rb % 16 == 0

        def side_blk(jj, i):
            return jnp.where(jj == 0, 0, (jj - 1) * n_p + jnp.minimum(i, n_p - 1))

        in_specs.append(pl.BlockSpec((None, rb, side_cols), lambda jj, i: (side_layer, side_blk(jj, i), 0)))
        args.append(side)
        out_specs.append(pl.BlockSpec((rb, side_cols), lambda jj, i: (side_blk(jj, i), 0)))
        out_shape.append(jax.ShapeDtypeStruct((side_rows, side_cols), BF16))
        side_bytes = _nbytes((rb, side_cols), F32) + _nbytes((rb, side_cols), BF16)
    body = functools.partial(_ws_matmul_body, k_bounds=tuple(bounds), has_res=res is not None,
                             has_side=side_cast is not None, act=act, n_prompt_blocks=n_p,
                             chunk_rows=chunk_rows, transposed=transposed)
    n_io = 2 if res is not None else 1
    pipelined = (_nbytes((bm + ms, k_total), BF16) + _nbytes((k_total // n_p, bn), F32)
                 + n_io * _nbytes((bm + ms, bn), F32) + side_bytes)
    limit = min(VMEM_CAP, 2 * pipelined + _nbytes((2, k_total, bn), BF16) + 2 * _nbytes((bm, bn), F32))
    return pl.pallas_call(
        body, grid=(n_j + 1, n_p + 1), in_specs=in_specs, out_specs=out_specs, out_shape=out_shape,
        scratch_shapes=[scratch],
        compiler_params=pltpu.CompilerParams(
            dimension_semantics=("arbitrary", "arbitrary"), vmem_limit_bytes=int(limit)),
        name=name)(*args)


def _linear_scan_rows(a, b, carry):
    t, lanes = b.shape
    g = t // SUBLANES
    b3 = b.reshape(g, SUBLANES, lanes)
    a3 = None if a is None else a.reshape(g, SUBLANES, lanes)
    sub = lax.broadcasted_iota(jnp.int32, b3.shape, 1)
    s = 1
    while s < SUBLANES:
        keep = sub >= s
        b_prev = jnp.where(keep, pltpu.roll(b3, s, 1), 0.0)
        if a3 is None:
            b3 = b3 + b_prev
        else:
            b3 = a3 * b_prev + b3
            a3 = a3 * jnp.where(keep, pltpu.roll(a3, s, 1), 1.0)
        s *= 2
    outs = []
    for j in range(g):
        h = b3[j] + carry if a3 is None else b3[j] + a3[j] * carry
        outs.append(h)
        carry = h[SUBLANES - 1:, :]
    return jnp.concatenate(outs, axis=0), carry


def _softplus(x):
    return jnp.maximum(x, 0.0) + jnp.log1p(jnp.exp(-jnp.abs(x)))


def _gla_body(q_ref, k_ref, v_ref, g_ref, alr_ref, wa_ref, ba_ref, gn_ref, s0_ref,
              o_ref, s_out_ref, st_ref, *, chunk, n_chunks, n_pairs):
    n = pl.program_id(2)

    @pl.when(n == 0)
    def _():
        for p in range(n_pairs):
            st_ref[p] = s0_ref[0, p].T

    c = chunk
    qw, vw = 2 * DK_A, 2 * DV_A
    lane = lax.broadcasted_iota(jnp.int32, (c, qw), 1)
    head0 = lane < DK_A
    row2 = lax.broadcasted_iota(jnp.int32, (2 * c, 2 * c), 0)
    col2 = lax.broadcasted_iota(jnp.int32, (2 * c, 2 * c), 1)
    causal = ((row2 >= c) == (col2 >= c)) & (row2 >= col2)
    nt = (((1,), (1,)), ((), ()))
    tn = (((0,), (0,)), ((), ()))

    def stack_heads(z):
        return jnp.concatenate([jnp.where(head0, z, 0.0), jnp.where(head0, 0.0, z)], axis=0).astype(BF16)

    for ci in range(n_chunks):
        rows = pl.ds(ci * c, c)
        z = jnp.dot(alr_ref[rows, :].astype(BF16), wa_ref[...], preferred_element_type=F32) + ba_ref[...]
        log_alpha = -(jnp.maximum(-z, 0.0) + jnp.log(1.0 + jnp.exp(-jnp.abs(z)))) / GLA_TAU
        b, b_end = _linear_scan_rows(None, log_alpha, jnp.zeros((1, log_alpha.shape[1]), F32))
        q = q_ref[rows, :] * (DK_A ** -0.5)
        k = k_ref[rows, :]
        q_dec = q * jnp.exp(b)
        k_inv = k * jnp.exp(-b)
        k_end = k * jnp.exp(b_end - b)
        decay = jnp.exp(b_end)
        for p in range(n_pairs):
            ql = slice(p * qw, (p + 1) * qw)
            q_t, k_t, k_e = stack_heads(q_dec[:, ql]), stack_heads(k_inv[:, ql]), stack_heads(k_end[:, ql])
            v0 = v_ref[rows, pl.ds(p * vw, DV_A)]
            v1 = v_ref[rows, pl.ds(p * vw + DV_A, DV_A)]
            v_s = jnp.concatenate([v0, v1], axis=0).astype(BF16)
            att = lax.dot_general(q_t, k_t, nt, preferred_element_type=F32)
            att = jnp.where(causal, att, 0.0).astype(BF16)
            st = st_ref[p]
            o = (jnp.dot(att, v_s, preferred_element_type=F32)
                 + lax.dot_general(q_t, st.astype(BF16), nt, preferred_element_type=F32))
            st_ref[p] = decay[:, ql] * st + lax.dot_general(v_s, k_e, tn, preferred_element_type=F32)
            ms = jnp.mean(jnp.square(o), axis=-1, keepdims=True)
            o = (o * lax.rsqrt(ms + EPS) * gn_ref[...])
            g0 = g_ref[rows, pl.ds(p * vw, DV_A)]
            g1 = g_ref[rows, pl.ds(p * vw + DV_A, DV_A)]
            o_ref[rows, pl.ds(p * vw, DV_A)] = (o[:c] * (g0 * jax.nn.sigmoid(g0))).astype(o_ref.dtype)
            o_ref[rows, pl.ds(p * vw + DV_A, DV_A)] = (o[c:] * (g1 * jax.nn.sigmoid(g1))).astype(o_ref.dtype)

    @pl.when(n == pl.num_programs(2) - 1)
    def _():
        for p in range(n_pairs):
            s_out_ref[0, p] = st_ref[p].T


def _gla(hproj, alr, wa, ba, gn, s0, *, n_seq, seq_len, col_q, col_k, col_v, col_g):
    t_total = hproj.shape[0]
    h_a = s0.shape[1]
    pairs = h_a // 2
    pps = GLA_PAIRS_PER_STEP
    assert pairs % pps == 0
    chunk = min(CHUNK, seq_len)
    n_chunks = min(8, seq_len // chunk)
    tb = chunk * n_chunks
    nb = seq_len // tb
    qw, vw = 2 * DK_A * pps, 2 * DV_A * pps
    assert col_q % qw == 0 and col_k % qw == 0 and col_v % vw == 0 and col_g % vw == 0
    cq, ck, cv, cg = col_q // qw, col_k // qw, col_v // vw, col_g // vw
    row = lambda s, n: s * nb + n
    in_specs = [
        pl.BlockSpec((tb, qw), lambda s, p, n: (row(s, n), cq + p)),
        pl.BlockSpec((tb, qw), lambda s, p, n: (row(s, n), ck + p)),
        pl.BlockSpec((tb, vw), lambda s, p, n: (row(s, n), cv + p)),
        pl.BlockSpec((tb, vw), lambda s, p, n: (row(s, n), cg + p)),
        pl.BlockSpec((tb, LANES), lambda s, p, n: (row(s, n), 0)),
        pl.BlockSpec((LANES, qw), lambda s, p, n: (0, p)),
        pl.BlockSpec((1, qw), lambda s, p, n: (0, p)),
        pl.BlockSpec((1, DV_A), lambda s, p, n: (0, 0)),
        pl.BlockSpec((1, pps, 2 * DK_A, DV_A), lambda s, p, n: (s, p, 0, 0)),
    ]
    out_specs = [
        pl.BlockSpec((tb, vw), lambda s, p, n: (row(s, n), p)),
        pl.BlockSpec((1, pps, 2 * DK_A, DV_A), lambda s, p, n: (s, p, 0, 0)),
    ]
    o_a, s_new = pl.pallas_call(
        functools.partial(_gla_body, chunk=chunk, n_chunks=n_chunks, n_pairs=pps),
        grid=(n_seq, pairs // pps, nb), in_specs=in_specs, out_specs=out_specs,
        out_shape=[jax.ShapeDtypeStruct((t_total, h_a * DV_A), BF16),
                   jax.ShapeDtypeStruct((n_seq, pairs, 2 * DK_A, DV_A), F32)],
        scratch_shapes=[pltpu.VMEM((pps, DV_A, 2 * DK_A), F32)],
        compiler_params=pltpu.CompilerParams(
            dimension_semantics=("parallel", "parallel", "arbitrary"),
            vmem_limit_bytes=_vmem_limit(_nbytes((tb, 2 * qw + 2 * vw + LANES), F32), _nbytes((tb, vw), BF16))),
        name="gla")(hproj, hproj, hproj, hproj, alr, wa, ba, gn, s0.reshape(n_seq, pairs, 2 * DK_A, DV_A))
    return o_a, s_new.reshape(n_seq, h_a, DK_A, DV_A)


def _gmlp_body(u_ref, v_ref, w_ref, b_ref, o_ref, *vg_refs, c, dh):
    tril = (lax.broadcasted_iota(jnp.int32, (c, c), 0) >= lax.broadcasted_iota(jnp.int32, (c, c), 1))
    w_tril = [jnp.where(tril, w_ref[h, :c, :c], 0.0).astype(BF16) for h in range(w_ref.shape[0])]
    for ci in range(u_ref.shape[0] // c):
        rows = pl.ds(ci * c, c)
        vg = jax.nn.gelu(v_ref[rows, :])
        if vg_refs:
            vg_refs[0][rows, :] = vg
        ug = jax.nn.gelu(u_ref[rows, :])
        for h, w in enumerate(w_tril):
            cols = pl.ds(h * dh, dh)
            mixed = (jnp.dot(w, vg[:, h * dh:(h + 1) * dh].astype(BF16), preferred_element_type=F32)
                     + b_ref[:c, cols])
            o_ref[rows, cols] = (ug[:, h * dh:(h + 1) * dh] * mixed).astype(o_ref.dtype)


def _gmlp(hproj, w_sp, b_full, *, seq_len, col_u, col_v, want_v):
    t_total = hproj.shape[0]
    n_h, gc, _ = w_sp.shape
    w_b = b_full.shape[1]
    dh = w_b // n_h
    c = min(gc, seq_len)
    tb = c * min(4, seq_len // c)
    blk = lambda col: pl.BlockSpec((tb, w_b), lambda r: (r, col // w_b))
    out_specs = [pl.BlockSpec((tb, w_b), lambda r: (r, 0))]
    out_shape = [jax.ShapeDtypeStruct((t_total, w_b), BF16)]
    if want_v:
        out_specs.append(pl.BlockSpec((tb, w_b), lambda r: (r, 0)))
        out_shape.append(jax.ShapeDtypeStruct((t_total, w_b), F32))
    outs = pl.pallas_call(
        functools.partial(_gmlp_body, c=c, dh=dh), grid=(t_total // tb,),
        in_specs=[blk(col_u), blk(col_v),
                  pl.BlockSpec((n_h, gc, gc), lambda r: (0, 0, 0)),
                  pl.BlockSpec((gc, w_b), lambda r: (0, 0))],
        out_specs=out_specs, out_shape=out_shape,
        compiler_params=pltpu.CompilerParams(
            dimension_semantics=("parallel",),
            vmem_limit_bytes=_vmem_limit(4 * _nbytes((tb, w_b), F32), _nbytes((n_h, gc, gc), F32))),
        name="gmlp")(hproj, hproj, w_sp, b_full)
    return (outs[0], outs[1]) if want_v else (outs[0], None)


def _lru_body(x_ref, gate_ref, cw_ref, cb_ref, wr_ref, br_ref, wi_ref, bi_ref, lam_ref, h0_ref, tail0_ref,
              o_ref, h_ref, tail_ref, *, dh):
    n = pl.program_id(1)

    @pl.when(n == 0)
    def _():
        h_ref[...] = h0_ref[...]
        tail_ref[...] = tail0_ref[...]

    tb = x_ref.shape[0]
    row8 = lax.broadcasted_iota(jnp.int32, (SUBLANES, dh), 0)
    for hb in range(x_ref.shape[1] // dh):
        cols = pl.ds(hb * dh, dh)
        x = x_ref[:, cols]
        tail = tail_ref[0, :, cols]

        def delayed(j):
            rolled = pltpu.roll(x, j, 0)
            head = jnp.where(row8 >= j, rolled[:SUBLANES], pltpu.roll(tail, j, 0))
            return jnp.concatenate([head, rolled[SUBLANES:]], axis=0)

        y = cb_ref[:, cols] + delayed(CONV_W - 1) * cw_ref[0:1, cols]
        for j in range(1, CONV_W - 1):
            y = y + delayed(CONV_W - 1 - j) * cw_ref[j:j + 1, cols]
        y = y + x * cw_ref[CONV_W - 1:CONV_W, cols]
        tail_ref[0, :, cols] = x[tb - SUBLANES:]

        yb = y.astype(BF16)
        r = jax.nn.sigmoid(jnp.dot(yb, wr_ref[hb].astype(BF16), preferred_element_type=F32) + br_ref[:, cols])
        ig = jax.nn.sigmoid(jnp.dot(yb, wi_ref[hb].astype(BF16), preferred_element_type=F32) + bi_ref[:, cols])
        log_a = -LRU_C * r * _softplus(-lam_ref[:, cols])
        a = jnp.exp(log_a)
        bt = jnp.sqrt(-jnp.tanh(log_a) * (a * a + 1.0)) * (ig * y)
        hseq, h_last = _linear_scan_rows(a, bt, h_ref[0, :, cols])
        h_ref[0, :, cols] = h_last
        o_ref[:, cols] = (hseq * jax.nn.gelu(gate_ref[:, cols])).astype(o_ref.dtype)


def _lru(hproj, cw, cb, wr, br, wi, bi, lam, h0, tail0, *, n_seq, seq_len, col_x, col_gate):
    t_total = hproj.shape[0]
    w_c = cw.shape[1]
    n_h, dh, _ = wr.shape
    tb = min(seq_len, 256)
    nb = seq_len // tb
    vec = lambda rows: pl.BlockSpec((rows, w_c), lambda s, n: (0, 0))
    wspec = pl.BlockSpec((n_h, dh, dh), lambda s, n: (0, 0, 0))
    hspec = pl.BlockSpec((1, 1, w_c), lambda s, n: (s, 0, 0))
    tspec = pl.BlockSpec((1, SUBLANES, w_c), lambda s, n: (s, 0, 0))
    o_c, h_new, tail = pl.pallas_call(
        functools.partial(_lru_body, dh=dh), grid=(n_seq, nb),
        in_specs=[pl.BlockSpec((tb, w_c), lambda s, n: (s * nb + n, col_x // w_c)),
                  pl.BlockSpec((tb, w_c), lambda s, n: (s * nb + n, col_gate // w_c)),
                  vec(CONV_W), vec(1), wspec, vec(1), wspec, vec(1), vec(1), hspec, tspec],
        out_specs=[pl.BlockSpec((tb, w_c), lambda s, n: (s * nb + n, 0)), hspec, tspec],
        out_shape=[jax.ShapeDtypeStruct((t_total, w_c), BF16),
                   jax.ShapeDtypeStruct((n_seq, 1, w_c), F32),
                   jax.ShapeDtypeStruct((n_seq, SUBLANES, w_c), F32)],
        compiler_params=pltpu.CompilerParams(
            dimension_semantics=("parallel", "arbitrary"),
            vmem_limit_bytes=_vmem_limit(4 * _nbytes((tb, w_c), F32))),
        name="conv_rglru")(hproj, hproj, cw, cb.reshape(1, w_c), wr, br.reshape(1, w_c), wi,
                           bi.reshape(1, w_c), lam.reshape(1, w_c), h0.reshape(n_seq, 1, w_c), tail0)
    return o_c, h_new.reshape(n_seq, w_c), tail


def _mixers(hproj, alr, s_gla, s_conv, s_lru, w, *, n_seq, seq_len, want_v):
    w_a = s_gla.shape[1] * DV_A
    qk_a = s_gla.shape[1] * DK_A
    w_b = w_c = s_lru.shape[1]
    col_u = 2 * qk_a + 2 * w_a
    o_a, s_gla_new = _gla(hproj, alr, w["w_alpha2"], w["b_alpha"], w["g_onorm"], s_gla, n_seq=n_seq,
                          seq_len=seq_len, col_q=0, col_k=qk_a, col_v=2 * qk_a, col_g=2 * qk_a + w_a)
    o_b, vg = _gmlp(hproj, w["w_spatial"], w["b_spatial"], seq_len=seq_len, col_u=col_u, col_v=col_u + w_b,
                    want_v=want_v)
    tail0 = jnp.pad(s_conv, ((0, 0), (SUBLANES - (CONV_W - 1), 0), (0, 0)))
    o_c, s_lru_new, tail = _lru(hproj, w["conv_w"], w["conv_b"], w["w_rgate"], w["b_rgate"], w["w_igate"],
                                w["b_igate"], w["lru_lambda"], s_lru, tail0, n_seq=n_seq, seq_len=seq_len,
                                col_x=col_u + 2 * w_b, col_gate=col_u + 2 * w_b + w_c)
    return [o_a, o_b, o_c], (s_gla_new, tail[:, SUBLANES - (CONV_W - 1):], s_lru_new, vg)


def _small_layer_weights(l, w_in_t, w_alpha2, b_alpha, g_onorm, w_spatial, b_spatial, conv_w, conv_b,
                         w_rgate, b_rgate, w_igate, b_igate, lru_lambda):
    d = w_in_t.shape[2]
    w_a, w_b = d // 2, d // 4
    qk_a = (w_a // DV_A) * DK_A
    c_alr = 2 * qk_a + 2 * w_a
    w_alr = jnp.pad(w_in_t[l, c_alr:c_alr + GLA_RANK, :], ((0, LANES - GLA_RANK), (0, 0)))
    w_alpha2_p = jnp.pad(w_alpha2[l], ((0, LANES - GLA_RANK), (0, 0))).astype(BF16)
    b_full = jnp.repeat(b_spatial[l].T, w_b // H_B, axis=1)
    return dict(
        w_alr=w_alr, w_alpha2=w_alpha2_p, b_alpha=b_alpha[l].reshape(1, qk_a),
        g_onorm=g_onorm[l].reshape(1, DV_A), w_spatial=w_spatial[l], b_spatial=b_full,
        conv_w=conv_w[l], conv_b=conv_b[l], w_rgate=w_rgate[l], b_rgate=b_rgate[l],
        w_igate=w_igate[l], b_igate=b_igate[l], lru_lambda=lru_lambda[l])


def kernel(x_prompt, x_sample, state_gla, state_conv, state_lru, ln1, w_in, w_alpha2, b_alpha, g_onorm, w_spatial, b_spatial, conv_w, conv_b, w_rgate, b_rgate, w_igate, b_igate, lru_lambda, w_out, ln2, w_up, w_down, ln_final):
    bp, tp, d = x_prompt.shape
    bs, ts, _ = x_sample.shape
    depth = w_in.shape[0]
    w_a, w_b, w_c = d // 2, d // 4, d // 4
    h_a = w_a // DV_A
    qk_a = h_a * DK_A
    n_a = 2 * qk_a + 2 * w_a
    n_b = 2 * w_b + 2 * w_c
    zero_gla = jnp.zeros((bp, h_a, DK_A, DV_A), F32)
    zero_conv = jnp.zeros((bp, CONV_W - 1, w_c), F32)
    zero_lru = jnp.zeros((bp, w_c), F32)
    xp = x_prompt.reshape(bp * tp, d)
    xs = x_sample.reshape(bs * ts, d)
    w_in_t = jnp.swapaxes(w_in, 1, 2)
    mm = _ws_matmul
    st_p, st_s = [], []
    for l in range(depth):
        w = _small_layer_weights(l, w_in_t, w_alpha2, b_alpha, g_onorm, w_spatial, b_spatial, conv_w, conv_b,
                                 w_rgate, b_rgate, w_igate, b_igate, lru_lambda)
        xn_p, alr_p = _rmsnorm(xp, ln1[l], BF16, w_proj=w["w_alr"])
        xn_s, alr_s = _rmsnorm(xs, ln1[l], BF16, w_proj=w["w_alr"])
        hp_p, hp_s = mm([xn_p], [xn_s], w_in_t, l, n_cols=n_a + n_b, transposed=True,
                        w_row_gap=(n_a, GLA_RANK), bm=1024, name="in_proj")
        mix_p, new_p = _mixers(hp_p, alr_p, zero_gla, zero_conv, zero_lru, w, n_seq=bp, seq_len=tp,
                               want_v=False)
        mix_s, new_s = _mixers(hp_s, alr_s, state_gla[l], state_conv[l], state_lru[l], w, n_seq=bs,
                               seq_len=ts, want_v=True)
        st_p.append(new_p)
        st_s.append(new_s)
        xp, xs = mm(mix_p, mix_s, w_out, l, n_cols=d, res=(xp, xs), name="out_proj")
        xn_p = _rmsnorm(xp, ln2[l], BF16)
        xn_s = _rmsnorm(xs, ln2[l], BF16)
        hid_p, hid_s, w_down_bf = mm([xn_p], [xn_s], w_up, l, n_cols=w_up.shape[2], act="relu2",
                                     out_dtype=BF16, side_cast=(w_down, l), bm=1024, name="mlp_up")
        xp, xs = _matmul([hid_p], [hid_s], w_down_bf[None], 0, n_cols=d, bm=1024, bn=1024, bk=2048,
                         res=(xp, xs), name="mlp_down")
    y_prompt = _rmsnorm(xp, ln_final, F32).reshape(bp, tp, d)
    y_sample = _rmsnorm(xs, ln_final, F32).reshape(bs, ts, d)
    stack = lambda sts, i: jnp.stack([s[i] for s in sts])
    return (y_prompt, y_sample, stack(st_p, 0), stack(st_p, 1), stack(st_p, 2),
            stack(st_s, 0), stack(st_s, 1), stack(st_s, 2),
            jnp.stack([s[3].reshape(bs, ts, w_b) for s in st_s]))
```

```python
import functools

import jax
import jax.numpy as jnp
from jax import lax
from jax.experimental import pallas as pl
from jax.experimental.pallas import tpu as pltpu

F32 = jnp.float32
BF16 = jnp.bfloat16

DV_A = 128
DK_A = 64
GLA_RANK = 16
GLA_TAU = 16.0
CHUNK = 64
GMLP_CHUNK = 128
H_B = 8
H_C = 8
CONV_W = 4
LRU_C = 8.0
EPS = 1e-6

LANES = 128
SUBLANES = 8
V7X_VMEM_BYTES = 64 * 1024 * 1024
VMEM_CAP = V7X_VMEM_BYTES - 6 * 1024 * 1024

GLA_PAIRS_PER_STEP = 8


def _vmem_limit(*block_bytes):
    need = 4 * sum(block_bytes)
    return int(min(max(need, 16 * 1024 * 1024), VMEM_CAP))


def _nbytes(shape, dtype):
    n = 1
    for s in shape:
        n *= s
    return n * jnp.dtype(dtype).itemsize


def _rmsnorm_body(x_ref, g_ref, o_ref):
    x = x_ref[...]
    ms = jnp.mean(jnp.square(x), axis=-1, keepdims=True)
    o_ref[...] = (x * lax.rsqrt(ms + EPS) * g_ref[...]).astype(o_ref.dtype)


def _rmsnorm_proj_body(x_ref, g_ref, w_ref, o_ref, p_ref):
    x = x_ref[...]
    ms = jnp.mean(jnp.square(x), axis=-1, keepdims=True)
    y = (x * lax.rsqrt(ms + EPS) * g_ref[...]).astype(BF16)
    o_ref[...] = y
    p_ref[...] = lax.dot_general(y, w_ref[...].astype(BF16), (((1,), (1,)), ((), ())),
                                 preferred_element_type=F32)


def _rmsnorm(x, g, out_dtype, w_proj=None):
    t, d = x.shape
    tb = min(t, 512)
    grid = (t // tb,)
    x_spec = pl.BlockSpec((tb, d), lambda i: (i, 0))
    g_spec = pl.BlockSpec((1, d), lambda i: (0, 0))
    params = pltpu.CompilerParams(
        dimension_semantics=("parallel",),
        vmem_limit_bytes=_vmem_limit(_nbytes((tb, d), F32), _nbytes((tb, d), out_dtype)))
    if w_proj is None:
        return pl.pallas_call(
            _rmsnorm_body, grid=grid, in_specs=[x_spec, g_spec], out_specs=x_spec,
            out_shape=jax.ShapeDtypeStruct((t, d), out_dtype), compiler_params=params,
            name="rmsnorm")(x, g.reshape(1, d))
    n = w_proj.shape[0]
    return pl.pallas_call(
        _rmsnorm_proj_body, grid=grid,
        in_specs=[x_spec, g_spec, pl.BlockSpec((n, d), lambda i: (0, 0))],
        out_specs=[x_spec, pl.BlockSpec((tb, n), lambda i: (i, 0))],
        out_shape=[jax.ShapeDtypeStruct((t, d), BF16), jax.ShapeDtypeStruct((t, n), F32)],
        compiler_params=params, name="rmsnorm_proj")(x, g.reshape(1, d), w_proj)


def _inv_mean_square_body(p_ref, o_ref, *, n_parts, width):
    ss = p_ref[:, 0:LANES]
    for j in range(1, n_parts):
        ss = ss + p_ref[:, j * LANES:(j + 1) * LANES]
    o_ref[...] = 1.0 / (ss / width + EPS)


def _inv_mean_square(ssq_parts, width):
    t, cols = ssq_parts.shape
    tb = min(t, 1024)
    return pl.pallas_call(
        functools.partial(_inv_mean_square_body, n_parts=cols // LANES, width=width), grid=(t // tb,),
        in_specs=[pl.BlockSpec((tb, cols), lambda i: (i, 0))],
        out_specs=pl.BlockSpec((tb, LANES), lambda i: (i, 0)),
        out_shape=jax.ShapeDtypeStruct((t, LANES), F32),
        compiler_params=pltpu.CompilerParams(dimension_semantics=("parallel",)),
        name="inv_mean_square")(ssq_parts)


def _ktiled_matmul_body(a_p, a_s, w_ref, res_p, res_s, scale_p, scale_s, o_p, o_s, *, n_prompt_blocks):
    i = pl.program_id(0)

    def run(a_ref, res_ref, scale_ref, o_ref):
        @pl.when(pl.program_id(2) == 0)
        def _():
            o_ref[...] = res_ref[...]

        acc = jnp.dot(a_ref[...], w_ref[...], preferred_element_type=F32)
        o_ref[...] += acc * scale_ref[:, 0:1]

    @pl.when(i < n_prompt_blocks)
    def _():
        run(a_p, res_p, scale_p, o_p)

    @pl.when(i == n_prompt_blocks)
    def _():
        run(a_s, res_s, scale_s, o_s)


def _ktiled_matmul(a_p, a_s, w, res, scale, *, bm, bn, bk, name):
    mp, ms = a_p.shape[0], a_s.shape[0]
    k_total, n_cols = w.shape
    assert mp % bm == 0 and n_cols % bn == 0 and k_total % bk == 0
    n_p, n_j, nk = mp // bm, n_cols // bn, k_total // bk

    def p_row(i):
        return jnp.minimum(i, n_p - 1)

    def p_col(i, j):
        return jnp.where(i < n_p, j, n_j - 1)

    def s_col(i, j):
        return jnp.where(i == n_p, j, 0)

    io_p = pl.BlockSpec((bm, bn), lambda i, j, k: (p_row(i), p_col(i, j)))
    io_s = pl.BlockSpec((ms, bn), lambda i, j, k: (0, s_col(i, j)))
    in_specs = [
        pl.BlockSpec((bm, bk), lambda i, j, k: (p_row(i), jnp.where(i < n_p, k, nk - 1))),
        pl.BlockSpec((ms, bk), lambda i, j, k: (0, jnp.where(i == n_p, k, 0))),
        pl.BlockSpec((bk, bn), lambda i, j, k: (k, j)),
        io_p, io_s,
        pl.BlockSpec((bm, LANES), lambda i, j, k: (p_row(i), 0)),
        pl.BlockSpec((ms, LANES), lambda i, j, k: (0, 0)),
    ]
    limit = _vmem_limit(_nbytes((bm + ms, bk), BF16), _nbytes((bk, bn), BF16), 2 * _nbytes((bm + ms, bn), F32))
    return pl.pallas_call(
        functools.partial(_ktiled_matmul_body, n_prompt_blocks=n_p),
        grid=(n_p + 1, n_j, nk), in_specs=in_specs, out_specs=[io_p, io_s],
        out_shape=[jax.ShapeDtypeStruct((mp, n_cols), F32), jax.ShapeDtypeStruct((ms, n_cols), F32)],
        compiler_params=pltpu.CompilerParams(
            dimension_semantics=("arbitrary", "arbitrary", "arbitrary"), vmem_limit_bytes=limit),
        name=name)(a_p, a_s, w, res[0], res[1], scale[0], scale[1])


def _ws_matmul_body(*refs, k_bounds, has_res, has_side, has_gain, act, n_prompt_blocks, chunk_rows,
                    transposed):
    refs = list(refs)
    n_lhs = len(k_bounds)
    take = lambda n: [refs.pop(0) for _ in range(n)]
    lhs_p, lhs_s = take(n_lhs), take(n_lhs)
    (wc_ref,) = take(1)
    res_p, res_s = take(2) if has_res else (None, None)
    (side_in,) = take(1) if has_side else (None,)
    (gain_ref,) = take(1) if has_gain else (None,)
    o_p, o_s = take(2)
    (side_out,) = take(1) if has_side else (None,)
    gained_p, gained_s, ssq_p, ssq_s = take(4) if has_gain else (None,) * 4
    (w_scr,) = take(1)
    jj, i = pl.program_id(0), pl.program_id(1)
    slot_load = jj % 2
    slot_use = 1 - slot_load

    def stage():
        r0 = pl.multiple_of(jnp.minimum(i, n_prompt_blocks - 1) * chunk_rows, chunk_rows)
        chunk = wc_ref[0] if len(wc_ref.shape) == 3 else wc_ref[...]
        w_scr[slot_load, pl.ds(r0, chunk_rows), :] = chunk.astype(BF16)

    def run(lhs_refs, res_ref, o_ref, gained_ref, ssq_ref):
        stage()
        acc = None
        for a_ref, (k0, k1) in zip(lhs_refs, k_bounds):
            if transposed:
                part = lax.dot_general(a_ref[...], w_scr[slot_use, :, k0:k1], (((1,), (1,)), ((), ())),
                                       preferred_element_type=F32)
            else:
                part = jnp.dot(a_ref[...], w_scr[slot_use, k0:k1, :], preferred_element_type=F32)
            acc = part if acc is None else acc + part
        if has_res:
            acc = res_ref[...] + acc
        if act == "relu2":
            acc = jnp.square(jnp.maximum(acc, 0.0))
        o_ref[...] = acc.astype(o_ref.dtype)
        if has_gain:
            gained_ref[...] = (acc * gain_ref[...]).astype(gained_ref.dtype)
            ssq_ref[...] = jnp.broadcast_to(jnp.sum(acc * acc, axis=1, keepdims=True), ssq_ref.shape)

    @pl.when(jj == 0)
    def _():
        stage()

    @pl.when((jj > 0) & (i < n_prompt_blocks))
    def _():
        if has_side:
            side_out[...] = side_in[...].astype(side_out.dtype)
        run(lhs_p, res_p, o_p, gained_p, ssq_p)

    @pl.when((jj > 0) & (i == n_prompt_blocks))
    def _():
        run(lhs_s, res_s, o_s, gained_s, ssq_s)


def _ws_matmul(lhs_p, lhs_s, w, layer, *, n_cols, transposed=False, w_row_gap=(0, 0), bm=512, bn=1024,
               res=None, act=None, out_dtype=F32, side_cast=None, norm_gain=None, name="ws_matmul"):
    mp, ms = lhs_p[0].shape[0], lhs_s[0].shape[0]
    k_total = w.shape[2] if transposed else w.shape[1]
    widths = [a.shape[1] for a in lhs_p]
    assert sum(widths) == k_total and mp % bm == 0 and n_cols % bn == 0
    n_p, n_j = mp // bm, n_cols // bn
    bounds, k0 = [], 0
    for wd in widths:
        bounds.append((k0, k0 + wd))
        k0 += wd

    def row_blk(jj, i):
        return jnp.where(jj == 0, 0, jnp.minimum(i, n_p - 1))

    def lhs_row_blk(jj, i):
        return jnp.where((jj == 0) | (i == n_p), 0, i)

    def col_use(jj):
        return jnp.maximum(jj - 1, 0)

    def chunk(jj, i):
        return jnp.where(jj < n_j, jnp.minimum(i, n_p - 1), n_p - 1)

    def col_load(jj):
        return jnp.minimum(jj, n_j - 1)

    if transposed:
        chunk_rows = bn // n_p
        gap_row, gap_rows = w_row_gap
        assert gap_row % bn == 0 and gap_rows % SUBLANES == 0

        def w_row(jj, i):
            row = (col_load(jj) * n_p + chunk(jj, i)) * chunk_rows
            return pl.multiple_of(row + jnp.where(row >= gap_row, gap_rows, 0), SUBLANES)

        w_spec = pl.BlockSpec((pl.Element(1), pl.Element(chunk_rows), pl.Element(k_total)),
                              lambda jj, i: (layer, w_row(jj, i), 0))
        scratch = pltpu.VMEM((2, bn, k_total), BF16)
    else:
        chunk_rows = k_total // n_p
        w_spec = pl.BlockSpec((None, chunk_rows, bn), lambda jj, i: (layer, chunk(jj, i), col_load(jj)))
        scratch = pltpu.VMEM((2, k_total, bn), BF16)
    assert chunk_rows * n_p == (bn if transposed else k_total) and chunk_rows % 16 == 0
    specs_p = [pl.BlockSpec((bm, wd), lambda jj, i: (lhs_row_blk(jj, i), 0)) for wd in widths]
    specs_s = [pl.BlockSpec((ms, wd), lambda jj, i: (0, 0)) for wd in widths]
    io_p = pl.BlockSpec((bm, bn), lambda jj, i: (row_blk(jj, i), col_use(jj)))
    io_s = pl.BlockSpec((ms, bn), lambda jj, i: (0, col_use(jj)))
    in_specs = specs_p + specs_s + [w_spec]
    args = list(lhs_p) + list(lhs_s) + [w]
    if res is not None:
        in_specs += [io_p, io_s]
        args += list(res)
    out_specs = [io_p, io_s]
    out_shape = [jax.ShapeDtypeStruct((mp, n_cols), out_dtype), jax.ShapeDtypeStruct((ms, n_cols), out_dtype)]
    side_bytes = 0
    if side_cast is not None:
        side, side_layer = side_cast
        side_rows, side_cols = side.shape[1], side.shape[2]
        rb = side_rows // (n_j * n_p)
        assert rb * n_j * n_p == side_rows and rb % 16 == 0

        def side_blk(jj, i):
            return jnp.where(jj == 0, 0, (jj - 1) * n_p + jnp.minimum(i, n_p - 1))

        in_specs.append(pl.BlockSpec((None, rb, side_cols), lambda jj, i: (side_layer, side_blk(jj, i), 0)))
        args.append(side)
        out_specs.append(pl.BlockSpec((rb, side_cols), lambda jj, i: (side_blk(jj, i), 0)))
        out_shape.append(jax.ShapeDtypeStruct((side_rows, side_cols), BF16))
        side_bytes = _nbytes((rb, side_cols), F32) + _nbytes((rb, side_cols), BF16)
    if norm_gain is not None:
        in_specs.append(pl.BlockSpec((1, bn), lambda jj, i: (0, col_use(jj))))
        args.append(norm_gain)
        out_specs += [io_p, io_s,
                      pl.BlockSpec((bm, LANES), lambda jj, i: (row_blk(jj, i), col_use(jj))),
                      pl.BlockSpec((ms, LANES), lambda jj, i: (0, col_use(jj)))]
        out_shape += [jax.ShapeDtypeStruct((mp, n_cols), BF16), jax.ShapeDtypeStruct((ms, n_cols), BF16),
                      jax.ShapeDtypeStruct((mp, n_j * LANES), F32), jax.ShapeDtypeStruct((ms, n_j * LANES), F32)]
        side_bytes += _nbytes((bm + ms, bn), BF16) + _nbytes((bm + ms, LANES), F32)
    body = functools.partial(_ws_matmul_body, k_bounds=tuple(bounds), has_res=res is not None,
                             has_side=side_cast is not None, has_gain=norm_gain is not None, act=act,
                             n_prompt_blocks=n_p, chunk_rows=chunk_rows, transposed=transposed)
    n_io = 2 if res is not None else 1
    pipelined = (_nbytes((bm + ms, k_total), BF16) + _nbytes((k_total // n_p, bn), F32)
                 + n_io * _nbytes((bm + ms, bn), F32) + side_bytes)
    limit = min(VMEM_CAP, 2 * pipelined + _nbytes((2, k_total, bn), BF16) + 2 * _nbytes((bm, bn), F32))
    return pl.pallas_call(
        body, grid=(n_j + 1, n_p + 1), in_specs=in_specs, out_specs=out_specs, out_shape=out_shape,
        scratch_shapes=[scratch],
        compiler_params=pltpu.CompilerParams(
            dimension_semantics=("arbitrary", "arbitrary"), vmem_limit_bytes=int(limit)),
        name=name)(*args)


def _linear_scan_rows(a, b, carry):
    t, lanes = b.shape
    g = t // SUBLANES
    b3 = b.reshape(g, SUBLANES, lanes)
    a3 = None if a is None else a.reshape(g, SUBLANES, lanes)
    sub = lax.broadcasted_iota(jnp.int32, b3.shape, 1)
    s = 1
    while s < SUBLANES:
        keep = sub >= s
        b_prev = jnp.where(keep, pltpu.roll(b3, s, 1), 0.0)
        if a3 is None:
            b3 = b3 + b_prev
        else:
            b3 = a3 * b_prev + b3
            a3 = a3 * jnp.where(keep, pltpu.roll(a3, s, 1), 1.0)
        s *= 2
    outs = []
    for j in range(g):
        h = b3[j] + carry if a3 is None else b3[j] + a3[j] * carry
        outs.append(h)
        carry = h[SUBLANES - 1:, :]
    return jnp.concatenate(outs, axis=0), carry


def _softplus(x):
    return jnp.maximum(x, 0.0) + jnp.log1p(jnp.exp(-jnp.abs(x)))


def _gla_body(q_ref, k_ref, v_ref, g_ref, alr_ref, wa_ref, ba_ref, gn_ref, s0_ref,
              o_ref, s_out_ref, st_ref, *, chunk, n_chunks, n_pairs):
    n = pl.program_id(2)

    @pl.when(n == 0)
    def _():
        for p in range(n_pairs):
            st_ref[p] = s0_ref[0, p].T

    c = chunk
    qw, vw = 2 * DK_A, 2 * DV_A
    lane = lax.broadcasted_iota(jnp.int32, (c, qw), 1)
    head0 = lane < DK_A
    row2 = lax.broadcasted_iota(jnp.int32, (2 * c, 2 * c), 0)
    col2 = lax.broadcasted_iota(jnp.int32, (2 * c, 2 * c), 1)
    causal = ((row2 >= c) == (col2 >= c)) & (row2 >= col2)
    nt = (((1,), (1,)), ((), ()))
    tn = (((0,), (0,)), ((), ()))

    def stack_heads(z):
        return jnp.concatenate([jnp.where(head0, z, 0.0), jnp.where(head0, 0.0, z)], axis=0).astype(BF16)

    for ci in range(n_chunks):
        rows = pl.ds(ci * c, c)
        z = jnp.dot(alr_ref[rows, :].astype(BF16), wa_ref[...], preferred_element_type=F32) + ba_ref[...]
        log_alpha = -(jnp.maximum(-z, 0.0) + jnp.log(1.0 + jnp.exp(-jnp.abs(z)))) / GLA_TAU
        b, b_end = _linear_scan_rows(None, log_alpha, jnp.zeros((1, log_alpha.shape[1]), F32))
        q = q_ref[rows, :] * (DK_A ** -0.5)
        k = k_ref[rows, :]
        q_dec = q * jnp.exp(b)
        k_inv = k * jnp.exp(-b)
        k_end = k * jnp.exp(b_end - b)
        decay = jnp.exp(b_end)
        for p in range(n_pairs):
            ql = slice(p * qw, (p + 1) * qw)
            q_t, k_t, k_e = stack_heads(q_dec[:, ql]), stack_heads(k_inv[:, ql]), stack_heads(k_end[:, ql])
            v0 = v_ref[rows, pl.ds(p * vw, DV_A)]
            v1 = v_ref[rows, pl.ds(p * vw + DV_A, DV_A)]
            v_s = jnp.concatenate([v0, v1], axis=0).astype(BF16)
            att = lax.dot_general(q_t, k_t, nt, preferred_element_type=F32)
            att = jnp.where(causal, att, 0.0).astype(BF16)
            st = st_ref[p]
            o = (jnp.dot(att, v_s, preferred_element_type=F32)
                 + lax.dot_general(q_t, st.astype(BF16), nt, preferred_element_type=F32))
            st_ref[p] = decay[:, ql] * st + lax.dot_general(v_s, k_e, tn, preferred_element_type=F32)
            ms = jnp.mean(jnp.square(o), axis=-1, keepdims=True)
            o = (o * lax.rsqrt(ms + EPS) * gn_ref[...])
            g0 = g_ref[rows, pl.ds(p * vw, DV_A)]
            g1 = g_ref[rows, pl.ds(p * vw + DV_A, DV_A)]
            o_ref[rows, pl.ds(p * vw, DV_A)] = (o[:c] * (g0 * jax.nn.sigmoid(g0))).astype(o_ref.dtype)
            o_ref[rows, pl.ds(p * vw + DV_A, DV_A)] = (o[c:] * (g1 * jax.nn.sigmoid(g1))).astype(o_ref.dtype)

    @pl.when(n == pl.num_programs(2) - 1)
    def _():
        for p in range(n_pairs):
            s_out_ref[0, p] = st_ref[p].T


def _gla(hproj, alr, wa, ba, gn, s0, *, n_seq, seq_len, col_q, col_k, col_v, col_g):
    t_total = hproj.shape[0]
    h_a = s0.shape[1]
    pairs = h_a // 2
    pps = GLA_PAIRS_PER_STEP
    assert pairs % pps == 0
    chunk = min(CHUNK, seq_len)
    n_chunks = min(8, seq_len // chunk)
    tb = chunk * n_chunks
    nb = seq_len // tb
    qw, vw = 2 * DK_A * pps, 2 * DV_A * pps
    assert col_q % qw == 0 and col_k % qw == 0 and col_v % vw == 0 and col_g % vw == 0
    cq, ck, cv, cg = col_q // qw, col_k // qw, col_v // vw, col_g // vw
    row = lambda s, n: s * nb + n
    in_specs = [
        pl.BlockSpec((tb, qw), lambda s, p, n: (row(s, n), cq + p)),
        pl.BlockSpec((tb, qw), lambda s, p, n: (row(s, n), ck + p)),
        pl.BlockSpec((tb, vw), lambda s, p, n: (row(s, n), cv + p)),
        pl.BlockSpec((tb, vw), lambda s, p, n: (row(s, n), cg + p)),
        pl.BlockSpec((tb, LANES), lambda s, p, n: (row(s, n), 0)),
        pl.BlockSpec((LANES, qw), lambda s, p, n: (0, p)),
        pl.BlockSpec((1, qw), lambda s, p, n: (0, p)),
        pl.BlockSpec((1, DV_A), lambda s, p, n: (0, 0)),
        pl.BlockSpec((1, pps, 2 * DK_A, DV_A), lambda s, p, n: (s, p, 0, 0)),
    ]
    out_specs = [
        pl.BlockSpec((tb, vw), lambda s, p, n: (row(s, n), p)),
        pl.BlockSpec((1, pps, 2 * DK_A, DV_A), lambda s, p, n: (s, p, 0, 0)),
    ]
    o_a, s_new = pl.pallas_call(
        functools.partial(_gla_body, chunk=chunk, n_chunks=n_chunks, n_pairs=pps),
        grid=(n_seq, pairs // pps, nb), in_specs=in_specs, out_specs=out_specs,
        out_shape=[jax.ShapeDtypeStruct((t_total, h_a * DV_A), BF16),
                   jax.ShapeDtypeStruct((n_seq, pairs, 2 * DK_A, DV_A), F32)],
        scratch_shapes=[pltpu.VMEM((pps, DV_A, 2 * DK_A), F32)],
        compiler_params=pltpu.CompilerParams(
            dimension_semantics=("parallel", "parallel", "arbitrary"),
            vmem_limit_bytes=_vmem_limit(_nbytes((tb, 2 * qw + 2 * vw + LANES), F32), _nbytes((tb, vw), BF16))),
        name="gla")(hproj, hproj, hproj, hproj, alr, wa, ba, gn, s0.reshape(n_seq, pairs, 2 * DK_A, DV_A))
    return o_a, s_new.reshape(n_seq, h_a, DK_A, DV_A)


def _gmlp_body(u_ref, v_ref, w_ref, b_ref, o_ref, *vg_refs, c, dh):
    tril = (lax.broadcasted_iota(jnp.int32, (c, c), 0) >= lax.broadcasted_iota(jnp.int32, (c, c), 1))
    w_tril = [jnp.where(tril, w_ref[h, :c, :c], 0.0).astype(BF16) for h in range(w_ref.shape[0])]
    for ci in range(u_ref.shape[0] // c):
        rows = pl.ds(ci * c, c)
        vg = jax.nn.gelu(v_ref[rows, :])
        if vg_refs:
            vg_refs[0][rows, :] = vg
        ug = jax.nn.gelu(u_ref[rows, :])
        for h, w in enumerate(w_tril):
            cols = pl.ds(h * dh, dh)
            mixed = (jnp.dot(w, vg[:, h * dh:(h + 1) * dh].astype(BF16), preferred_element_type=F32)
                     + b_ref[:c, cols])
            o_ref[rows, cols] = (ug[:, h * dh:(h + 1) * dh] * mixed).astype(o_ref.dtype)


def _gmlp(hproj, w_sp, b_full, *, seq_len, col_u, col_v, want_v):
    t_total = hproj.shape[0]
    n_h, gc, _ = w_sp.shape
    w_b = b_full.shape[1]
    dh = w_b // n_h
    c = min(gc, seq_len)
    tb = c * min(4, seq_len // c)
    blk = lambda col: pl.BlockSpec((tb, w_b), lambda r: (r, col // w_b))
    out_specs = [pl.BlockSpec((tb, w_b), lambda r: (r, 0))]
    out_shape = [jax.ShapeDtypeStruct((t_total, w_b), BF16)]
    if want_v:
        out_specs.append(pl.BlockSpec((tb, w_b), lambda r: (r, 0)))
        out_shape.append(jax.ShapeDtypeStruct((t_total, w_b), F32))
    outs = pl.pallas_call(
        functools.partial(_gmlp_body, c=c, dh=dh), grid=(t_total // tb,),
        in_specs=[blk(col_u), blk(col_v),
                  pl.BlockSpec((n_h, gc, gc), lambda r: (0, 0, 0)),
                  pl.BlockSpec((gc, w_b), lambda r: (0, 0))],
        out_specs=out_specs, out_shape=out_shape,
        compiler_params=pltpu.CompilerParams(
            dimension_semantics=("parallel",),
            vmem_limit_bytes=_vmem_limit(4 * _nbytes((tb, w_b), F32), _nbytes((n_h, gc, gc), F32))),
        name="gmlp")(hproj, hproj, w_sp, b_full)
    return (outs[0], outs[1]) if want_v else (outs[0], None)


def _lru_body(x_ref, gate_ref, cw_ref, cb_ref, wr_ref, br_ref, wi_ref, bi_ref, lam_ref, h0_ref, tail0_ref,
              o_ref, h_ref, tail_ref, *, dh):
    n = pl.program_id(1)

    @pl.when(n == 0)
    def _():
        h_ref[...] = h0_ref[...]
        tail_ref[...] = tail0_ref[...]

    tb = x_ref.shape[0]
    row8 = lax.broadcasted_iota(jnp.int32, (SUBLANES, dh), 0)
    for hb in range(x_ref.shape[1] // dh):
        cols = pl.ds(hb * dh, dh)
        x = x_ref[:, cols]
        tail = tail_ref[0, :, cols]

        def delayed(j):
            rolled = pltpu.roll(x, j, 0)
            head = jnp.where(row8 >= j, rolled[:SUBLANES], pltpu.roll(tail, j, 0))
            return jnp.concatenate([head, rolled[SUBLANES:]], axis=0)

        y = cb_ref[:, cols] + delayed(CONV_W - 1) * cw_ref[0:1, cols]
        for j in range(1, CONV_W - 1):
            y = y + delayed(CONV_W - 1 - j) * cw_ref[j:j + 1, cols]
        y = y + x * cw_ref[CONV_W - 1:CONV_W, cols]
        tail_ref[0, :, cols] = x[tb - SUBLANES:]

        yb = y.astype(BF16)
        r = jax.nn.sigmoid(jnp.dot(yb, wr_ref[hb].astype(BF16), preferred_element_type=F32) + br_ref[:, cols])
        ig = jax.nn.sigmoid(jnp.dot(yb, wi_ref[hb].astype(BF16), preferred_element_type=F32) + bi_ref[:, cols])
        log_a = -LRU_C * r * _softplus(-lam_ref[:, cols])
        a = jnp.exp(log_a)
        bt = jnp.sqrt(-jnp.tanh(log_a) * (a * a + 1.0)) * (ig * y)
        hseq, h_last = _linear_scan_rows(a, bt, h_ref[0, :, cols])
        h_ref[0, :, cols] = h_last
        o_ref[:, cols] = (hseq * jax.nn.gelu(gate_ref[:, cols])).astype(o_ref.dtype)


def _lru(hproj, cw, cb, wr, br, wi, bi, lam, h0, tail0, *, n_seq, seq_len, col_x, col_gate):
    t_total = hproj.shape[0]
    w_c = cw.shape[1]
    n_h, dh, _ = wr.shape
    tb = min(seq_len, 256)
    nb = seq_len // tb
    vec = lambda rows: pl.BlockSpec((rows, w_c), lambda s, n: (0, 0))
    wspec = pl.BlockSpec((n_h, dh, dh), lambda s, n: (0, 0, 0))
    hspec = pl.BlockSpec((1, 1, w_c), lambda s, n: (s, 0, 0))
    tspec = pl.BlockSpec((1, SUBLANES, w_c), lambda s, n: (s, 0, 0))
    o_c, h_new, tail = pl.pallas_call(
        functools.partial(_lru_body, dh=dh), grid=(n_seq, nb),
        in_specs=[pl.BlockSpec((tb, w_c), lambda s, n: (s * nb + n, col_x // w_c)),
                  pl.BlockSpec((tb, w_c), lambda s, n: (s * nb + n, col_gate // w_c)),
                  vec(CONV_W), vec(1), wspec, vec(1), wspec, vec(1), vec(1), hspec, tspec],
        out_specs=[pl.BlockSpec((tb, w_c), lambda s, n: (s * nb + n, 0)), hspec, tspec],
        out_shape=[jax.ShapeDtypeStruct((t_total, w_c), BF16),
                   jax.ShapeDtypeStruct((n_seq, 1, w_c), F32),
                   jax.ShapeDtypeStruct((n_seq, SUBLANES, w_c), F32)],
        compiler_params=pltpu.CompilerParams(
            dimension_semantics=("parallel", "arbitrary"),
            vmem_limit_bytes=_vmem_limit(4 * _nbytes((tb, w_c), F32))),
        name="conv_rglru")(hproj, hproj, cw, cb.reshape(1, w_c), wr, br.reshape(1, w_c), wi,
                           bi.reshape(1, w_c), lam.reshape(1, w_c), h0.reshape(n_seq, 1, w_c), tail0)
    return o_c, h_new.reshape(n_seq, w_c), tail


def _mixers(hproj, alr, s_gla, s_conv, s_lru, w, *, n_seq, seq_len, want_v):
    w_a = s_gla.shape[1] * DV_A
    qk_a = s_gla.shape[1] * DK_A
    w_b = w_c = s_lru.shape[1]
    col_u = 2 * qk_a + 2 * w_a
    o_a, s_gla_new = _gla(hproj, alr, w["w_alpha2"], w["b_alpha"], w["g_onorm"], s_gla, n_seq=n_seq,
                          seq_len=seq_len, col_q=0, col_k=qk_a, col_v=2 * qk_a, col_g=2 * qk_a + w_a)
    o_b, vg = _gmlp(hproj, w["w_spatial"], w["b_spatial"], seq_len=seq_len, col_u=col_u, col_v=col_u + w_b,
                    want_v=want_v)
    tail0 = jnp.pad(s_conv, ((0, 0), (SUBLANES - (CONV_W - 1), 0), (0, 0)))
    o_c, s_lru_new, tail = _lru(hproj, w["conv_w"], w["conv_b"], w["w_rgate"], w["b_rgate"], w["w_igate"],
                                w["b_igate"], w["lru_lambda"], s_lru, tail0, n_seq=n_seq, seq_len=seq_len,
                                col_x=col_u + 2 * w_b, col_gate=col_u + 2 * w_b + w_c)
    return [o_a, o_b, o_c], (s_gla_new, tail[:, SUBLANES - (CONV_W - 1):], s_lru_new, vg)


def _small_layer_weights(l, w_in_t, w_alpha2, b_alpha, g_onorm, w_spatial, b_spatial, conv_w, conv_b,
                         w_rgate, b_rgate, w_igate, b_igate, lru_lambda):
    d = w_in_t.shape[2]
    w_a, w_b = d // 2, d // 4
    qk_a = (w_a // DV_A) * DK_A
    c_alr = 2 * qk_a + 2 * w_a
    w_alr = jnp.pad(w_in_t[l, c_alr:c_alr + GLA_RANK, :], ((0, LANES - GLA_RANK), (0, 0)))
    w_alpha2_p = jnp.pad(w_alpha2[l], ((0, LANES - GLA_RANK), (0, 0))).astype(BF16)
    b_full = jnp.repeat(b_spatial[l].T, w_b // H_B, axis=1)
    return dict(
        w_alr=w_alr, w_alpha2=w_alpha2_p, b_alpha=b_alpha[l].reshape(1, qk_a),
        g_onorm=g_onorm[l].reshape(1, DV_A), w_spatial=w_spatial[l], b_spatial=b_full,
        conv_w=conv_w[l], conv_b=conv_b[l], w_rgate=w_rgate[l], b_rgate=b_rgate[l],
        w_igate=w_igate[l], b_igate=b_igate[l], lru_lambda=lru_lambda[l])


def kernel(x_prompt, x_sample, state_gla, state_conv, state_lru, ln1, w_in, w_alpha2, b_alpha, g_onorm, w_spatial, b_spatial, conv_w, conv_b, w_rgate, b_rgate, w_igate, b_igate, lru_lambda, w_out, ln2, w_up, w_down, ln_final):
    bp, tp, d = x_prompt.shape
    bs, ts, _ = x_sample.shape
    depth = w_in.shape[0]
    w_a, w_b, w_c = d // 2, d // 4, d // 4
    h_a = w_a // DV_A
    qk_a = h_a * DK_A
    n_a = 2 * qk_a + 2 * w_a
    n_b = 2 * w_b + 2 * w_c
    zero_gla = jnp.zeros((bp, h_a, DK_A, DV_A), F32)
    zero_conv = jnp.zeros((bp, CONV_W - 1, w_c), F32)
    zero_lru = jnp.zeros((bp, w_c), F32)
    xp = x_prompt.reshape(bp * tp, d)
    xs = x_sample.reshape(bs * ts, d)
    w_in_t = jnp.swapaxes(w_in, 1, 2)
    mm = _ws_matmul
    st_p, st_s = [], []
    for l in range(depth):
        w = _small_layer_weights(l, w_in_t, w_alpha2, b_alpha, g_onorm, w_spatial, b_spatial, conv_w, conv_b,
                                 w_rgate, b_rgate, w_igate, b_igate, lru_lambda)
        xn_p, alr_p = _rmsnorm(xp, ln1[l], BF16, w_proj=w["w_alr"])
        xn_s, alr_s = _rmsnorm(xs, ln1[l], BF16, w_proj=w["w_alr"])
        hp_p, hp_s = mm([xn_p], [xn_s], w_in_t, l, n_cols=n_a + n_b, transposed=True,
                        w_row_gap=(n_a, GLA_RANK), bm=1024, name="in_proj")
        mix_p, new_p = _mixers(hp_p, alr_p, zero_gla, zero_conv, zero_lru, w, n_seq=bp, seq_len=tp,
                               want_v=False)
        mix_s, new_s = _mixers(hp_s, alr_s, state_gla[l], state_conv[l], state_lru[l], w, n_seq=bs,
                               seq_len=ts, want_v=True)
        st_p.append(new_p)
        st_s.append(new_s)
        xp, xs, xg_p, xg_s, ssq_p, ssq_s = mm(mix_p, mix_s, w_out, l, n_cols=d, res=(xp, xs),
                                              norm_gain=ln2[l].reshape(1, d), name="out_proj")
        hid_p, hid_s, w_down_bf = mm([xg_p], [xg_s], w_up, l, n_cols=w_up.shape[2], act="relu2",
                                     out_dtype=BF16, side_cast=(w_down, l), bm=1024, name="mlp_up")
        scale = (_inv_mean_square(ssq_p, d), _inv_mean_square(ssq_s, d))
        xp, xs = _ktiled_matmul(hid_p, hid_s, w_down_bf, (xp, xs), scale, bm=1024, bn=1024, bk=2048,
                                name="mlp_down")
    y_prompt = _rmsnorm(xp, ln_final, F32).reshape(bp, tp, d)
    y_sample = _rmsnorm(xs, ln_final, F32).reshape(bs, ts, d)
    stack = lambda sts, i: jnp.stack([s[i] for s in sts])
    return (y_prompt, y_sample, stack(st_p, 0), stack(st_p, 1), stack(st_p, 2),
            stack(st_s, 0), stack(st_s, 1), stack(st_s, 2),
            jnp.stack([s[3].reshape(bs, ts, w_b) for s in st_s]))
```

```python
import functools

import jax
import jax.numpy as jnp
from jax import lax
from jax.experimental import pallas as pl
from jax.experimental.pallas import tpu as pltpu

F32 = jnp.float32
BF16 = jnp.bfloat16

DV_A = 128
DK_A = 64
GLA_RANK = 16
GLA_TAU = 16.0
CHUNK = 64
GMLP_CHUNK = 128
H_B = 8
H_C = 8
CONV_W = 4
LRU_C = 8.0
EPS = 1e-6

LANES = 128
SUBLANES = 8
V7X_VMEM_BYTES = 64 * 1024 * 1024
VMEM_CAP = V7X_VMEM_BYTES - 6 * 1024 * 1024

GLA_PAIRS_PER_STEP = 8


def _vmem_limit(*block_bytes):
    need = 4 * sum(block_bytes)
    return int(min(max(need, 16 * 1024 * 1024), VMEM_CAP))


def _nbytes(shape, dtype):
    n = 1
    for s in shape:
        n *= s
    return n * jnp.dtype(dtype).itemsize


def _rmsnorm_body(x_ref, g_ref, o_ref):
    x = x_ref[...]
    ms = jnp.mean(jnp.square(x), axis=-1, keepdims=True)
    o_ref[...] = (x * lax.rsqrt(ms + EPS) * g_ref[...]).astype(o_ref.dtype)


def _rmsnorm_proj_body(x_ref, g_ref, w_ref, o_ref, p_ref):
    x = x_ref[...]
    ms = jnp.mean(jnp.square(x), axis=-1, keepdims=True)
    y = (x * lax.rsqrt(ms + EPS) * g_ref[...]).astype(BF16)
    o_ref[...] = y
    p_ref[...] = lax.dot_general(y, w_ref[...].astype(BF16), (((1,), (1,)), ((), ())),
                                 preferred_element_type=F32)


def _rmsnorm(x, g, out_dtype, w_proj=None):
    t, d = x.shape
    tb = min(t, 512)
    grid = (t // tb,)
    x_spec = pl.BlockSpec((tb, d), lambda i: (i, 0))
    g_spec = pl.BlockSpec((1, d), lambda i: (0, 0))
    params = pltpu.CompilerParams(
        dimension_semantics=("parallel",),
        vmem_limit_bytes=_vmem_limit(_nbytes((tb, d), F32), _nbytes((tb, d), out_dtype)))
    if w_proj is None:
        return pl.pallas_call(
            _rmsnorm_body, grid=grid, in_specs=[x_spec, g_spec], out_specs=x_spec,
            out_shape=jax.ShapeDtypeStruct((t, d), out_dtype), compiler_params=params,
            name="rmsnorm")(x, g.reshape(1, d))
    n = w_proj.shape[0]
    return pl.pallas_call(
        _rmsnorm_proj_body, grid=grid,
        in_specs=[x_spec, g_spec, pl.BlockSpec((n, d), lambda i: (0, 0))],
        out_specs=[x_spec, pl.BlockSpec((tb, n), lambda i: (i, 0))],
        out_shape=[jax.ShapeDtypeStruct((t, d), BF16), jax.ShapeDtypeStruct((t, n), F32)],
        compiler_params=params, name="rmsnorm_proj")(x, g.reshape(1, d), w_proj)


def _ktiled_matmul_body(a_p, a_s, w_ref, res_p, res_s, ssq_p, ssq_s, o_p, o_s, *, n_prompt_blocks, width):
    i = pl.program_id(0)

    def run(a_ref, res_ref, ssq_ref, o_ref):
        @pl.when(pl.program_id(2) == 0)
        def _():
            o_ref[...] = res_ref[...]

        ss = ssq_ref[:, 0:1]
        for j in range(1, ssq_ref.shape[1] // LANES):
            ss = ss + ssq_ref[:, j * LANES:j * LANES + 1]
        acc = jnp.dot(a_ref[...], w_ref[...], preferred_element_type=F32)
        o_ref[...] += acc * (1.0 / (ss / width + EPS))

    @pl.when(i < n_prompt_blocks)
    def _():
        run(a_p, res_p, ssq_p, o_p)

    @pl.when(i == n_prompt_blocks)
    def _():
        run(a_s, res_s, ssq_s, o_s)


def _ktiled_matmul(a_p, a_s, w, res, ssq, width, *, bm, bn, bk, name):
    mp, ms = a_p.shape[0], a_s.shape[0]
    k_total, n_cols = w.shape
    assert mp % bm == 0 and n_cols % bn == 0 and k_total % bk == 0
    n_p, n_j, nk = mp // bm, n_cols // bn, k_total // bk

    def p_row(i):
        return jnp.minimum(i, n_p - 1)

    def p_col(i, j):
        return jnp.where(i < n_p, j, n_j - 1)

    def s_col(i, j):
        return jnp.where(i == n_p, j, 0)

    io_p = pl.BlockSpec((bm, bn), lambda i, j, k: (p_row(i), p_col(i, j)))
    io_s = pl.BlockSpec((ms, bn), lambda i, j, k: (0, s_col(i, j)))
    in_specs = [
        pl.BlockSpec((bm, bk), lambda i, j, k: (p_row(i), jnp.where(i < n_p, k, nk - 1))),
        pl.BlockSpec((ms, bk), lambda i, j, k: (0, jnp.where(i == n_p, k, 0))),
        pl.BlockSpec((bk, bn), lambda i, j, k: (k, j)),
        io_p, io_s,
        pl.BlockSpec((bm, ssq[0].shape[1]), lambda i, j, k: (p_row(i), 0)),
        pl.BlockSpec((ms, ssq[1].shape[1]), lambda i, j, k: (0, 0)),
    ]
    limit = _vmem_limit(_nbytes((bm + ms, bk), BF16), _nbytes((bk, bn), BF16), 2 * _nbytes((bm + ms, bn), F32),
                        _nbytes((bm + ms, ssq[0].shape[1]), F32))
    return pl.pallas_call(
        functools.partial(_ktiled_matmul_body, n_prompt_blocks=n_p, width=width),
        grid=(n_p + 1, n_j, nk), in_specs=in_specs, out_specs=[io_p, io_s],
        out_shape=[jax.ShapeDtypeStruct((mp, n_cols), F32), jax.ShapeDtypeStruct((ms, n_cols), F32)],
        compiler_params=pltpu.CompilerParams(
            dimension_semantics=("arbitrary", "arbitrary", "arbitrary"), vmem_limit_bytes=limit),
        name=name)(a_p, a_s, w, res[0], res[1], ssq[0], ssq[1])


def _ws_matmul_body(*refs, k_bounds, has_res, has_side, has_gain, act, n_prompt_blocks, chunk_rows,
                    transposed):
    refs = list(refs)
    n_lhs = len(k_bounds)
    take = lambda n: [refs.pop(0) for _ in range(n)]
    lhs_p, lhs_s = take(n_lhs), take(n_lhs)
    (wc_ref,) = take(1)
    res_p, res_s = take(2) if has_res else (None, None)
    (side_in,) = take(1) if has_side else (None,)
    (gain_ref,) = take(1) if has_gain else (None,)
    o_p, o_s = take(2)
    (side_out,) = take(1) if has_side else (None,)
    gained_p, gained_s, ssq_p, ssq_s = take(4) if has_gain else (None,) * 4
    (w_scr,) = take(1)
    jj, i = pl.program_id(0), pl.program_id(1)
    slot_load = jj % 2
    slot_use = 1 - slot_load

    def stage():
        r0 = pl.multiple_of(jnp.minimum(i, n_prompt_blocks - 1) * chunk_rows, chunk_rows)
        chunk = wc_ref[0] if len(wc_ref.shape) == 3 else wc_ref[...]
        w_scr[slot_load, pl.ds(r0, chunk_rows), :] = chunk.astype(BF16)

    def run(lhs_refs, res_ref, o_ref, gained_ref, ssq_ref):
        stage()
        acc = None
        for a_ref, (k0, k1) in zip(lhs_refs, k_bounds):
            if transposed:
                part = lax.dot_general(a_ref[...], w_scr[slot_use, :, k0:k1], (((1,), (1,)), ((), ())),
                                       preferred_element_type=F32)
            else:
                part = jnp.dot(a_ref[...], w_scr[slot_use, k0:k1, :], preferred_element_type=F32)
            acc = part if acc is None else acc + part
        if has_res:
            acc = res_ref[...] + acc
        if act == "relu2":
            acc = jnp.square(jnp.maximum(acc, 0.0))
        o_ref[...] = acc.astype(o_ref.dtype)
        if has_gain:
            gained_ref[...] = (acc * gain_ref[...]).astype(gained_ref.dtype)
            ssq_ref[...] = jnp.broadcast_to(jnp.sum(acc * acc, axis=1, keepdims=True), ssq_ref.shape)

    @pl.when(jj == 0)
    def _():
        stage()

    @pl.when((jj > 0) & (i < n_prompt_blocks))
    def _():
        if has_side:
            side_out[...] = side_in[...].astype(side_out.dtype)
        run(lhs_p, res_p, o_p, gained_p, ssq_p)

    @pl.when((jj > 0) & (i == n_prompt_blocks))
    def _():
        run(lhs_s, res_s, o_s, gained_s, ssq_s)


def _ws_matmul(lhs_p, lhs_s, w, layer, *, n_cols, transposed=False, w_row_gap=(0, 0), bm=512, bn=1024,
               res=None, act=None, out_dtype=F32, side_cast=None, norm_gain=None, name="ws_matmul"):
    mp, ms = lhs_p[0].shape[0], lhs_s[0].shape[0]
    k_total = w.shape[2] if transposed else w.shape[1]
    widths = [a.shape[1] for a in lhs_p]
    assert sum(widths) == k_total and mp % bm == 0 and n_cols % bn == 0
    n_p, n_j = mp // bm, n_cols // bn
    bounds, k0 = [], 0
    for wd in widths:
        bounds.append((k0, k0 + wd))
        k0 += wd

    def row_blk(jj, i):
        return jnp.where(jj == 0, 0, jnp.minimum(i, n_p - 1))

    def lhs_row_blk(jj, i):
        return jnp.where((jj == 0) | (i == n_p), 0, i)

    def col_use(jj):
        return jnp.maximum(jj - 1, 0)

    def chunk(jj, i):
        return jnp.where(jj < n_j, jnp.minimum(i, n_p - 1), n_p - 1)

    def col_load(jj):
        return jnp.minimum(jj, n_j - 1)

    if transposed:
        chunk_rows = bn // n_p
        gap_row, gap_rows = w_row_gap
        assert gap_row % bn == 0 and gap_rows % SUBLANES == 0

        def w_row(jj, i):
            row = (col_load(jj) * n_p + chunk(jj, i)) * chunk_rows
            return pl.multiple_of(row + jnp.where(row >= gap_row, gap_rows, 0), SUBLANES)

        w_spec = pl.BlockSpec((pl.Element(1), pl.Element(chunk_rows), pl.Element(k_total)),
                              lambda jj, i: (layer, w_row(jj, i), 0))
        scratch = pltpu.VMEM((2, bn, k_total), BF16)
    else:
        chunk_rows = k_total // n_p
        w_spec = pl.BlockSpec((None, chunk_rows, bn), lambda jj, i: (layer, chunk(jj, i), col_load(jj)))
        scratch = pltpu.VMEM((2, k_total, bn), BF16)
    assert chunk_rows * n_p == (bn if transposed else k_total) and chunk_rows % 16 == 0
    specs_p = [pl.BlockSpec((bm, wd), lambda jj, i: (lhs_row_blk(jj, i), 0)) for wd in widths]
    specs_s = [pl.BlockSpec((ms, wd), lambda jj, i: (0, 0)) for wd in widths]
    io_p = pl.BlockSpec((bm, bn), lambda jj, i: (row_blk(jj, i), col_use(jj)))
    io_s = pl.BlockSpec((ms, bn), lambda jj, i: (0, col_use(jj)))
    in_specs = specs_p + specs_s + [w_spec]
    args = list(lhs_p) + list(lhs_s) + [w]
    if res is not None:
        in_specs += [io_p, io_s]
        args += list(res)
    out_specs = [io_p, io_s]
    out_shape = [jax.ShapeDtypeStruct((mp, n_cols), out_dtype), jax.ShapeDtypeStruct((ms, n_cols), out_dtype)]
    side_bytes = 0
    if side_cast is not None:
        side, side_layer = side_cast
        side_rows, side_cols = side.shape[1], side.shape[2]
        rb = side_rows // (n_j * n_p)
        assert rb * n_j * n_p == side_rows and rb % 16 == 0

        def side_blk(jj, i):
            return jnp.where(jj == 0, 0, (jj - 1) * n_p + jnp.minimum(i, n_p - 1))

        in_specs.append(pl.BlockSpec((None, rb, side_cols), lambda jj, i: (side_layer, side_blk(jj, i), 0)))
        args.append(side)
        out_specs.append(pl.BlockSpec((rb, side_cols), lambda jj, i: (side_blk(jj, i), 0)))
        out_shape.append(jax.ShapeDtypeStruct((side_rows, side_cols), BF16))
        side_bytes = _nbytes((rb, side_cols), F32) + _nbytes((rb, side_cols), BF16)
    if norm_gain is not None:
        in_specs.append(pl.BlockSpec((1, bn), lambda jj, i: (0, col_use(jj))))
        args.append(norm_gain)
        out_specs += [io_p, io_s,
                      pl.BlockSpec((bm, LANES), lambda jj, i: (row_blk(jj, i), col_use(jj))),
                      pl.BlockSpec((ms, LANES), lambda jj, i: (0, col_use(jj)))]
        out_shape += [jax.ShapeDtypeStruct((mp, n_cols), BF16), jax.ShapeDtypeStruct((ms, n_cols), BF16),
                      jax.ShapeDtypeStruct((mp, n_j * LANES), F32), jax.ShapeDtypeStruct((ms, n_j * LANES), F32)]
        side_bytes += _nbytes((bm + ms, bn), BF16) + _nbytes((bm + ms, LANES), F32)
    body = functools.partial(_ws_matmul_body, k_bounds=tuple(bounds), has_res=res is not None,
                             has_side=side_cast is not None, has_gain=norm_gain is not None, act=act,
                             n_prompt_blocks=n_p, chunk_rows=chunk_rows, transposed=transposed)
    n_io = 2 if res is not None else 1
    pipelined = (_nbytes((bm + ms, k_total), BF16) + _nbytes((k_total // n_p, bn), F32)
                 + n_io * _nbytes((bm + ms, bn), F32) + side_bytes)
    limit = min(VMEM_CAP, 2 * pipelined + _nbytes((2, k_total, bn), BF16) + 2 * _nbytes((bm, bn), F32))
    return pl.pallas_call(
        body, grid=(n_j + 1, n_p + 1), in_specs=in_specs, out_specs=out_specs, out_shape=out_shape,
        scratch_shapes=[scratch],
        compiler_params=pltpu.CompilerParams(
            dimension_semantics=("arbitrary", "arbitrary"), vmem_limit_bytes=int(limit)),
        name=name)(*args)


def _linear_scan_rows(a, b, carry):
    t, lanes = b.shape
    g = t // SUBLANES
    b3 = b.reshape(g, SUBLANES, lanes)
    a3 = None if a is None else a.reshape(g, SUBLANES, lanes)
    sub = lax.broadcasted_iota(jnp.int32, b3.shape, 1)
    s = 1
    while s < SUBLANES:
        keep = sub >= s
        b_prev = jnp.where(keep, pltpu.roll(b3, s, 1), 0.0)
        if a3 is None:
            b3 = b3 + b_prev
        else:
            b3 = a3 * b_prev + b3
            a3 = a3 * jnp.where(keep, pltpu.roll(a3, s, 1), 1.0)
        s *= 2
    outs = []
    for j in range(g):
        h = b3[j] + carry if a3 is None else b3[j] + a3[j] * carry
        outs.append(h)
        carry = h[SUBLANES - 1:, :]
    return jnp.concatenate(outs, axis=0), carry


def _softplus(x):
    return jnp.maximum(x, 0.0) + jnp.log1p(jnp.exp(-jnp.abs(x)))


def _gla_body(q_ref, k_ref, v_ref, g_ref, alr_ref, wa_ref, ba_ref, gn_ref, s0_ref,
              o_ref, s_out_ref, st_ref, *, chunk, n_chunks, n_pairs):
    n = pl.program_id(2)

    @pl.when(n == 0)
    def _():
        for p in range(n_pairs):
            st_ref[p] = s0_ref[0, p].T

    c = chunk
    qw, vw = 2 * DK_A, 2 * DV_A
    lane = lax.broadcasted_iota(jnp.int32, (c, qw), 1)
    head0 = lane < DK_A
    row2 = lax.broadcasted_iota(jnp.int32, (2 * c, 2 * c), 0)
    col2 = lax.broadcasted_iota(jnp.int32, (2 * c, 2 * c), 1)
    causal = ((row2 >= c) == (col2 >= c)) & (row2 >= col2)
    nt = (((1,), (1,)), ((), ()))
    tn = (((0,), (0,)), ((), ()))

    def stack_heads(z):
        return jnp.concatenate([jnp.where(head0, z, 0.0), jnp.where(head0, 0.0, z)], axis=0).astype(BF16)

    for ci in range(n_chunks):
        rows = pl.ds(ci * c, c)
        z = jnp.dot(alr_ref[rows, :].astype(BF16), wa_ref[...], preferred_element_type=F32) + ba_ref[...]
        log_alpha = -(jnp.maximum(-z, 0.0) + jnp.log(1.0 + jnp.exp(-jnp.abs(z)))) / GLA_TAU
        b, b_end = _linear_scan_rows(None, log_alpha, jnp.zeros((1, log_alpha.shape[1]), F32))
        q = q_ref[rows, :] * (DK_A ** -0.5)
        k = k_ref[rows, :]
        q_dec = q * jnp.exp(b)
        k_inv = k * jnp.exp(-b)
        k_end = k * jnp.exp(b_end - b)
        decay = jnp.exp(b_end)
        for p in range(n_pairs):
            ql = slice(p * qw, (p + 1) * qw)
            q_t, k_t, k_e = stack_heads(q_dec[:, ql]), stack_heads(k_inv[:, ql]), stack_heads(k_end[:, ql])
            v0 = v_ref[rows, pl.ds(p * vw, DV_A)]
            v1 = v_ref[rows, pl.ds(p * vw + DV_A, DV_A)]
            v_s = jnp.concatenate([v0, v1], axis=0).astype(BF16)
            att = lax.dot_general(q_t, k_t, nt, preferred_element_type=F32)
            att = jnp.where(causal, att, 0.0).astype(BF16)
            st = st_ref[p]
            o = (jnp.dot(att, v_s, preferred_element_type=F32)
                 + lax.dot_general(q_t, st.astype(BF16), nt, preferred_element_type=F32))
            st_ref[p] = decay[:, ql] * st + lax.dot_general(v_s, k_e, tn, preferred_element_type=F32)
            ms = jnp.mean(jnp.square(o), axis=-1, keepdims=True)
            o = (o * lax.rsqrt(ms + EPS) * gn_ref[...])
            g0 = g_ref[rows, pl.ds(p * vw, DV_A)]
            g1 = g_ref[rows, pl.ds(p * vw + DV_A, DV_A)]
            o_ref[rows, pl.ds(p * vw, DV_A)] = (o[:c] * (g0 * jax.nn.sigmoid(g0))).astype(o_ref.dtype)
            o_ref[rows, pl.ds(p * vw + DV_A, DV_A)] = (o[c:] * (g1 * jax.nn.sigmoid(g1))).astype(o_ref.dtype)

    @pl.when(n == pl.num_programs(2) - 1)
    def _():
        for p in range(n_pairs):
            s_out_ref[0, p] = st_ref[p].T


def _gla(hproj, alr, wa, ba, gn, s0, *, n_seq, seq_len, col_q, col_k, col_v, col_g):
    t_total = hproj.shape[0]
    h_a = s0.shape[1]
    pairs = h_a // 2
    pps = GLA_PAIRS_PER_STEP
    assert pairs % pps == 0
    chunk = min(CHUNK, seq_len)
    n_chunks = min(8, seq_len // chunk)
    tb = chunk * n_chunks
    nb = seq_len // tb
    qw, vw = 2 * DK_A * pps, 2 * DV_A * pps
    assert col_q % qw == 0 and col_k % qw == 0 and col_v % vw == 0 and col_g % vw == 0
    cq, ck, cv, cg = col_q // qw, col_k // qw, col_v // vw, col_g // vw
    row = lambda s, n: s * nb + n
    in_specs = [
        pl.BlockSpec((tb, qw), lambda s, p, n: (row(s, n), cq + p)),
        pl.BlockSpec((tb, qw), lambda s, p, n: (row(s, n), ck + p)),
        pl.BlockSpec((tb, vw), lambda s, p, n: (row(s, n), cv + p)),
        pl.BlockSpec((tb, vw), lambda s, p, n: (row(s, n), cg + p)),
        pl.BlockSpec((tb, LANES), lambda s, p, n: (row(s, n), 0)),
        pl.BlockSpec((LANES, qw), lambda s, p, n: (0, p)),
        pl.BlockSpec((1, qw), lambda s, p, n: (0, p)),
        pl.BlockSpec((1, DV_A), lambda s, p, n: (0, 0)),
        pl.BlockSpec((1, pps, 2 * DK_A, DV_A), lambda s, p, n: (s, p, 0, 0)),
    ]
    out_specs = [
        pl.BlockSpec((tb, vw), lambda s, p, n: (row(s, n), p)),
        pl.BlockSpec((1, pps, 2 * DK_A, DV_A), lambda s, p, n: (s, p, 0, 0)),
    ]
    o_a, s_new = pl.pallas_call(
        functools.partial(_gla_body, chunk=chunk, n_chunks=n_chunks, n_pairs=pps),
        grid=(n_seq, pairs // pps, nb), in_specs=in_specs, out_specs=out_specs,
        out_shape=[jax.ShapeDtypeStruct((t_total, h_a * DV_A), BF16),
                   jax.ShapeDtypeStruct((n_seq, pairs, 2 * DK_A, DV_A), F32)],
        scratch_shapes=[pltpu.VMEM((pps, DV_A, 2 * DK_A), F32)],
        compiler_params=pltpu.CompilerParams(
            dimension_semantics=("parallel", "parallel", "arbitrary"),
            vmem_limit_bytes=_vmem_limit(_nbytes((tb, 2 * qw + 2 * vw + LANES), F32), _nbytes((tb, vw), BF16))),
        name="gla")(hproj, hproj, hproj, hproj, alr, wa, ba, gn, s0.reshape(n_seq, pairs, 2 * DK_A, DV_A))
    return o_a, s_new.reshape(n_seq, h_a, DK_A, DV_A)


def _gmlp_body(u_ref, v_ref, w_ref, b_ref, o_ref, *vg_refs, c, dh):
    tril = (lax.broadcasted_iota(jnp.int32, (c, c), 0) >= lax.broadcasted_iota(jnp.int32, (c, c), 1))
    w_tril = [jnp.where(tril, w_ref[h, :c, :c], 0.0).astype(BF16) for h in range(w_ref.shape[0])]
    for ci in range(u_ref.shape[0] // c):
        rows = pl.ds(ci * c, c)
        vg = jax.nn.gelu(v_ref[rows, :])
        if vg_refs:
            vg_refs[0][rows, :] = vg
        ug = jax.nn.gelu(u_ref[rows, :])
        for h, w in enumerate(w_tril):
            cols = pl.ds(h * dh, dh)
            mixed = (jnp.dot(w, vg[:, h * dh:(h + 1) * dh].astype(BF16), preferred_element_type=F32)
                     + b_ref[:c, cols])
            o_ref[rows, cols] = (ug[:, h * dh:(h + 1) * dh] * mixed).astype(o_ref.dtype)


def _gmlp(hproj, w_sp, b_full, *, seq_len, col_u, col_v, want_v):
    t_total = hproj.shape[0]
    n_h, gc, _ = w_sp.shape
    w_b = b_full.shape[1]
    dh = w_b // n_h
    c = min(gc, seq_len)
    tb = c * min(4, seq_len // c)
    blk = lambda col: pl.BlockSpec((tb, w_b), lambda r: (r, col // w_b))
    out_specs = [pl.BlockSpec((tb, w_b), lambda r: (r, 0))]
    out_shape = [jax.ShapeDtypeStruct((t_total, w_b), BF16)]
    if want_v:
        out_specs.append(pl.BlockSpec((tb, w_b), lambda r: (r, 0)))
        out_shape.append(jax.ShapeDtypeStruct((t_total, w_b), F32))
    outs = pl.pallas_call(
        functools.partial(_gmlp_body, c=c, dh=dh), grid=(t_total // tb,),
        in_specs=[blk(col_u), blk(col_v),
                  pl.BlockSpec((n_h, gc, gc), lambda r: (0, 0, 0)),
                  pl.BlockSpec((gc, w_b), lambda r: (0, 0))],
        out_specs=out_specs, out_shape=out_shape,
        compiler_params=pltpu.CompilerParams(
            dimension_semantics=("parallel",),
            vmem_limit_bytes=_vmem_limit(4 * _nbytes((tb, w_b), F32), _nbytes((n_h, gc, gc), F32))),
        name="gmlp")(hproj, hproj, w_sp, b_full)
    return (outs[0], outs[1]) if want_v else (outs[0], None)


def _lru_body(x_ref, gate_ref, cw_ref, cb_ref, wr_ref, br_ref, wi_ref, bi_ref, lam_ref, h0_ref, tail0_ref,
              o_ref, h_ref, tail_ref, *, dh):
    n = pl.program_id(1)

    @pl.when(n == 0)
    def _():
        h_ref[...] = h0_ref[...]
        tail_ref[...] = tail0_ref[...]

    tb = x_ref.shape[0]
    row8 = lax.broadcasted_iota(jnp.int32, (SUBLANES, dh), 0)
    for hb in range(x_ref.shape[1] // dh):
        cols = pl.ds(hb * dh, dh)
        x = x_ref[:, cols]
        tail = tail_ref[0, :, cols]

        def delayed(j):
            rolled = pltpu.roll(x, j, 0)
            head = jnp.where(row8 >= j, rolled[:SUBLANES], pltpu.roll(tail, j, 0))
            return jnp.concatenate([head, rolled[SUBLANES:]], axis=0)

        y = cb_ref[:, cols] + delayed(CONV_W - 1) * cw_ref[0:1, cols]
        for j in range(1, CONV_W - 1):
            y = y + delayed(CONV_W - 1 - j) * cw_ref[j:j + 1, cols]
        y = y + x * cw_ref[CONV_W - 1:CONV_W, cols]
        tail_ref[0, :, cols] = x[tb - SUBLANES:]

        yb = y.astype(BF16)
        r = jax.nn.sigmoid(jnp.dot(yb, wr_ref[hb].astype(BF16), preferred_element_type=F32) + br_ref[:, cols])
        ig = jax.nn.sigmoid(jnp.dot(yb, wi_ref[hb].astype(BF16), preferred_element_type=F32) + bi_ref[:, cols])
        log_a = -LRU_C * r * _softplus(-lam_ref[:, cols])
        a = jnp.exp(log_a)
        bt = jnp.sqrt(-jnp.tanh(log_a) * (a * a + 1.0)) * (ig * y)
        hseq, h_last = _linear_scan_rows(a, bt, h_ref[0, :, cols])
        h_ref[0, :, cols] = h_last
        o_ref[:, cols] = (hseq * jax.nn.gelu(gate_ref[:, cols])).astype(o_ref.dtype)


def _lru(hproj, cw, cb, wr, br, wi, bi, lam, h0, tail0, *, n_seq, seq_len, col_x, col_gate):
    t_total = hproj.shape[0]
    w_c = cw.shape[1]
    n_h, dh, _ = wr.shape
    tb = min(seq_len, 256)
    nb = seq_len // tb
    vec = lambda rows: pl.BlockSpec((rows, w_c), lambda s, n: (0, 0))
    wspec = pl.BlockSpec((n_h, dh, dh), lambda s, n: (0, 0, 0))
    hspec = pl.BlockSpec((1, 1, w_c), lambda s, n: (s, 0, 0))
    tspec = pl.BlockSpec((1, SUBLANES, w_c), lambda s, n: (s, 0, 0))
    o_c, h_new, tail = pl.pallas_call(
        functools.partial(_lru_body, dh=dh), grid=(n_seq, nb),
        in_specs=[pl.BlockSpec((tb, w_c), lambda s, n: (s * nb + n, col_x // w_c)),
                  pl.BlockSpec((tb, w_c), lambda s, n: (s * nb + n, col_gate // w_c)),
                  vec(CONV_W), vec(1), wspec, vec(1), wspec, vec(1), vec(1), hspec, tspec],
        out_specs=[pl.BlockSpec((tb, w_c), lambda s, n: (s * nb + n, 0)), hspec, tspec],
        out_shape=[jax.ShapeDtypeStruct((t_total, w_c), BF16),
                   jax.ShapeDtypeStruct((n_seq, 1, w_c), F32),
                   jax.ShapeDtypeStruct((n_seq, SUBLANES, w_c), F32)],
        compiler_params=pltpu.CompilerParams(
            dimension_semantics=("parallel", "arbitrary"),
            vmem_limit_bytes=_vmem_limit(4 * _nbytes((tb, w_c), F32))),
        name="conv_rglru")(hproj, hproj, cw, cb.reshape(1, w_c), wr, br.reshape(1, w_c), wi,
                           bi.reshape(1, w_c), lam.reshape(1, w_c), h0.reshape(n_seq, 1, w_c), tail0)
    return o_c, h_new.reshape(n_seq, w_c), tail


def _mixers(hproj, alr, s_gla, s_conv, s_lru, w, *, n_seq, seq_len, want_v):
    w_a = s_gla.shape[1] * DV_A
    qk_a = s_gla.shape[1] * DK_A
    w_b = w_c = s_lru.shape[1]
    col_u = 2 * qk_a + 2 * w_a
    o_a, s_gla_new = _gla(hproj, alr, w["w_alpha2"], w["b_alpha"], w["g_onorm"], s_gla, n_seq=n_seq,
                          seq_len=seq_len, col_q=0, col_k=qk_a, col_v=2 * qk_a, col_g=2 * qk_a + w_a)
    o_b, vg = _gmlp(hproj, w["w_spatial"], w["b_spatial"], seq_len=seq_len, col_u=col_u, col_v=col_u + w_b,
                    want_v=want_v)
    tail0 = jnp.pad(s_conv, ((0, 0), (SUBLANES - (CONV_W - 1), 0), (0, 0)))
    o_c, s_lru_new, tail = _lru(hproj, w["conv_w"], w["conv_b"], w["w_rgate"], w["b_rgate"], w["w_igate"],
                                w["b_igate"], w["lru_lambda"], s_lru, tail0, n_seq=n_seq, seq_len=seq_len,
                                col_x=col_u + 2 * w_b, col_gate=col_u + 2 * w_b + w_c)
    return [o_a, o_b, o_c], (s_gla_new, tail[:, SUBLANES - (CONV_W - 1):], s_lru_new, vg)


def _small_layer_weights(l, w_in_t, w_alpha2, b_alpha, g_onorm, w_spatial, b_spatial, conv_w, conv_b,
                         w_rgate, b_rgate, w_igate, b_igate, lru_lambda):
    d = w_in_t.shape[2]
    w_a, w_b = d // 2, d // 4
    qk_a = (w_a // DV_A) * DK_A
    c_alr = 2 * qk_a + 2 * w_a
    w_alr = jnp.pad(w_in_t[l, c_alr:c_alr + GLA_RANK, :], ((0, LANES - GLA_RANK), (0, 0)))
    w_alpha2_p = jnp.pad(w_alpha2[l], ((0, LANES - GLA_RANK), (0, 0))).astype(BF16)
    b_full = jnp.repeat(b_spatial[l].T, w_b // H_B, axis=1)
    return dict(
        w_alr=w_alr, w_alpha2=w_alpha2_p, b_alpha=b_alpha[l].reshape(1, qk_a),
        g_onorm=g_onorm[l].reshape(1, DV_A), w_spatial=w_spatial[l], b_spatial=b_full,
        conv_w=conv_w[l], conv_b=conv_b[l], w_rgate=w_rgate[l], b_rgate=b_rgate[l],
        w_igate=w_igate[l], b_igate=b_igate[l], lru_lambda=lru_lambda[l])


def kernel(x_prompt, x_sample, state_gla, state_conv, state_lru, ln1, w_in, w_alpha2, b_alpha, g_onorm, w_spatial, b_spatial, conv_w, conv_b, w_rgate, b_rgate, w_igate, b_igate, lru_lambda, w_out, ln2, w_up, w_down, ln_final):
    bp, tp, d = x_prompt.shape
    bs, ts, _ = x_sample.shape
    depth = w_in.shape[0]
    w_a, w_b, w_c = d // 2, d // 4, d // 4
    h_a = w_a // DV_A
    qk_a = h_a * DK_A
    n_a = 2 * qk_a + 2 * w_a
    n_b = 2 * w_b + 2 * w_c
    zero_gla = jnp.zeros((bp, h_a, DK_A, DV_A), F32)
    zero_conv = jnp.zeros((bp, CONV_W - 1, w_c), F32)
    zero_lru = jnp.zeros((bp, w_c), F32)
    xp = x_prompt.reshape(bp * tp, d)
    xs = x_sample.reshape(bs * ts, d)
    w_in_t = jnp.swapaxes(w_in, 1, 2)
    mm = _ws_matmul
    st_p, st_s = [], []
    for l in range(depth):
        w = _small_layer_weights(l, w_in_t, w_alpha2, b_alpha, g_onorm, w_spatial, b_spatial, conv_w, conv_b,
                                 w_rgate, b_rgate, w_igate, b_igate, lru_lambda)
        xn_p, alr_p = _rmsnorm(xp, ln1[l], BF16, w_proj=w["w_alr"])
        xn_s, alr_s = _rmsnorm(xs, ln1[l], BF16, w_proj=w["w_alr"])
        hp_p, hp_s = mm([xn_p], [xn_s], w_in_t, l, n_cols=n_a + n_b, transposed=True,
                        w_row_gap=(n_a, GLA_RANK), bm=1024, name="in_proj")
        mix_p, new_p = _mixers(hp_p, alr_p, zero_gla, zero_conv, zero_lru, w, n_seq=bp, seq_len=tp,
                               want_v=False)
        mix_s, new_s = _mixers(hp_s, alr_s, state_gla[l], state_conv[l], state_lru[l], w, n_seq=bs,
                               seq_len=ts, want_v=True)
        st_p.append(new_p)
        st_s.append(new_s)
        xp, xs, xg_p, xg_s, ssq_p, ssq_s = mm(mix_p, mix_s, w_out, l, n_cols=d, res=(xp, xs),
                                              norm_gain=ln2[l].reshape(1, d), name="out_proj")
        hid_p, hid_s, w_down_bf = mm([xg_p], [xg_s], w_up, l, n_cols=w_up.shape[2], act="relu2",
                                     out_dtype=BF16, side_cast=(w_down, l), bm=1024, name="mlp_up")
        xp, xs = _ktiled_matmul(hid_p, hid_s, w_down_bf, (xp, xs), (ssq_p, ssq_s), d, bm=1024, bn=1024,
                                bk=2048, name="mlp_down")
    y_prompt = _rmsnorm(xp, ln_final, F32).reshape(bp, tp, d)
    y_sample = _rmsnorm(xs, ln_final, F32).reshape(bs, ts, d)
    stack = lambda sts, i: jnp.stack([s[i] for s in sts])
    return (y_prompt, y_sample, stack(st_p, 0), stack(st_p, 1), stack(st_p, 2),
            stack(st_s, 0), stack(st_s, 1), stack(st_s, 2),
            jnp.stack([s[3].reshape(bs, ts, w_b) for s in st_s]))
```

```python
import functools

import jax
import jax.numpy as jnp
from jax import lax
from jax.experimental import pallas as pl
from jax.experimental.pallas import tpu as pltpu

F32 = jnp.float32
BF16 = jnp.bfloat16

DV_A = 128
DK_A = 64
GLA_RANK = 16
GLA_TAU = 16.0
CHUNK = 64
GMLP_CHUNK = 128
H_B = 8
H_C = 8
CONV_W = 4
LRU_C = 8.0
EPS = 1e-6

LANES = 128
SUBLANES = 8
V7X_VMEM_BYTES = 64 * 1024 * 1024
VMEM_CAP = V7X_VMEM_BYTES - 6 * 1024 * 1024

GLA_PAIRS_PER_STEP = 8


def _vmem_limit(*block_bytes):
    need = 4 * sum(block_bytes)
    return int(min(max(need, 16 * 1024 * 1024), VMEM_CAP))


def _nbytes(shape, dtype):
    n = 1
    for s in shape:
        n *= s
    return n * jnp.dtype(dtype).itemsize


def _rmsnorm_body(x_ref, g_ref, o_ref):
    x = x_ref[...]
    ms = jnp.mean(jnp.square(x), axis=-1, keepdims=True)
    o_ref[...] = (x * lax.rsqrt(ms + EPS) * g_ref[...]).astype(o_ref.dtype)


def _rmsnorm_proj_body(x_ref, g_ref, w_ref, o_ref, p_ref):
    x = x_ref[...]
    ms = jnp.mean(jnp.square(x), axis=-1, keepdims=True)
    y = (x * lax.rsqrt(ms + EPS) * g_ref[...]).astype(BF16)
    o_ref[...] = y
    p_ref[...] = lax.dot_general(y, w_ref[...].astype(BF16), (((1,), (1,)), ((), ())),
                                 preferred_element_type=F32)


def _rmsnorm(x, g, out_dtype, w_proj=None):
    t, d = x.shape
    tb = min(t, 512)
    grid = (t // tb,)
    x_spec = pl.BlockSpec((tb, d), lambda i: (i, 0))
    g_spec = pl.BlockSpec((1, d), lambda i: (0, 0))
    params = pltpu.CompilerParams(
        dimension_semantics=("parallel",),
        vmem_limit_bytes=_vmem_limit(_nbytes((tb, d), F32), _nbytes((tb, d), out_dtype)))
    if w_proj is None:
        return pl.pallas_call(
            _rmsnorm_body, grid=grid, in_specs=[x_spec, g_spec], out_specs=x_spec,
            out_shape=jax.ShapeDtypeStruct((t, d), out_dtype), compiler_params=params,
            name="rmsnorm")(x, g.reshape(1, d))
    n = w_proj.shape[0]
    return pl.pallas_call(
        _rmsnorm_proj_body, grid=grid,
        in_specs=[x_spec, g_spec, pl.BlockSpec((n, d), lambda i: (0, 0))],
        out_specs=[x_spec, pl.BlockSpec((tb, n), lambda i: (i, 0))],
        out_shape=[jax.ShapeDtypeStruct((t, d), BF16), jax.ShapeDtypeStruct((t, n), F32)],
        compiler_params=params, name="rmsnorm_proj")(x, g.reshape(1, d), w_proj)


def _ktiled_matmul_body(a_p, a_s, w_ref, res_p, res_s, ssq_p, ssq_s, o_p, o_s, *, n_prompt_blocks, width):
    i = pl.program_id(0)

    def run(a_ref, res_ref, ssq_ref, o_ref):
        def scaled_dot():
            ss = ssq_ref[:, 0:1]
            for j in range(1, ssq_ref.shape[1] // LANES):
                ss = ss + ssq_ref[:, j * LANES:j * LANES + 1]
            acc = jnp.dot(a_ref[...], w_ref[...], preferred_element_type=F32)
            return acc * (1.0 / (ss / width + EPS))

        @pl.when(pl.program_id(2) == 0)
        def _():
            o_ref[...] = res_ref[...] + scaled_dot()

        @pl.when(pl.program_id(2) > 0)
        def _():
            o_ref[...] += scaled_dot()

    @pl.when(i < n_prompt_blocks)
    def _():
        run(a_p, res_p, ssq_p, o_p)

    @pl.when(i == n_prompt_blocks)
    def _():
        run(a_s, res_s, ssq_s, o_s)


def _ktiled_matmul(a_p, a_s, w, res, ssq, width, *, bm, bn, bk, name):
    mp, ms = a_p.shape[0], a_s.shape[0]
    k_total, n_cols = w.shape
    assert mp % bm == 0 and n_cols % bn == 0 and k_total % bk == 0
    n_p, n_j, nk = mp // bm, n_cols // bn, k_total // bk

    def p_row(i):
        return jnp.minimum(i, n_p - 1)

    def p_col(i, j):
        return jnp.where(i < n_p, j, n_j - 1)

    def s_col(i, j):
        return jnp.where(i == n_p, j, 0)

    io_p = pl.BlockSpec((bm, bn), lambda i, j, k: (p_row(i), p_col(i, j)))
    io_s = pl.BlockSpec((ms, bn), lambda i, j, k: (0, s_col(i, j)))
    in_specs = [
        pl.BlockSpec((bm, bk), lambda i, j, k: (p_row(i), jnp.where(i < n_p, k, nk - 1))),
        pl.BlockSpec((ms, bk), lambda i, j, k: (0, jnp.where(i == n_p, k, 0))),
        pl.BlockSpec((bk, bn), lambda i, j, k: (k, j)),
        io_p, io_s,
        pl.BlockSpec((bm, ssq[0].shape[1]), lambda i, j, k: (p_row(i), 0)),
        pl.BlockSpec((ms, ssq[1].shape[1]), lambda i, j, k: (0, 0)),
    ]
    limit = _vmem_limit(_nbytes((bm + ms, bk), BF16), _nbytes((bk, bn), BF16), 2 * _nbytes((bm + ms, bn), F32),
                        _nbytes((bm + ms, ssq[0].shape[1]), F32))
    return pl.pallas_call(
        functools.partial(_ktiled_matmul_body, n_prompt_blocks=n_p, width=width),
        grid=(n_p + 1, n_j, nk), in_specs=in_specs, out_specs=[io_p, io_s],
        out_shape=[jax.ShapeDtypeStruct((mp, n_cols), F32), jax.ShapeDtypeStruct((ms, n_cols), F32)],
        compiler_params=pltpu.CompilerParams(
            dimension_semantics=("arbitrary", "arbitrary", "arbitrary"), vmem_limit_bytes=limit),
        name=name)(a_p, a_s, w, res[0], res[1], ssq[0], ssq[1])


def _ws_matmul_body(*refs, k_bounds, has_res, has_side, has_gain, act, n_prompt_blocks, chunk_rows,
                    transposed):
    refs = list(refs)
    n_lhs = len(k_bounds)
    take = lambda n: [refs.pop(0) for _ in range(n)]
    lhs_p, lhs_s = take(n_lhs), take(n_lhs)
    (wc_ref,) = take(1)
    res_p, res_s = take(2) if has_res else (None, None)
    (side_in,) = take(1) if has_side else (None,)
    (gain_ref,) = take(1) if has_gain else (None,)
    o_p, o_s = take(2)
    (side_out,) = take(1) if has_side else (None,)
    gained_p, gained_s, ssq_p, ssq_s = take(4) if has_gain else (None,) * 4
    (w_scr,) = take(1)
    jj, i = pl.program_id(0), pl.program_id(1)
    slot_load = jj % 2
    slot_use = 1 - slot_load

    def stage():
        r0 = pl.multiple_of(jnp.minimum(i, n_prompt_blocks - 1) * chunk_rows, chunk_rows)
        chunk = wc_ref[0] if len(wc_ref.shape) == 3 else wc_ref[...]
        w_scr[slot_load, pl.ds(r0, chunk_rows), :] = chunk.astype(BF16)

    def run(lhs_refs, res_ref, o_ref, gained_ref, ssq_ref):
        stage()
        acc = None
        for a_ref, (k0, k1) in zip(lhs_refs, k_bounds):
            if transposed:
                part = lax.dot_general(a_ref[...], w_scr[slot_use, :, k0:k1], (((1,), (1,)), ((), ())),
                                       preferred_element_type=F32)
            else:
                part = jnp.dot(a_ref[...], w_scr[slot_use, k0:k1, :], preferred_element_type=F32)
            acc = part if acc is None else acc + part
        if has_res:
            acc = res_ref[...] + acc
        if act == "relu2":
            acc = jnp.square(jnp.maximum(acc, 0.0))
        o_ref[...] = acc.astype(o_ref.dtype)
        if has_gain:
            gained_ref[...] = (acc * gain_ref[...]).astype(gained_ref.dtype)
            ssq_ref[...] = jnp.broadcast_to(jnp.sum(acc * acc, axis=1, keepdims=True), ssq_ref.shape)

    @pl.when(jj == 0)
    def _():
        stage()

    @pl.when((jj > 0) & (i < n_prompt_blocks))
    def _():
        if has_side:
            side_out[...] = side_in[...].astype(side_out.dtype)
        run(lhs_p, res_p, o_p, gained_p, ssq_p)

    @pl.when((jj > 0) & (i == n_prompt_blocks))
    def _():
        run(lhs_s, res_s, o_s, gained_s, ssq_s)


def _ws_matmul(lhs_p, lhs_s, w, layer, *, n_cols, transposed=False, w_row_gap=(0, 0), bm=512, bn=1024,
               res=None, act=None, out_dtype=F32, side_cast=None, norm_gain=None, name="ws_matmul"):
    mp, ms = lhs_p[0].shape[0], lhs_s[0].shape[0]
    k_total = w.shape[2] if transposed else w.shape[1]
    widths = [a.shape[1] for a in lhs_p]
    assert sum(widths) == k_total and mp % bm == 0 and n_cols % bn == 0
    n_p, n_j = mp // bm, n_cols // bn
    bounds, k0 = [], 0
    for wd in widths:
        bounds.append((k0, k0 + wd))
        k0 += wd

    def row_blk(jj, i):
        return jnp.where(jj == 0, 0, jnp.minimum(i, n_p - 1))

    def lhs_row_blk(jj, i):
        return jnp.where((jj == 0) | (i == n_p), 0, i)

    def col_use(jj):
        return jnp.maximum(jj - 1, 0)

    def chunk(jj, i):
        return jnp.where(jj < n_j, jnp.minimum(i, n_p - 1), n_p - 1)

    def col_load(jj):
        return jnp.minimum(jj, n_j - 1)

    if transposed:
        chunk_rows = bn // n_p
        gap_row, gap_rows = w_row_gap
        assert gap_row % bn == 0 and gap_rows % SUBLANES == 0

        def w_row(jj, i):
            row = (col_load(jj) * n_p + chunk(jj, i)) * chunk_rows
            return pl.multiple_of(row + jnp.where(row >= gap_row, gap_rows, 0), SUBLANES)

        w_spec = pl.BlockSpec((pl.Element(1), pl.Element(chunk_rows), pl.Element(k_total)),
                              lambda jj, i: (layer, w_row(jj, i), 0))
        scratch = pltpu.VMEM((2, bn, k_total), BF16)
    else:
        chunk_rows = k_total // n_p
        w_spec = pl.BlockSpec((None, chunk_rows, bn), lambda jj, i: (layer, chunk(jj, i), col_load(jj)))
        scratch = pltpu.VMEM((2, k_total, bn), BF16)
    assert chunk_rows * n_p == (bn if transposed else k_total) and chunk_rows % 16 == 0
    specs_p = [pl.BlockSpec((bm, wd), lambda jj, i: (lhs_row_blk(jj, i), 0)) for wd in widths]
    specs_s = [pl.BlockSpec((ms, wd), lambda jj, i: (0, 0)) for wd in widths]
    io_p = pl.BlockSpec((bm, bn), lambda jj, i: (row_blk(jj, i), col_use(jj)))
    io_s = pl.BlockSpec((ms, bn), lambda jj, i: (0, col_use(jj)))
    in_specs = specs_p + specs_s + [w_spec]
    args = list(lhs_p) + list(lhs_s) + [w]
    if res is not None:
        in_specs += [io_p, io_s]
        args += list(res)
    out_specs = [io_p, io_s]
    out_shape = [jax.ShapeDtypeStruct((mp, n_cols), out_dtype), jax.ShapeDtypeStruct((ms, n_cols), out_dtype)]
    side_bytes = 0
    if side_cast is not None:
        side, side_layer = side_cast
        side_rows, side_cols = side.shape[1], side.shape[2]
        rb = side_rows // (n_j * n_p)
        assert rb * n_j * n_p == side_rows and rb % 16 == 0

        def side_blk(jj, i):
            return jnp.where(jj == 0, 0, (jj - 1) * n_p + jnp.minimum(i, n_p - 1))

        in_specs.append(pl.BlockSpec((None, rb, side_cols), lambda jj, i: (side_layer, side_blk(jj, i), 0)))
        args.append(side)
        out_specs.append(pl.BlockSpec((rb, side_cols), lambda jj, i: (side_blk(jj, i), 0)))
        out_shape.append(jax.ShapeDtypeStruct((side_rows, side_cols), BF16))
        side_bytes = _nbytes((rb, side_cols), F32) + _nbytes((rb, side_cols), BF16)
    if norm_gain is not None:
        in_specs.append(pl.BlockSpec((1, bn), lambda jj, i: (0, col_use(jj))))
        args.append(norm_gain)
        out_specs += [io_p, io_s,
                      pl.BlockSpec((bm, LANES), lambda jj, i: (row_blk(jj, i), col_use(jj))),
                      pl.BlockSpec((ms, LANES), lambda jj, i: (0, col_use(jj)))]
        out_shape += [jax.ShapeDtypeStruct((mp, n_cols), BF16), jax.ShapeDtypeStruct((ms, n_cols), BF16),
                      jax.ShapeDtypeStruct((mp, n_j * LANES), F32), jax.ShapeDtypeStruct((ms, n_j * LANES), F32)]
        side_bytes += _nbytes((bm + ms, bn), BF16) + _nbytes((bm + ms, LANES), F32)
    body = functools.partial(_ws_matmul_body, k_bounds=tuple(bounds), has_res=res is not None,
                             has_side=side_cast is not None, has_gain=norm_gain is not None, act=act,
                             n_prompt_blocks=n_p, chunk_rows=chunk_rows, transposed=transposed)
    n_io = 2 if res is not None else 1
    pipelined = (_nbytes((bm + ms, k_total), BF16) + _nbytes((k_total // n_p, bn), F32)
                 + n_io * _nbytes((bm + ms, bn), F32) + side_bytes)
    limit = min(VMEM_CAP, 2 * pipelined + _nbytes((2, k_total, bn), BF16) + 2 * _nbytes((bm, bn), F32))
    return pl.pallas_call(
        body, grid=(n_j + 1, n_p + 1), in_specs=in_specs, out_specs=out_specs, out_shape=out_shape,
        scratch_shapes=[scratch],
        compiler_params=pltpu.CompilerParams(
            dimension_semantics=("arbitrary", "arbitrary"), vmem_limit_bytes=int(limit)),
        name=name)(*args)


def _linear_scan_rows(a, b, carry):
    t, lanes = b.shape
    g = t // SUBLANES
    b3 = b.reshape(g, SUBLANES, lanes)
    a3 = None if a is None else a.reshape(g, SUBLANES, lanes)
    sub = lax.broadcasted_iota(jnp.int32, b3.shape, 1)
    s = 1
    while s < SUBLANES:
        keep = sub >= s
        b_prev = jnp.where(keep, pltpu.roll(b3, s, 1), 0.0)
        if a3 is None:
            b3 = b3 + b_prev
        else:
            b3 = a3 * b_prev + b3
            a3 = a3 * jnp.where(keep, pltpu.roll(a3, s, 1), 1.0)
        s *= 2
    outs = []
    for j in range(g):
        h = b3[j] + carry if a3 is None else b3[j] + a3[j] * carry
        outs.append(h)
        carry = h[SUBLANES - 1:, :]
    return jnp.concatenate(outs, axis=0), carry


def _softplus(x):
    return jnp.maximum(x, 0.0) + jnp.log1p(jnp.exp(-jnp.abs(x)))


def _gla_body(q_ref, k_ref, v_ref, g_ref, alr_ref, wa_ref, ba_ref, gn_ref, s0_ref,
              o_ref, s_out_ref, st_ref, *, chunk, n_chunks, n_pairs):
    n = pl.program_id(2)

    @pl.when(n == 0)
    def _():
        for p in range(n_pairs):
            st_ref[p] = s0_ref[0, p].T

    c = chunk
    qw, vw = 2 * DK_A, 2 * DV_A
    lane = lax.broadcasted_iota(jnp.int32, (c, qw), 1)
    head0 = lane < DK_A
    row2 = lax.broadcasted_iota(jnp.int32, (2 * c, 2 * c), 0)
    col2 = lax.broadcasted_iota(jnp.int32, (2 * c, 2 * c), 1)
    causal = ((row2 >= c) == (col2 >= c)) & (row2 >= col2)
    nt = (((1,), (1,)), ((), ()))
    tn = (((0,), (0,)), ((), ()))

    def stack_heads(z):
        return jnp.concatenate([jnp.where(head0, z, 0.0), jnp.where(head0, 0.0, z)], axis=0).astype(BF16)

    for ci in range(n_chunks):
        rows = pl.ds(ci * c, c)
        z = jnp.dot(alr_ref[rows, :].astype(BF16), wa_ref[...], preferred_element_type=F32) + ba_ref[...]
        log_alpha = -(jnp.maximum(-z, 0.0) + jnp.log(1.0 + jnp.exp(-jnp.abs(z)))) / GLA_TAU
        b, b_end = _linear_scan_rows(None, log_alpha, jnp.zeros((1, log_alpha.shape[1]), F32))
        q = q_ref[rows, :] * (DK_A ** -0.5)
        k = k_ref[rows, :]
        q_dec = q * jnp.exp(b)
        k_inv = k * jnp.exp(-b)
        k_end = k * jnp.exp(b_end - b)
        decay = jnp.exp(b_end)
        for p in range(n_pairs):
            ql = slice(p * qw, (p + 1) * qw)
            q_t, k_t, k_e = stack_heads(q_dec[:, ql]), stack_heads(k_inv[:, ql]), stack_heads(k_end[:, ql])
            v0 = v_ref[rows, pl.ds(p * vw, DV_A)]
            v1 = v_ref[rows, pl.ds(p * vw + DV_A, DV_A)]
            v_s = jnp.concatenate([v0, v1], axis=0).astype(BF16)
            att = lax.dot_general(q_t, k_t, nt, preferred_element_type=F32)
            att = jnp.where(causal, att, 0.0).astype(BF16)
            st = st_ref[p]
            o = (jnp.dot(att, v_s, preferred_element_type=F32)
                 + lax.dot_general(q_t, st.astype(BF16), nt, preferred_element_type=F32))
            st_ref[p] = decay[:, ql] * st + lax.dot_general(v_s, k_e, tn, preferred_element_type=F32)
            ms = jnp.mean(jnp.square(o), axis=-1, keepdims=True)
            o = (o * lax.rsqrt(ms + EPS) * gn_ref[...])
            g0 = g_ref[rows, pl.ds(p * vw, DV_A)]
            g1 = g_ref[rows, pl.ds(p * vw + DV_A, DV_A)]
            o_ref[rows, pl.ds(p * vw, DV_A)] = (o[:c] * (g0 * jax.nn.sigmoid(g0))).astype(o_ref.dtype)
            o_ref[rows, pl.ds(p * vw + DV_A, DV_A)] = (o[c:] * (g1 * jax.nn.sigmoid(g1))).astype(o_ref.dtype)

    @pl.when(n == pl.num_programs(2) - 1)
    def _():
        for p in range(n_pairs):
            s_out_ref[0, p] = st_ref[p].T


def _gla(hproj, alr, wa, ba, gn, s0, *, n_seq, seq_len, col_q, col_k, col_v, col_g):
    t_total = hproj.shape[0]
    h_a = s0.shape[1]
    pairs = h_a // 2
    pps = GLA_PAIRS_PER_STEP
    assert pairs % pps == 0
    chunk = min(CHUNK, seq_len)
    n_chunks = min(8, seq_len // chunk)
    tb = chunk * n_chunks
    nb = seq_len // tb
    qw, vw = 2 * DK_A * pps, 2 * DV_A * pps
    assert col_q % qw == 0 and col_k % qw == 0 and col_v % vw == 0 and col_g % vw == 0
    cq, ck, cv, cg = col_q // qw, col_k // qw, col_v // vw, col_g // vw
    row = lambda s, n: s * nb + n
    in_specs = [
        pl.BlockSpec((tb, qw), lambda s, p, n: (row(s, n), cq + p)),
        pl.BlockSpec((tb, qw), lambda s, p, n: (row(s, n), ck + p)),
        pl.BlockSpec((tb, vw), lambda s, p, n: (row(s, n), cv + p)),
        pl.BlockSpec((tb, vw), lambda s, p, n: (row(s, n), cg + p)),
        pl.BlockSpec((tb, LANES), lambda s, p, n: (row(s, n), 0)),
        pl.BlockSpec((LANES, qw), lambda s, p, n: (0, p)),
        pl.BlockSpec((1, qw), lambda s, p, n: (0, p)),
        pl.BlockSpec((1, DV_A), lambda s, p, n: (0, 0)),
        pl.BlockSpec((1, pps, 2 * DK_A, DV_A), lambda s, p, n: (s, p, 0, 0)),
    ]
    out_specs = [
        pl.BlockSpec((tb, vw), lambda s, p, n: (row(s, n), p)),
        pl.BlockSpec((1, pps, 2 * DK_A, DV_A), lambda s, p, n: (s, p, 0, 0)),
    ]
    o_a, s_new = pl.pallas_call(
        functools.partial(_gla_body, chunk=chunk, n_chunks=n_chunks, n_pairs=pps),
        grid=(n_seq, pairs // pps, nb), in_specs=in_specs, out_specs=out_specs,
        out_shape=[jax.ShapeDtypeStruct((t_total, h_a * DV_A), BF16),
                   jax.ShapeDtypeStruct((n_seq, pairs, 2 * DK_A, DV_A), F32)],
        scratch_shapes=[pltpu.VMEM((pps, DV_A, 2 * DK_A), F32)],
        compiler_params=pltpu.CompilerParams(
            dimension_semantics=("parallel", "parallel", "arbitrary"),
            vmem_limit_bytes=_vmem_limit(_nbytes((tb, 2 * qw + 2 * vw + LANES), F32), _nbytes((tb, vw), BF16))),
        name="gla")(hproj, hproj, hproj, hproj, alr, wa, ba, gn, s0.reshape(n_seq, pairs, 2 * DK_A, DV_A))
    return o_a, s_new.reshape(n_seq, h_a, DK_A, DV_A)


def _gmlp_body(u_ref, v_ref, w_ref, b_ref, o_ref, *vg_refs, c, dh):
    tril = (lax.broadcasted_iota(jnp.int32, (c, c), 0) >= lax.broadcasted_iota(jnp.int32, (c, c), 1))
    w_tril = [jnp.where(tril, w_ref[h, :c, :c], 0.0).astype(BF16) for h in range(w_ref.shape[0])]
    for ci in range(u_ref.shape[0] // c):
        rows = pl.ds(ci * c, c)
        vg = jax.nn.gelu(v_ref[rows, :])
        if vg_refs:
            vg_refs[0][rows, :] = vg
        ug = jax.nn.gelu(u_ref[rows, :])
        for h, w in enumerate(w_tril):
            cols = pl.ds(h * dh, dh)
            mixed = (jnp.dot(w, vg[:, h * dh:(h + 1) * dh].astype(BF16), preferred_element_type=F32)
                     + b_ref[:c, cols])
            o_ref[rows, cols] = (ug[:, h * dh:(h + 1) * dh] * mixed).astype(o_ref.dtype)


def _gmlp(hproj, w_sp, b_full, *, seq_len, col_u, col_v, want_v):
    t_total = hproj.shape[0]
    n_h, gc, _ = w_sp.shape
    w_b = b_full.shape[1]
    dh = w_b // n_h
    c = min(gc, seq_len)
    tb = c * min(4, seq_len // c)
    blk = lambda col: pl.BlockSpec((tb, w_b), lambda r: (r, col // w_b))
    out_specs = [pl.BlockSpec((tb, w_b), lambda r: (r, 0))]
    out_shape = [jax.ShapeDtypeStruct((t_total, w_b), BF16)]
    if want_v:
        out_specs.append(pl.BlockSpec((tb, w_b), lambda r: (r, 0)))
        out_shape.append(jax.ShapeDtypeStruct((t_total, w_b), F32))
    outs = pl.pallas_call(
        functools.partial(_gmlp_body, c=c, dh=dh), grid=(t_total // tb,),
        in_specs=[blk(col_u), blk(col_v),
                  pl.BlockSpec((n_h, gc, gc), lambda r: (0, 0, 0)),
                  pl.BlockSpec((gc, w_b), lambda r: (0, 0))],
        out_specs=out_specs, out_shape=out_shape,
        compiler_params=pltpu.CompilerParams(
            dimension_semantics=("parallel",),
            vmem_limit_bytes=_vmem_limit(4 * _nbytes((tb, w_b), F32), _nbytes((n_h, gc, gc), F32))),
        name="gmlp")(hproj, hproj, w_sp, b_full)
    return (outs[0], outs[1]) if want_v else (outs[0], None)


def _lru_body(x_ref, gate_ref, cw_ref, cb_ref, wr_ref, br_ref, wi_ref, bi_ref, lam_ref, h0_ref, tail0_ref,
              o_ref, h_ref, tail_ref, *, dh):
    n = pl.program_id(1)

    @pl.when(n == 0)
    def _():
        h_ref[...] = h0_ref[...]
        tail_ref[...] = tail0_ref[...]

    tb = x_ref.shape[0]
    row8 = lax.broadcasted_iota(jnp.int32, (SUBLANES, dh), 0)
    for hb in range(x_ref.shape[1] // dh):
        cols = pl.ds(hb * dh, dh)
        x = x_ref[:, cols]
        tail = tail_ref[0, :, cols]

        def delayed(j):
            rolled = pltpu.roll(x, j, 0)
            head = jnp.where(row8 >= j, rolled[:SUBLANES], pltpu.roll(tail, j, 0))
            return jnp.concatenate([head, rolled[SUBLANES:]], axis=0)

        y = cb_ref[:, cols] + delayed(CONV_W - 1) * cw_ref[0:1, cols]
        for j in range(1, CONV_W - 1):
            y = y + delayed(CONV_W - 1 - j) * cw_ref[j:j + 1, cols]
        y = y + x * cw_ref[CONV_W - 1:CONV_W, cols]
        tail_ref[0, :, cols] = x[tb - SUBLANES:]

        yb = y.astype(BF16)
        r = jax.nn.sigmoid(jnp.dot(yb, wr_ref[hb].astype(BF16), preferred_element_type=F32) + br_ref[:, cols])
        ig = jax.nn.sigmoid(jnp.dot(yb, wi_ref[hb].astype(BF16), preferred_element_type=F32) + bi_ref[:, cols])
        log_a = r * (-LRU_C * _softplus(-lam_ref[:, cols]))
        a = jnp.exp(log_a)
        gain2 = -jnp.tanh(log_a) * (a * a + 1.0)
        bt = jnp.where(gain2 > 0.0, gain2 * lax.rsqrt(gain2), 0.0) * (ig * y)
        hseq, h_last = _linear_scan_rows(a, bt, h_ref[0, :, cols])
        h_ref[0, :, cols] = h_last
        o_ref[:, cols] = (hseq * jax.nn.gelu(gate_ref[:, cols])).astype(o_ref.dtype)


def _lru(hproj, cw, cb, wr, br, wi, bi, lam, h0, tail0, *, n_seq, seq_len, col_x, col_gate):
    t_total = hproj.shape[0]
    w_c = cw.shape[1]
    n_h, dh, _ = wr.shape
    tb = min(seq_len, 256)
    nb = seq_len // tb
    vec = lambda rows: pl.BlockSpec((rows, w_c), lambda s, n: (0, 0))
    wspec = pl.BlockSpec((n_h, dh, dh), lambda s, n: (0, 0, 0))
    hspec = pl.BlockSpec((1, 1, w_c), lambda s, n: (s, 0, 0))
    tspec = pl.BlockSpec((1, SUBLANES, w_c), lambda s, n: (s, 0, 0))
    o_c, h_new, tail = pl.pallas_call(
        functools.partial(_lru_body, dh=dh), grid=(n_seq, nb),
        in_specs=[pl.BlockSpec((tb, w_c), lambda s, n: (s * nb + n, col_x // w_c)),
                  pl.BlockSpec((tb, w_c), lambda s, n: (s * nb + n, col_gate // w_c)),
                  vec(CONV_W), vec(1), wspec, vec(1), wspec, vec(1), vec(1), hspec, tspec],
        out_specs=[pl.BlockSpec((tb, w_c), lambda s, n: (s * nb + n, 0)), hspec, tspec],
        out_shape=[jax.ShapeDtypeStruct((t_total, w_c), BF16),
                   jax.ShapeDtypeStruct((n_seq, 1, w_c), F32),
                   jax.ShapeDtypeStruct((n_seq, SUBLANES, w_c), F32)],
        compiler_params=pltpu.CompilerParams(
            dimension_semantics=("parallel", "arbitrary"),
            vmem_limit_bytes=_vmem_limit(4 * _nbytes((tb, w_c), F32))),
        name="conv_rglru")(hproj, hproj, cw, cb.reshape(1, w_c), wr, br.reshape(1, w_c), wi,
                           bi.reshape(1, w_c), lam.reshape(1, w_c), h0.reshape(n_seq, 1, w_c), tail0)
    return o_c, h_new.reshape(n_seq, w_c), tail


def _mixers(hproj, alr, s_gla, s_conv, s_lru, w, *, n_seq, seq_len, want_v):
    w_a = s_gla.shape[1] * DV_A
    qk_a = s_gla.shape[1] * DK_A
    w_b = w_c = s_lru.shape[1]
    col_u = 2 * qk_a + 2 * w_a
    o_a, s_gla_new = _gla(hproj, alr, w["w_alpha2"], w["b_alpha"], w["g_onorm"], s_gla, n_seq=n_seq,
                          seq_len=seq_len, col_q=0, col_k=qk_a, col_v=2 * qk_a, col_g=2 * qk_a + w_a)
    o_b, vg = _gmlp(hproj, w["w_spatial"], w["b_spatial"], seq_len=seq_len, col_u=col_u, col_v=col_u + w_b,
                    want_v=want_v)
    tail0 = jnp.pad(s_conv, ((0, 0), (SUBLANES - (CONV_W - 1), 0), (0, 0)))
    o_c, s_lru_new, tail = _lru(hproj, w["conv_w"], w["conv_b"], w["w_rgate"], w["b_rgate"], w["w_igate"],
                                w["b_igate"], w["lru_lambda"], s_lru, tail0, n_seq=n_seq, seq_len=seq_len,
                                col_x=col_u + 2 * w_b, col_gate=col_u + 2 * w_b + w_c)
    return [o_a, o_b, o_c], (s_gla_new, tail[:, SUBLANES - (CONV_W - 1):], s_lru_new, vg)


def _small_layer_weights(l, w_in_t, w_alpha2, b_alpha, g_onorm, w_spatial, b_spatial, conv_w, conv_b,
                         w_rgate, b_rgate, w_igate, b_igate, lru_lambda):
    d = w_in_t.shape[2]
    w_a, w_b = d // 2, d // 4
    qk_a = (w_a // DV_A) * DK_A
    c_alr = 2 * qk_a + 2 * w_a
    w_alr = jnp.pad(w_in_t[l, c_alr:c_alr + GLA_RANK, :], ((0, LANES - GLA_RANK), (0, 0)))
    w_alpha2_p = jnp.pad(w_alpha2[l], ((0, LANES - GLA_RANK), (0, 0))).astype(BF16)
    b_full = jnp.repeat(b_spatial[l].T, w_b // H_B, axis=1)
    return dict(
        w_alr=w_alr, w_alpha2=w_alpha2_p, b_alpha=b_alpha[l].reshape(1, qk_a),
        g_onorm=g_onorm[l].reshape(1, DV_A), w_spatial=w_spatial[l], b_spatial=b_full,
        conv_w=conv_w[l], conv_b=conv_b[l], w_rgate=w_rgate[l], b_rgate=b_rgate[l],
        w_igate=w_igate[l], b_igate=b_igate[l], lru_lambda=lru_lambda[l])


def kernel(x_prompt, x_sample, state_gla, state_conv, state_lru, ln1, w_in, w_alpha2, b_alpha, g_onorm, w_spatial, b_spatial, conv_w, conv_b, w_rgate, b_rgate, w_igate, b_igate, lru_lambda, w_out, ln2, w_up, w_down, ln_final):
    bp, tp, d = x_prompt.shape
    bs, ts, _ = x_sample.shape
    depth = w_in.shape[0]
    w_a, w_b, w_c = d // 2, d // 4, d // 4
    h_a = w_a // DV_A
    qk_a = h_a * DK_A
    n_a = 2 * qk_a + 2 * w_a
    n_b = 2 * w_b + 2 * w_c
    zero_gla = jnp.zeros((bp, h_a, DK_A, DV_A), F32)
    zero_conv = jnp.zeros((bp, CONV_W - 1, w_c), F32)
    zero_lru = jnp.zeros((bp, w_c), F32)
    xp = x_prompt.reshape(bp * tp, d)
    xs = x_sample.reshape(bs * ts, d)
    w_in_t = jnp.swapaxes(w_in, 1, 2)
    mm = _ws_matmul
    st_p, st_s = [], []
    for l in range(depth):
        w = _small_layer_weights(l, w_in_t, w_alpha2, b_alpha, g_onorm, w_spatial, b_spatial, conv_w, conv_b,
                                 w_rgate, b_rgate, w_igate, b_igate, lru_lambda)
        xn_p, alr_p = _rmsnorm(xp, ln1[l], BF16, w_proj=w["w_alr"])
        xn_s, alr_s = _rmsnorm(xs, ln1[l], BF16, w_proj=w["w_alr"])
        hp_p, hp_s = mm([xn_p], [xn_s], w_in_t, l, n_cols=n_a + n_b, transposed=True,
                        w_row_gap=(n_a, GLA_RANK), bm=1024, name="in_proj")
        mix_p, new_p = _mixers(hp_p, alr_p, zero_gla, zero_conv, zero_lru, w, n_seq=bp, seq_len=tp,
                               want_v=False)
        mix_s, new_s = _mixers(hp_s, alr_s, state_gla[l], state_conv[l], state_lru[l], w, n_seq=bs,
                               seq_len=ts, want_v=True)
        st_p.append(new_p)
        st_s.append(new_s)
        xp, xs, xg_p, xg_s, ssq_p, ssq_s = mm(mix_p, mix_s, w_out, l, n_cols=d, res=(xp, xs),
                                              norm_gain=ln2[l].reshape(1, d), name="out_proj")
        hid_p, hid_s, w_down_bf = mm([xg_p], [xg_s], w_up, l, n_cols=w_up.shape[2], act="relu2",
                                     out_dtype=BF16, side_cast=(w_down, l), bm=1024, name="mlp_up")
        xp, xs = _ktiled_matmul(hid_p, hid_s, w_down_bf, (xp, xs), (ssq_p, ssq_s), d, bm=1024, bn=1024,
                                bk=2048, name="mlp_down")
    y_prompt = _rmsnorm(xp, ln_final, F32).reshape(bp, tp, d)
    y_sample = _rmsnorm(xs, ln_final, F32).reshape(bs, ts, d)
    stack = lambda sts, i: jnp.stack([s[i] for s in sts])
    return (y_prompt, y_sample, stack(st_p, 0), stack(st_p, 1), stack(st_p, 2),
            stack(st_s, 0), stack(st_s, 1), stack(st_s, 2),
            jnp.stack([s[3].reshape(bs, ts, w_b) for s in st_s]))
```

```python
import functools

import jax
import jax.numpy as jnp
from jax import lax
from jax.experimental import pallas as pl
from jax.experimental.pallas import tpu as pltpu

F32 = jnp.float32
BF16 = jnp.bfloat16

DV_A = 128
DK_A = 64
GLA_RANK = 16
GLA_TAU = 16.0
CHUNK = 64
GMLP_CHUNK = 128
H_B = 8
H_C = 8
CONV_W = 4
LRU_C = 8.0
EPS = 1e-6

LANES = 128
SUBLANES = 8
V7X_VMEM_BYTES = 64 * 1024 * 1024
VMEM_CAP = V7X_VMEM_BYTES - 6 * 1024 * 1024

GLA_PAIRS_PER_STEP = 8


def _vmem_limit(*block_bytes):
    need = 4 * sum(block_bytes)
    return int(min(max(need, 16 * 1024 * 1024), VMEM_CAP))


def _nbytes(shape, dtype):
    n = 1
    for s in shape:
        n *= s
    return n * jnp.dtype(dtype).itemsize


def _rmsnorm_body(x_ref, g_ref, o_ref):
    x = x_ref[...]
    ms = jnp.mean(jnp.square(x), axis=-1, keepdims=True)
    o_ref[...] = (x * lax.rsqrt(ms + EPS) * g_ref[...]).astype(o_ref.dtype)


def _rmsnorm_proj_body(x_ref, g_ref, w_ref, o_ref, p_ref):
    x = x_ref[...]
    ms = jnp.mean(jnp.square(x), axis=-1, keepdims=True)
    y = (x * lax.rsqrt(ms + EPS) * g_ref[...]).astype(BF16)
    o_ref[...] = y
    p_ref[...] = lax.dot_general(y, w_ref[...].astype(BF16), (((1,), (1,)), ((), ())),
                                 preferred_element_type=F32)


def _rmsnorm(x, g, out_dtype, w_proj=None):
    t, d = x.shape
    tb = min(t, 512)
    grid = (t // tb,)
    x_spec = pl.BlockSpec((tb, d), lambda i: (i, 0))
    g_spec = pl.BlockSpec((1, d), lambda i: (0, 0))
    params = pltpu.CompilerParams(
        dimension_semantics=("parallel",),
        vmem_limit_bytes=_vmem_limit(_nbytes((tb, d), F32), _nbytes((tb, d), out_dtype)))
    if w_proj is None:
        return pl.pallas_call(
            _rmsnorm_body, grid=grid, in_specs=[x_spec, g_spec], out_specs=x_spec,
            out_shape=jax.ShapeDtypeStruct((t, d), out_dtype), compiler_params=params,
            name="rmsnorm")(x, g.reshape(1, d))
    n = w_proj.shape[0]
    return pl.pallas_call(
        _rmsnorm_proj_body, grid=grid,
        in_specs=[x_spec, g_spec, pl.BlockSpec((n, d), lambda i: (0, 0))],
        out_specs=[x_spec, pl.BlockSpec((tb, n), lambda i: (i, 0))],
        out_shape=[jax.ShapeDtypeStruct((t, d), BF16), jax.ShapeDtypeStruct((t, n), F32)],
        compiler_params=params, name="rmsnorm_proj")(x, g.reshape(1, d), w_proj)


def _ktiled_matmul_body(a_p, a_s, w_ref, res_p, res_s, ssq_p, ssq_s, o_p, o_s, *, n_prompt_blocks, width):
    i = pl.program_id(0)

    def run(a_ref, res_ref, ssq_ref, o_ref):
        def scaled_dot():
            ss = ssq_ref[:, 0:1]
            for j in range(1, ssq_ref.shape[1] // LANES):
                ss = ss + ssq_ref[:, j * LANES:j * LANES + 1]
            acc = jnp.dot(a_ref[...], w_ref[...], preferred_element_type=F32)
            return acc * (1.0 / (ss / width + EPS))

        @pl.when(pl.program_id(2) == 0)
        def _():
            o_ref[...] = res_ref[...] + scaled_dot()

        @pl.when(pl.program_id(2) > 0)
        def _():
            o_ref[...] += scaled_dot()

    @pl.when(i < n_prompt_blocks)
    def _():
        run(a_p, res_p, ssq_p, o_p)

    @pl.when(i == n_prompt_blocks)
    def _():
        run(a_s, res_s, ssq_s, o_s)


def _ktiled_matmul(a_p, a_s, w, res, ssq, width, *, bm, bn, bk, name):
    mp, ms = a_p.shape[0], a_s.shape[0]
    k_total, n_cols = w.shape
    assert mp % bm == 0 and n_cols % bn == 0 and k_total % bk == 0
    n_p, n_j, nk = mp // bm, n_cols // bn, k_total // bk

    def p_row(i):
        return jnp.minimum(i, n_p - 1)

    def p_col(i, j):
        return jnp.where(i < n_p, j, n_j - 1)

    def s_col(i, j):
        return jnp.where(i == n_p, j, 0)

    def res_p_idx(i, j, k):
        ahead = (k > 0) & (i < n_p)
        wrap = j + 1 == n_j
        ii = jnp.where(ahead & wrap, i + 1, i)
        jj = jnp.where(ahead, jnp.where(wrap, 0, j + 1), j)
        return p_row(ii), p_col(ii, jj)

    io_p = pl.BlockSpec((bm, bn), lambda i, j, k: (p_row(i), p_col(i, j)))
    io_s = pl.BlockSpec((ms, bn), lambda i, j, k: (0, s_col(i, j)))
    in_specs = [
        pl.BlockSpec((bm, bk), lambda i, j, k: (p_row(i), jnp.where(i < n_p, k, nk - 1))),
        pl.BlockSpec((ms, bk), lambda i, j, k: (0, jnp.where(i == n_p, k, 0))),
        pl.BlockSpec((bk, bn), lambda i, j, k: (k, j)),
        pl.BlockSpec((bm, bn), res_p_idx), io_s,
        pl.BlockSpec((bm, ssq[0].shape[1]), lambda i, j, k: (p_row(i), 0)),
        pl.BlockSpec((ms, ssq[1].shape[1]), lambda i, j, k: (0, 0)),
    ]
    limit = _vmem_limit(_nbytes((bm + ms, bk), BF16), _nbytes((bk, bn), BF16), 2 * _nbytes((bm + ms, bn), F32),
                        _nbytes((bm + ms, ssq[0].shape[1]), F32))
    return pl.pallas_call(
        functools.partial(_ktiled_matmul_body, n_prompt_blocks=n_p, width=width),
        grid=(n_p + 1, n_j, nk), in_specs=in_specs, out_specs=[io_p, io_s],
        out_shape=[jax.ShapeDtypeStruct((mp, n_cols), F32), jax.ShapeDtypeStruct((ms, n_cols), F32)],
        compiler_params=pltpu.CompilerParams(
            dimension_semantics=("arbitrary", "arbitrary", "arbitrary"), vmem_limit_bytes=limit),
        name=name)(a_p, a_s, w, res[0], res[1], ssq[0], ssq[1])


def _ws_matmul_body(*refs, k_bounds, has_res, has_side, has_gain, act, n_prompt_blocks, chunk_rows,
                    transposed):
    refs = list(refs)
    n_lhs = len(k_bounds)
    take = lambda n: [refs.pop(0) for _ in range(n)]
    lhs_p, lhs_s = take(n_lhs), take(n_lhs)
    (wc_ref,) = take(1)
    res_p, res_s = take(2) if has_res else (None, None)
    (side_in,) = take(1) if has_side else (None,)
    (gain_ref,) = take(1) if has_gain else (None,)
    o_p, o_s = take(2)
    (side_out,) = take(1) if has_side else (None,)
    gained_p, gained_s, ssq_p, ssq_s = take(4) if has_gain else (None,) * 4
    (w_scr,) = take(1)
    jj, i = pl.program_id(0), pl.program_id(1)
    slot_load = jj % 2
    slot_use = 1 - slot_load

    def stage():
        r0 = pl.multiple_of(jnp.minimum(i, n_prompt_blocks - 1) * chunk_rows, chunk_rows)
        chunk = wc_ref[0] if len(wc_ref.shape) == 3 else wc_ref[...]
        w_scr[slot_load, pl.ds(r0, chunk_rows), :] = chunk.astype(BF16)

    def run(lhs_refs, res_ref, o_ref, gained_ref, ssq_ref):
        stage()
        acc = None
        for a_ref, (k0, k1) in zip(lhs_refs, k_bounds):
            if transposed:
                part = lax.dot_general(a_ref[...], w_scr[slot_use, :, k0:k1], (((1,), (1,)), ((), ())),
                                       preferred_element_type=F32)
            else:
                part = jnp.dot(a_ref[...], w_scr[slot_use, k0:k1, :], preferred_element_type=F32)
            acc = part if acc is None else acc + part
        if has_res:
            acc = res_ref[...] + acc
        if act == "relu2":
            acc = jnp.square(jnp.maximum(acc, 0.0))
        o_ref[...] = acc.astype(o_ref.dtype)
        if has_gain:
            gained_ref[...] = (acc * gain_ref[...]).astype(gained_ref.dtype)
            ssq_ref[...] = jnp.broadcast_to(jnp.sum(acc * acc, axis=1, keepdims=True), ssq_ref.shape)

    @pl.when(jj == 0)
    def _():
        stage()

    @pl.when((jj > 0) & (i < n_prompt_blocks))
    def _():
        if has_side:
            side_out[...] = side_in[...].astype(side_out.dtype)
        run(lhs_p, res_p, o_p, gained_p, ssq_p)

    @pl.when((jj > 0) & (i == n_prompt_blocks))
    def _():
        run(lhs_s, res_s, o_s, gained_s, ssq_s)


def _ws_matmul(lhs_p, lhs_s, w, layer, *, n_cols, transposed=False, w_row_gap=(0, 0), bm=512, bn=1024,
               res=None, act=None, out_dtype=F32, side_cast=None, norm_gain=None, name="ws_matmul"):
    mp, ms = lhs_p[0].shape[0], lhs_s[0].shape[0]
    k_total = w.shape[2] if transposed else w.shape[1]
    widths = [a.shape[1] for a in lhs_p]
    assert sum(widths) == k_total and mp % bm == 0 and n_cols % bn == 0
    n_p, n_j = mp // bm, n_cols // bn
    bounds, k0 = [], 0
    for wd in widths:
        bounds.append((k0, k0 + wd))
        k0 += wd

    def row_blk(jj, i):
        return jnp.where(jj == 0, 0, jnp.minimum(i, n_p - 1))

    def lhs_row_blk(jj, i):
        return jnp.where((jj == 0) | (i == n_p), 0, i)

    def col_use(jj):
        return jnp.maximum(jj - 1, 0)

    def chunk(jj, i):
        return jnp.where(jj < n_j, jnp.minimum(i, n_p - 1), n_p - 1)

    def col_load(jj):
        return jnp.minimum(jj, n_j - 1)

    if transposed:
        chunk_rows = bn // n_p
        gap_row, gap_rows = w_row_gap
        assert gap_row % bn == 0 and gap_rows % SUBLANES == 0

        def w_row(jj, i):
            row = (col_load(jj) * n_p + chunk(jj, i)) * chunk_rows
            return pl.multiple_of(row + jnp.where(row >= gap_row, gap_rows, 0), SUBLANES)

        w_spec = pl.BlockSpec((pl.Element(1), pl.Element(chunk_rows), pl.Element(k_total)),
                              lambda jj, i: (layer, w_row(jj, i), 0))
        scratch = pltpu.VMEM((2, bn, k_total), BF16)
    else:
        chunk_rows = k_total // n_p
        w_spec = pl.BlockSpec((None, chunk_rows, bn), lambda jj, i: (layer, chunk(jj, i), col_load(jj)))
        scratch = pltpu.VMEM((2, k_total, bn), BF16)
    assert chunk_rows * n_p == (bn if transposed else k_total) and chunk_rows % 16 == 0
    specs_p = [pl.BlockSpec((bm, wd), lambda jj, i: (lhs_row_blk(jj, i), 0)) for wd in widths]
    specs_s = [pl.BlockSpec((ms, wd), lambda jj, i: (0, 0)) for wd in widths]
    io_p = pl.BlockSpec((bm, bn), lambda jj, i: (row_blk(jj, i), col_use(jj)))
    io_s = pl.BlockSpec((ms, bn), lambda jj, i: (0, col_use(jj)))
    in_specs = specs_p + specs_s + [w_spec]
    args = list(lhs_p) + list(lhs_s) + [w]
    if res is not None:
        in_specs += [io_p, io_s]
        args += list(res)
    out_specs = [io_p, io_s]
    out_shape = [jax.ShapeDtypeStruct((mp, n_cols), out_dtype), jax.ShapeDtypeStruct((ms, n_cols), out_dtype)]
    side_bytes = 0
    if side_cast is not None:
        side, side_layer = side_cast
        side_rows, side_cols = side.shape[1], side.shape[2]
        rb = side_rows // (n_j * n_p)
        assert rb * n_j * n_p == side_rows and rb % 16 == 0

        def side_blk(jj, i):
            return jnp.where(jj == 0, 0, (jj - 1) * n_p + jnp.minimum(i, n_p - 1))

        in_specs.append(pl.BlockSpec((None, rb, side_cols), lambda jj, i: (side_layer, side_blk(jj, i), 0)))
        args.append(side)
        out_specs.append(pl.BlockSpec((rb, side_cols), lambda jj, i: (side_blk(jj, i), 0)))
        out_shape.append(jax.ShapeDtypeStruct((side_rows, side_cols), BF16))
        side_bytes = _nbytes((rb, side_cols), F32) + _nbytes((rb, side_cols), BF16)
    if norm_gain is not None:
        in_specs.append(pl.BlockSpec((1, bn), lambda jj, i: (0, col_use(jj))))
        args.append(norm_gain)
        out_specs += [io_p, io_s,
                      pl.BlockSpec((bm, LANES), lambda jj, i: (row_blk(jj, i), col_use(jj))),
                      pl.BlockSpec((ms, LANES), lambda jj, i: (0, col_use(jj)))]
        out_shape += [jax.ShapeDtypeStruct((mp, n_cols), BF16), jax.ShapeDtypeStruct((ms, n_cols), BF16),
                      jax.ShapeDtypeStruct((mp, n_j * LANES), F32), jax.ShapeDtypeStruct((ms, n_j * LANES), F32)]
        side_bytes += _nbytes((bm + ms, bn), BF16) + _nbytes((bm + ms, LANES), F32)
    body = functools.partial(_ws_matmul_body, k_bounds=tuple(bounds), has_res=res is not None,
                             has_side=side_cast is not None, has_gain=norm_gain is not None, act=act,
                             n_prompt_blocks=n_p, chunk_rows=chunk_rows, transposed=transposed)
    n_io = 2 if res is not None else 1
    pipelined = (_nbytes((bm + ms, k_total), BF16) + _nbytes((k_total // n_p, bn), F32)
                 + n_io * _nbytes((bm + ms, bn), F32) + side_bytes)
    limit = min(VMEM_CAP, 2 * pipelined + _nbytes((2, k_total, bn), BF16) + 2 * _nbytes((bm, bn), F32))
    return pl.pallas_call(
        body, grid=(n_j + 1, n_p + 1), in_specs=in_specs, out_specs=out_specs, out_shape=out_shape,
        scratch_shapes=[scratch],
        compiler_params=pltpu.CompilerParams(
            dimension_semantics=("arbitrary", "arbitrary"), vmem_limit_bytes=int(limit)),
        name=name)(*args)


def _linear_scan_rows(a, b, carry):
    t, lanes = b.shape
    g = t // SUBLANES
    b3 = b.reshape(g, SUBLANES, lanes)
    a3 = None if a is None else a.reshape(g, SUBLANES, lanes)
    sub = lax.broadcasted_iota(jnp.int32, b3.shape, 1)
    s = 1
    while s < SUBLANES:
        keep = sub >= s
        b_prev = jnp.where(keep, pltpu.roll(b3, s, 1), 0.0)
        if a3 is None:
            b3 = b3 + b_prev
        else:
            b3 = a3 * b_prev + b3
            a3 = a3 * jnp.where(keep, pltpu.roll(a3, s, 1), 1.0)
        s *= 2
    outs = []
    for j in range(g):
        h = b3[j] + carry if a3 is None else b3[j] + a3[j] * carry
        outs.append(h)
        carry = h[SUBLANES - 1:, :]
    return jnp.concatenate(outs, axis=0), carry


def _softplus(x):
    return jnp.maximum(x, 0.0) + jnp.log1p(jnp.exp(-jnp.abs(x)))


def _gla_body(q_ref, k_ref, v_ref, g_ref, alr_ref, wa_ref, ba_ref, gn_ref, s0_ref,
              o_ref, s_out_ref, st_ref, *, chunk, n_chunks, n_pairs):
    n = pl.program_id(2)

    @pl.when(n == 0)
    def _():
        for p in range(n_pairs):
            st_ref[p] = s0_ref[0, p].T

    c = chunk
    qw, vw = 2 * DK_A, 2 * DV_A
    lane = lax.broadcasted_iota(jnp.int32, (c, qw), 1)
    head0 = lane < DK_A
    row2 = lax.broadcasted_iota(jnp.int32, (2 * c, 2 * c), 0)
    col2 = lax.broadcasted_iota(jnp.int32, (2 * c, 2 * c), 1)
    causal = ((row2 >= c) == (col2 >= c)) & (row2 >= col2)
    nt = (((1,), (1,)), ((), ()))
    tn = (((0,), (0,)), ((), ()))

    def stack_heads(z):
        return jnp.concatenate([jnp.where(head0, z, 0.0), jnp.where(head0, 0.0, z)], axis=0).astype(BF16)

    for ci in range(n_chunks):
        rows = pl.ds(ci * c, c)
        z = jnp.dot(alr_ref[rows, :].astype(BF16), wa_ref[...], preferred_element_type=F32) + ba_ref[...]
        log_alpha = -(jnp.maximum(-z, 0.0) + jnp.log(1.0 + jnp.exp(-jnp.abs(z)))) / GLA_TAU
        b, b_end = _linear_scan_rows(None, log_alpha, jnp.zeros((1, log_alpha.shape[1]), F32))
        q = q_ref[rows, :] * (DK_A ** -0.5)
        k = k_ref[rows, :]
        q_dec = q * jnp.exp(b)
        k_inv = k * jnp.exp(-b)
        k_end = k * jnp.exp(b_end - b)
        decay = jnp.exp(b_end)
        for p in range(n_pairs):
            ql = slice(p * qw, (p + 1) * qw)
            q_t, k_t, k_e = stack_heads(q_dec[:, ql]), stack_heads(k_inv[:, ql]), stack_heads(k_end[:, ql])
            v0 = v_ref[rows, pl.ds(p * vw, DV_A)]
            v1 = v_ref[rows, pl.ds(p * vw + DV_A, DV_A)]
            v_s = jnp.concatenate([v0, v1], axis=0).astype(BF16)
            att = lax.dot_general(q_t, k_t, nt, preferred_element_type=F32)
            att = jnp.where(causal, att, 0.0).astype(BF16)
            st = st_ref[p]
            o = (jnp.dot(att, v_s, preferred_element_type=F32)
                 + lax.dot_general(q_t, st.astype(BF16), nt, preferred_element_type=F32))
            st_ref[p] = decay[:, ql] * st + lax.dot_general(v_s, k_e, tn, preferred_element_type=F32)
            ms = jnp.mean(jnp.square(o), axis=-1, keepdims=True)
            o = (o * lax.rsqrt(ms + EPS) * gn_ref[...])
            g0 = g_ref[rows, pl.ds(p * vw, DV_A)]
            g1 = g_ref[rows, pl.ds(p * vw + DV_A, DV_A)]
            o_ref[rows, pl.ds(p * vw, DV_A)] = (o[:c] * (g0 * jax.nn.sigmoid(g0))).astype(o_ref.dtype)
            o_ref[rows, pl.ds(p * vw + DV_A, DV_A)] = (o[c:] * (g1 * jax.nn.sigmoid(g1))).astype(o_ref.dtype)

    @pl.when(n == pl.num_programs(2) - 1)
    def _():
        for p in range(n_pairs):
            s_out_ref[0, p] = st_ref[p].T


def _gla(hproj, alr, wa, ba, gn, s0, *, n_seq, seq_len, col_q, col_k, col_v, col_g):
    t_total = hproj.shape[0]
    h_a = s0.shape[1]
    pairs = h_a // 2
    pps = GLA_PAIRS_PER_STEP
    assert pairs % pps == 0
    chunk = min(CHUNK, seq_len)
    n_chunks = min(8, seq_len // chunk)
    tb = chunk * n_chunks
    nb = seq_len // tb
    qw, vw = 2 * DK_A * pps, 2 * DV_A * pps
    assert col_q % qw == 0 and col_k % qw == 0 and col_v % vw == 0 and col_g % vw == 0
    cq, ck, cv, cg = col_q // qw, col_k // qw, col_v // vw, col_g // vw
    row = lambda s, n: s * nb + n
    in_specs = [
        pl.BlockSpec((tb, qw), lambda s, p, n: (row(s, n), cq + p)),
        pl.BlockSpec((tb, qw), lambda s, p, n: (row(s, n), ck + p)),
        pl.BlockSpec((tb, vw), lambda s, p, n: (row(s, n), cv + p)),
        pl.BlockSpec((tb, vw), lambda s, p, n: (row(s, n), cg + p)),
        pl.BlockSpec((tb, LANES), lambda s, p, n: (row(s, n), 0)),
        pl.BlockSpec((LANES, qw), lambda s, p, n: (0, p)),
        pl.BlockSpec((1, qw), lambda s, p, n: (0, p)),
        pl.BlockSpec((1, DV_A), lambda s, p, n: (0, 0)),
        pl.BlockSpec((1, pps, 2 * DK_A, DV_A), lambda s, p, n: (s, p, 0, 0)),
    ]
    out_specs = [
        pl.BlockSpec((tb, vw), lambda s, p, n: (row(s, n), p)),
        pl.BlockSpec((1, pps, 2 * DK_A, DV_A), lambda s, p, n: (s, p, 0, 0)),
    ]
    o_a, s_new = pl.pallas_call(
        functools.partial(_gla_body, chunk=chunk, n_chunks=n_chunks, n_pairs=pps),
        grid=(n_seq, pairs // pps, nb), in_specs=in_specs, out_specs=out_specs,
        out_shape=[jax.ShapeDtypeStruct((t_total, h_a * DV_A), BF16),
                   jax.ShapeDtypeStruct((n_seq, pairs, 2 * DK_A, DV_A), F32)],
        scratch_shapes=[pltpu.VMEM((pps, DV_A, 2 * DK_A), F32)],
        compiler_params=pltpu.CompilerParams(
            dimension_semantics=("parallel", "parallel", "arbitrary"),
            vmem_limit_bytes=_vmem_limit(_nbytes((tb, 2 * qw + 2 * vw + LANES), F32), _nbytes((tb, vw), BF16))),
        name="gla")(hproj, hproj, hproj, hproj, alr, wa, ba, gn, s0.reshape(n_seq, pairs, 2 * DK_A, DV_A))
    return o_a, s_new.reshape(n_seq, h_a, DK_A, DV_A)


def _gmlp_body(u_ref, v_ref, w_ref, b_ref, o_ref, *vg_refs, c, dh):
    tril = (lax.broadcasted_iota(jnp.int32, (c, c), 0) >= lax.broadcasted_iota(jnp.int32, (c, c), 1))
    w_tril = [jnp.where(tril, w_ref[h, :c, :c], 0.0).astype(BF16) for h in range(w_ref.shape[0])]
    for ci in range(u_ref.shape[0] // c):
        rows = pl.ds(ci * c, c)
        vg = jax.nn.gelu(v_ref[rows, :])
        if vg_refs:
            vg_refs[0][rows, :] = vg
        ug = jax.nn.gelu(u_ref[rows, :])
        for h, w in enumerate(w_tril):
            cols = pl.ds(h * dh, dh)
            mixed = (jnp.dot(w, vg[:, h * dh:(h + 1) * dh].astype(BF16), preferred_element_type=F32)
                     + b_ref[:c, cols])
            o_ref[rows, cols] = (ug[:, h * dh:(h + 1) * dh] * mixed).astype(o_ref.dtype)


def _gmlp(hproj, w_sp, b_full, *, seq_len, col_u, col_v, want_v):
    t_total = hproj.shape[0]
    n_h, gc, _ = w_sp.shape
    w_b = b_full.shape[1]
    dh = w_b // n_h
    c = min(gc, seq_len)
    tb = c * min(4, seq_len // c)
    blk = lambda col: pl.BlockSpec((tb, w_b), lambda r: (r, col // w_b))
    out_specs = [pl.BlockSpec((tb, w_b), lambda r: (r, 0))]
    out_shape = [jax.ShapeDtypeStruct((t_total, w_b), BF16)]
    if want_v:
        out_specs.append(pl.BlockSpec((tb, w_b), lambda r: (r, 0)))
        out_shape.append(jax.ShapeDtypeStruct((t_total, w_b), F32))
    outs = pl.pallas_call(
        functools.partial(_gmlp_body, c=c, dh=dh), grid=(t_total // tb,),
        in_specs=[blk(col_u), blk(col_v),
                  pl.BlockSpec((n_h, gc, gc), lambda r: (0, 0, 0)),
                  pl.BlockSpec((gc, w_b), lambda r: (0, 0))],
        out_specs=out_specs, out_shape=out_shape,
        compiler_params=pltpu.CompilerParams(
            dimension_semantics=("parallel",),
            vmem_limit_bytes=_vmem_limit(4 * _nbytes((tb, w_b), F32), _nbytes((n_h, gc, gc), F32))),
        name="gmlp")(hproj, hproj, w_sp, b_full)
    return (outs[0], outs[1]) if want_v else (outs[0], None)


def _lru_body(x_ref, gate_ref, cw_ref, cb_ref, wr_ref, br_ref, wi_ref, bi_ref, lam_ref, h0_ref, tail0_ref,
              o_ref, h_ref, tail_ref, *, dh):
    n = pl.program_id(1)

    @pl.when(n == 0)
    def _():
        h_ref[...] = h0_ref[...]
        tail_ref[...] = tail0_ref[...]

    tb = x_ref.shape[0]
    row8 = lax.broadcasted_iota(jnp.int32, (SUBLANES, dh), 0)
    for hb in range(x_ref.shape[1] // dh):
        cols = pl.ds(hb * dh, dh)
        x = x_ref[:, cols]
        tail = tail_ref[0, :, cols]

        def delayed(j):
            rolled = pltpu.roll(x, j, 0)
            head = jnp.where(row8 >= j, rolled[:SUBLANES], pltpu.roll(tail, j, 0))
            return jnp.concatenate([head, rolled[SUBLANES:]], axis=0)

        y = cb_ref[:, cols] + delayed(CONV_W - 1) * cw_ref[0:1, cols]
        for j in range(1, CONV_W - 1):
            y = y + delayed(CONV_W - 1 - j) * cw_ref[j:j + 1, cols]
        y = y + x * cw_ref[CONV_W - 1:CONV_W, cols]
        tail_ref[0, :, cols] = x[tb - SUBLANES:]

        yb = y.astype(BF16)
        r = jax.nn.sigmoid(jnp.dot(yb, wr_ref[hb].astype(BF16), preferred_element_type=F32) + br_ref[:, cols])
        ig = jax.nn.sigmoid(jnp.dot(yb, wi_ref[hb].astype(BF16), preferred_element_type=F32) + bi_ref[:, cols])
        log_a = r * (-LRU_C * _softplus(-lam_ref[:, cols]))
        a = jnp.exp(log_a)
        gain2 = -jnp.tanh(log_a) * (a * a + 1.0)
        bt = jnp.where(gain2 > 0.0, gain2 * lax.rsqrt(gain2), 0.0) * (ig * y)
        hseq, h_last = _linear_scan_rows(a, bt, h_ref[0, :, cols])
        h_ref[0, :, cols] = h_last
        o_ref[:, cols] = (hseq * jax.nn.gelu(gate_ref[:, cols])).astype(o_ref.dtype)


def _lru(hproj, cw, cb, wr, br, wi, bi, lam, h0, tail0, *, n_seq, seq_len, col_x, col_gate):
    t_total = hproj.shape[0]
    w_c = cw.shape[1]
    n_h, dh, _ = wr.shape
    tb = min(seq_len, 256)
    nb = seq_len // tb
    vec = lambda rows: pl.BlockSpec((rows, w_c), lambda s, n: (0, 0))
    wspec = pl.BlockSpec((n_h, dh, dh), lambda s, n: (0, 0, 0))
    hspec = pl.BlockSpec((1, 1, w_c), lambda s, n: (s, 0, 0))
    tspec = pl.BlockSpec((1, SUBLANES, w_c), lambda s, n: (s, 0, 0))
    o_c, h_new, tail = pl.pallas_call(
        functools.partial(_lru_body, dh=dh), grid=(n_seq, nb),
        in_specs=[pl.BlockSpec((tb, w_c), lambda s, n: (s * nb + n, col_x // w_c)),
                  pl.BlockSpec((tb, w_c), lambda s, n: (s * nb + n, col_gate // w_c)),
                  vec(CONV_W), vec(1), wspec, vec(1), wspec, vec(1), vec(1), hspec, tspec],
        out_specs=[pl.BlockSpec((tb, w_c), lambda s, n: (s * nb + n, 0)), hspec, tspec],
        out_shape=[jax.ShapeDtypeStruct((t_total, w_c), BF16),
                   jax.ShapeDtypeStruct((n_seq, 1, w_c), F32),
                   jax.ShapeDtypeStruct((n_seq, SUBLANES, w_c), F32)],
        compiler_params=pltpu.CompilerParams(
            dimension_semantics=("parallel", "arbitrary"),
            vmem_limit_bytes=_vmem_limit(4 * _nbytes((tb, w_c), F32))),
        name="conv_rglru")(hproj, hproj, cw, cb.reshape(1, w_c), wr, br.reshape(1, w_c), wi,
                           bi.reshape(1, w_c), lam.reshape(1, w_c), h0.reshape(n_seq, 1, w_c), tail0)
    return o_c, h_new.reshape(n_seq, w_c), tail


def _mixers(hproj, alr, s_gla, s_conv, s_lru, w, *, n_seq, seq_len, want_v):
    w_a = s_gla.shape[1] * DV_A
    qk_a = s_gla.shape[1] * DK_A
    w_b = w_c = s_lru.shape[1]
    col_u = 2 * qk_a + 2 * w_a
    o_a, s_gla_new = _gla(hproj, alr, w["w_alpha2"], w["b_alpha"], w["g_onorm"], s_gla, n_seq=n_seq,
                          seq_len=seq_len, col_q=0, col_k=qk_a, col_v=2 * qk_a, col_g=2 * qk_a + w_a)
    o_b, vg = _gmlp(hproj, w["w_spatial"], w["b_spatial"], seq_len=seq_len, col_u=col_u, col_v=col_u + w_b,
                    want_v=want_v)
    tail0 = jnp.pad(s_conv, ((0, 0), (SUBLANES - (CONV_W - 1), 0), (0, 0)))
    o_c, s_lru_new, tail = _lru(hproj, w["conv_w"], w["conv_b"], w["w_rgate"], w["b_rgate"], w["w_igate"],
                                w["b_igate"], w["lru_lambda"], s_lru, tail0, n_seq=n_seq, seq_len=seq_len,
                                col_x=col_u + 2 * w_b, col_gate=col_u + 2 * w_b + w_c)
    return [o_a, o_b, o_c], (s_gla_new, tail[:, SUBLANES - (CONV_W - 1):], s_lru_new, vg)


def _small_layer_weights(l, w_in_t, w_alpha2, b_alpha, g_onorm, w_spatial, b_spatial, conv_w, conv_b,
                         w_rgate, b_rgate, w_igate, b_igate, lru_lambda):
    d = w_in_t.shape[2]
    w_a, w_b = d // 2, d // 4
    qk_a = (w_a // DV_A) * DK_A
    c_alr = 2 * qk_a + 2 * w_a
    w_alr = jnp.pad(w_in_t[l, c_alr:c_alr + GLA_RANK, :], ((0, LANES - GLA_RANK), (0, 0)))
    w_alpha2_p = jnp.pad(w_alpha2[l], ((0, LANES - GLA_RANK), (0, 0))).astype(BF16)
    b_full = jnp.repeat(b_spatial[l].T, w_b // H_B, axis=1)
    return dict(
        w_alr=w_alr, w_alpha2=w_alpha2_p, b_alpha=b_alpha[l].reshape(1, qk_a),
        g_onorm=g_onorm[l].reshape(1, DV_A), w_spatial=w_spatial[l], b_spatial=b_full,
        conv_w=conv_w[l], conv_b=conv_b[l], w_rgate=w_rgate[l], b_rgate=b_rgate[l],
        w_igate=w_igate[l], b_igate=b_igate[l], lru_lambda=lru_lambda[l])


def kernel(x_prompt, x_sample, state_gla, state_conv, state_lru, ln1, w_in, w_alpha2, b_alpha, g_onorm, w_spatial, b_spatial, conv_w, conv_b, w_rgate, b_rgate, w_igate, b_igate, lru_lambda, w_out, ln2, w_up, w_down, ln_final):
    bp, tp, d = x_prompt.shape
    bs, ts, _ = x_sample.shape
    depth = w_in.shape[0]
    w_a, w_b, w_c = d // 2, d // 4, d // 4
    h_a = w_a // DV_A
    qk_a = h_a * DK_A
    n_a = 2 * qk_a + 2 * w_a
    n_b = 2 * w_b + 2 * w_c
    zero_gla = jnp.zeros((bp, h_a, DK_A, DV_A), F32)
    zero_conv = jnp.zeros((bp, CONV_W - 1, w_c), F32)
    zero_lru = jnp.zeros((bp, w_c), F32)
    xp = x_prompt.reshape(bp * tp, d)
    xs = x_sample.reshape(bs * ts, d)
    w_in_t = jnp.swapaxes(w_in, 1, 2)
    mm = _ws_matmul
    st_p, st_s = [], []
    for l in range(depth):
        w = _small_layer_weights(l, w_in_t, w_alpha2, b_alpha, g_onorm, w_spatial, b_spatial, conv_w, conv_b,
                                 w_rgate, b_rgate, w_igate, b_igate, lru_lambda)
        xn_p, alr_p = _rmsnorm(xp, ln1[l], BF16, w_proj=w["w_alr"])
        xn_s, alr_s = _rmsnorm(xs, ln1[l], BF16, w_proj=w["w_alr"])
        hp_p, hp_s = mm([xn_p], [xn_s], w_in_t, l, n_cols=n_a + n_b, transposed=True,
                        w_row_gap=(n_a, GLA_RANK), bm=1024, name="in_proj")
        mix_p, new_p = _mixers(hp_p, alr_p, zero_gla, zero_conv, zero_lru, w, n_seq=bp, seq_len=tp,
                               want_v=False)
        mix_s, new_s = _mixers(hp_s, alr_s, state_gla[l], state_conv[l], state_lru[l], w, n_seq=bs,
                               seq_len=ts, want_v=True)
        st_p.append(new_p)
        st_s.append(new_s)
        xp, xs, xg_p, xg_s, ssq_p, ssq_s = mm(mix_p, mix_s, w_out, l, n_cols=d, res=(xp, xs),
                                              norm_gain=ln2[l].reshape(1, d), name="out_proj")
        hid_p, hid_s, w_down_bf = mm([xg_p], [xg_s], w_up, l, n_cols=w_up.shape[2], act="relu2",
                                     out_dtype=BF16, side_cast=(w_down, l), bm=1024, name="mlp_up")
        xp, xs = _ktiled_matmul(hid_p, hid_s, w_down_bf, (xp, xs), (ssq_p, ssq_s), d, bm=1024, bn=1024,
                                bk=2048, name="mlp_down")
    y_prompt = _rmsnorm(xp, ln_final, F32).reshape(bp, tp, d)
    y_sample = _rmsnorm(xs, ln_final, F32).reshape(bs, ts, d)
    stack = lambda sts, i: jnp.stack([s[i] for s in sts])
    return (y_prompt, y_sample, stack(st_p, 0), stack(st_p, 1), stack(st_p, 2),
            stack(st_s, 0), stack(st_s, 1), stack(st_s, 2),
            jnp.stack([s[3].reshape(bs, ts, w_b) for s in st_s]))
```

```python
import functools

import jax
import jax.numpy as jnp
from jax import lax
from jax.experimental import pallas as pl
from jax.experimental.pallas import tpu as pltpu

F32 = jnp.float32
BF16 = jnp.bfloat16

DV_A = 128
DK_A = 64
GLA_RANK = 16
GLA_TAU = 16.0
CHUNK = 64
GMLP_CHUNK = 128
H_B = 8
H_C = 8
CONV_W = 4
LRU_C = 8.0
EPS = 1e-6

LANES = 128
SUBLANES = 8
V7X_VMEM_BYTES = 64 * 1024 * 1024
VMEM_CAP = V7X_VMEM_BYTES - 6 * 1024 * 1024

GLA_PAIRS_PER_STEP = 8


def _vmem_limit(*block_bytes):
    need = 4 * sum(block_bytes)
    return int(min(max(need, 16 * 1024 * 1024), VMEM_CAP))


def _nbytes(shape, dtype):
    n = 1
    for s in shape:
        n *= s
    return n * jnp.dtype(dtype).itemsize


def _rmsnorm_body(x_ref, g_ref, o_ref):
    x = x_ref[...]
    ms = jnp.mean(jnp.square(x), axis=-1, keepdims=True)
    o_ref[...] = (x * lax.rsqrt(ms + EPS) * g_ref[...]).astype(o_ref.dtype)


def _rmsnorm_proj_body(x_ref, g_ref, w_ref, o_ref, p_ref):
    x = x_ref[...]
    ms = jnp.mean(jnp.square(x), axis=-1, keepdims=True)
    y = (x * lax.rsqrt(ms + EPS) * g_ref[...]).astype(BF16)
    o_ref[...] = y
    p_ref[...] = lax.dot_general(y, w_ref[...].astype(BF16), (((1,), (1,)), ((), ())),
                                 preferred_element_type=F32)


def _rmsnorm(x, g, out_dtype, w_proj=None):
    t, d = x.shape
    tb = min(t, 512)
    grid = (t // tb,)
    x_spec = pl.BlockSpec((tb, d), lambda i: (i, 0))
    g_spec = pl.BlockSpec((1, d), lambda i: (0, 0))
    params = pltpu.CompilerParams(
        dimension_semantics=("parallel",),
        vmem_limit_bytes=_vmem_limit(_nbytes((tb, d), F32), _nbytes((tb, d), out_dtype)))
    if w_proj is None:
        return pl.pallas_call(
            _rmsnorm_body, grid=grid, in_specs=[x_spec, g_spec], out_specs=x_spec,
            out_shape=jax.ShapeDtypeStruct((t, d), out_dtype), compiler_params=params,
            name="rmsnorm")(x, g.reshape(1, d))
    n = w_proj.shape[0]
    return pl.pallas_call(
        _rmsnorm_proj_body, grid=grid,
        in_specs=[x_spec, g_spec, pl.BlockSpec((n, d), lambda i: (0, 0))],
        out_specs=[x_spec, pl.BlockSpec((tb, n), lambda i: (i, 0))],
        out_shape=[jax.ShapeDtypeStruct((t, d), BF16), jax.ShapeDtypeStruct((t, n), F32)],
        compiler_params=params, name="rmsnorm_proj")(x, g.reshape(1, d), w_proj)


def _ktiled_matmul_body(a_p, a_s, w_ref, res_p, res_s, ssq_p, ssq_s, o_p, o_s, *, n_prompt_blocks, width):
    i = pl.program_id(0)

    def run(a_ref, res_ref, ssq_ref, o_ref):
        def scaled_dot():
            ss = ssq_ref[:, 0:1]
            for j in range(1, ssq_ref.shape[1] // LANES):
                ss = ss + ssq_ref[:, j * LANES:j * LANES + 1]
            acc = jnp.dot(a_ref[...], w_ref[...], preferred_element_type=F32)
            return acc * (1.0 / (ss / width + EPS))

        @pl.when(pl.program_id(2) == 0)
        def _():
            o_ref[...] = res_ref[...] + scaled_dot()

        @pl.when(pl.program_id(2) > 0)
        def _():
            o_ref[...] += scaled_dot()

    @pl.when(i < n_prompt_blocks)
    def _():
        run(a_p, res_p, ssq_p, o_p)

    @pl.when(i == n_prompt_blocks)
    def _():
        run(a_s, res_s, ssq_s, o_s)


def _ktiled_matmul(a_p, a_s, w, res, ssq, width, *, bm, bn, bk, name):
    mp, ms = a_p.shape[0], a_s.shape[0]
    k_total, n_cols = w.shape[1], w.shape[0] * w.shape[2]
    assert mp % bm == 0 and w.shape[2] == bn and k_total % bk == 0
    n_p, n_j, nk = mp // bm, n_cols // bn, k_total // bk

    def p_row(i):
        return jnp.minimum(i, n_p - 1)

    def p_col(i, j):
        return jnp.where(i < n_p, j, n_j - 1)

    def s_col(i, j):
        return jnp.where(i == n_p, j, 0)

    io_p = pl.BlockSpec((bm, bn), lambda i, j, k: (p_row(i), p_col(i, j)))
    io_s = pl.BlockSpec((ms, bn), lambda i, j, k: (0, s_col(i, j)))
    in_specs = [
        pl.BlockSpec((bm, bk), lambda i, j, k: (p_row(i), jnp.where(i < n_p, k, nk - 1))),
        pl.BlockSpec((ms, bk), lambda i, j, k: (0, jnp.where(i == n_p, k, 0))),
        pl.BlockSpec((None, bk, bn), lambda i, j, k: (j, k, 0)),
        io_p, io_s,
        pl.BlockSpec((bm, ssq[0].shape[1]), lambda i, j, k: (p_row(i), 0)),
        pl.BlockSpec((ms, ssq[1].shape[1]), lambda i, j, k: (0, 0)),
    ]
    limit = _vmem_limit(_nbytes((bm + ms, bk), BF16), _nbytes((bk, bn), BF16), 2 * _nbytes((bm + ms, bn), F32),
                        _nbytes((bm + ms, ssq[0].shape[1]), F32))
    return pl.pallas_call(
        functools.partial(_ktiled_matmul_body, n_prompt_blocks=n_p, width=width),
        grid=(n_p + 1, n_j, nk), in_specs=in_specs, out_specs=[io_p, io_s],
        out_shape=[jax.ShapeDtypeStruct((mp, n_cols), F32), jax.ShapeDtypeStruct((ms, n_cols), F32)],
        compiler_params=pltpu.CompilerParams(
            dimension_semantics=("arbitrary", "arbitrary", "arbitrary"), vmem_limit_bytes=limit),
        name=name)(a_p, a_s, w, res[0], res[1], ssq[0], ssq[1])


def _ws_matmul_body(*refs, k_bounds, has_res, has_side, has_gain, act, n_prompt_blocks, chunk_rows,
                    transposed):
    refs = list(refs)
    n_lhs = len(k_bounds)
    take = lambda n: [refs.pop(0) for _ in range(n)]
    lhs_p, lhs_s = take(n_lhs), take(n_lhs)
    (wc_ref,) = take(1)
    res_p, res_s = take(2) if has_res else (None, None)
    (side_in,) = take(1) if has_side else (None,)
    (gain_ref,) = take(1) if has_gain else (None,)
    o_p, o_s = take(2)
    (side_out,) = take(1) if has_side else (None,)
    gained_p, gained_s, ssq_p, ssq_s = take(4) if has_gain else (None,) * 4
    (w_scr,) = take(1)
    jj, i = pl.program_id(0), pl.program_id(1)
    slot_load = jj % 2
    slot_use = 1 - slot_load

    def stage():
        r0 = pl.multiple_of(jnp.minimum(i, n_prompt_blocks - 1) * chunk_rows, chunk_rows)
        chunk = wc_ref[0] if len(wc_ref.shape) == 3 else wc_ref[...]
        w_scr[slot_load, pl.ds(r0, chunk_rows), :] = chunk.astype(BF16)

    def run(lhs_refs, res_ref, o_ref, gained_ref, ssq_ref):
        stage()
        acc = None
        for a_ref, (k0, k1) in zip(lhs_refs, k_bounds):
            if transposed:
                part = lax.dot_general(a_ref[...], w_scr[slot_use, :, k0:k1], (((1,), (1,)), ((), ())),
                                       preferred_element_type=F32)
            else:
                part = jnp.dot(a_ref[...], w_scr[slot_use, k0:k1, :], preferred_element_type=F32)
            acc = part if acc is None else acc + part
        if has_res:
            acc = res_ref[...] + acc
        if act == "relu2":
            acc = jnp.square(jnp.maximum(acc, 0.0))
        o_ref[...] = acc.astype(o_ref.dtype)
        if has_gain:
            gained_ref[...] = (acc * gain_ref[...]).astype(gained_ref.dtype)
            ssq_ref[...] = jnp.broadcast_to(jnp.sum(acc * acc, axis=1, keepdims=True), ssq_ref.shape)

    @pl.when(jj == 0)
    def _():
        stage()

    @pl.when((jj > 0) & (i < n_prompt_blocks))
    def _():
        if has_side:
            cw = side_out.shape[2]
            for cb in range(side_out.shape[0]):
                side_out[cb] = side_in[:, cb * cw:(cb + 1) * cw].astype(side_out.dtype)
        run(lhs_p, res_p, o_p, gained_p, ssq_p)

    @pl.when((jj > 0) & (i == n_prompt_blocks))
    def _():
        run(lhs_s, res_s, o_s, gained_s, ssq_s)


def _ws_matmul(lhs_p, lhs_s, w, layer, *, n_cols, transposed=False, w_row_gap=(0, 0), bm=512, bn=1024,
               res=None, act=None, out_dtype=F32, side_cast=None, norm_gain=None, name="ws_matmul"):
    mp, ms = lhs_p[0].shape[0], lhs_s[0].shape[0]
    k_total = w.shape[2] if transposed else w.shape[1]
    widths = [a.shape[1] for a in lhs_p]
    assert sum(widths) == k_total and mp % bm == 0 and n_cols % bn == 0
    n_p, n_j = mp // bm, n_cols // bn
    bounds, k0 = [], 0
    for wd in widths:
        bounds.append((k0, k0 + wd))
        k0 += wd

    def row_blk(jj, i):
        return jnp.where(jj == 0, 0, jnp.minimum(i, n_p - 1))

    def lhs_row_blk(jj, i):
        return jnp.where((jj == 0) | (i == n_p), 0, i)

    def col_use(jj):
        return jnp.maximum(jj - 1, 0)

    def chunk(jj, i):
        return jnp.where(jj < n_j, jnp.minimum(i, n_p - 1), n_p - 1)

    def col_load(jj):
        return jnp.minimum(jj, n_j - 1)

    if transposed:
        chunk_rows = bn // n_p
        gap_row, gap_rows = w_row_gap
        assert gap_row % bn == 0 and gap_rows % SUBLANES == 0

        def w_row(jj, i):
            row = (col_load(jj) * n_p + chunk(jj, i)) * chunk_rows
            return pl.multiple_of(row + jnp.where(row >= gap_row, gap_rows, 0), SUBLANES)

        w_spec = pl.BlockSpec((pl.Element(1), pl.Element(chunk_rows), pl.Element(k_total)),
                              lambda jj, i: (layer, w_row(jj, i), 0))
        scratch = pltpu.VMEM((2, bn, k_total), BF16)
    else:
        chunk_rows = k_total // n_p
        w_spec = pl.BlockSpec((None, chunk_rows, bn), lambda jj, i: (layer, chunk(jj, i), col_load(jj)))
        scratch = pltpu.VMEM((2, k_total, bn), BF16)
    assert chunk_rows * n_p == (bn if transposed else k_total) and chunk_rows % 16 == 0
    specs_p = [pl.BlockSpec((bm, wd), lambda jj, i: (lhs_row_blk(jj, i), 0)) for wd in widths]
    specs_s = [pl.BlockSpec((ms, wd), lambda jj, i: (0, 0)) for wd in widths]
    io_p = pl.BlockSpec((bm, bn), lambda jj, i: (row_blk(jj, i), col_use(jj)))
    io_s = pl.BlockSpec((ms, bn), lambda jj, i: (0, col_use(jj)))
    in_specs = specs_p + specs_s + [w_spec]
    args = list(lhs_p) + list(lhs_s) + [w]
    if res is not None:
        in_specs += [io_p, io_s]
        args += list(res)
    out_specs = [io_p, io_s]
    out_shape = [jax.ShapeDtypeStruct((mp, n_cols), out_dtype), jax.ShapeDtypeStruct((ms, n_cols), out_dtype)]
    side_bytes = 0
    if side_cast is not None:
        side, side_layer, side_cw = side_cast
        side_rows, side_cols = side.shape[1], side.shape[2]
        rb = side_rows // (n_j * n_p)
        assert rb * n_j * n_p == side_rows and rb % 16 == 0 and side_cols % side_cw == 0

        def side_blk(jj, i):
            return jnp.where(jj == 0, 0, (jj - 1) * n_p + jnp.minimum(i, n_p - 1))

        in_specs.append(pl.BlockSpec((None, rb, side_cols), lambda jj, i: (side_layer, side_blk(jj, i), 0)))
        args.append(side)
        out_specs.append(pl.BlockSpec((side_cols // side_cw, rb, side_cw), lambda jj, i: (0, side_blk(jj, i), 0)))
        out_shape.append(jax.ShapeDtypeStruct((side_cols // side_cw, side_rows, side_cw), BF16))
        side_bytes = _nbytes((rb, side_cols), F32) + _nbytes((rb, side_cols), BF16)
    if norm_gain is not None:
        in_specs.append(pl.BlockSpec((1, bn), lambda jj, i: (0, col_use(jj))))
        args.append(norm_gain)
        out_specs += [io_p, io_s,
                      pl.BlockSpec((bm, LANES), lambda jj, i: (row_blk(jj, i), col_use(jj))),
                      pl.BlockSpec((ms, LANES), lambda jj, i: (0, col_use(jj)))]
        out_shape += [jax.ShapeDtypeStruct((mp, n_cols), BF16), jax.ShapeDtypeStruct((ms, n_cols), BF16),
                      jax.ShapeDtypeStruct((mp, n_j * LANES), F32), jax.ShapeDtypeStruct((ms, n_j * LANES), F32)]
        side_bytes += _nbytes((bm + ms, bn), BF16) + _nbytes((bm + ms, LANES), F32)
    body = functools.partial(_ws_matmul_body, k_bounds=tuple(bounds), has_res=res is not None,
                             has_side=side_cast is not None, has_gain=norm_gain is not None, act=act,
                             n_prompt_blocks=n_p, chunk_rows=chunk_rows, transposed=transposed)
    n_io = 2 if res is not None else 1
    pipelined = (_nbytes((bm + ms, k_total), BF16) + _nbytes((k_total // n_p, bn), F32)
                 + n_io * _nbytes((bm + ms, bn), F32) + side_bytes)
    limit = min(VMEM_CAP, 2 * pipelined + _nbytes((2, k_total, bn), BF16) + 2 * _nbytes((bm, bn), F32))
    return pl.pallas_call(
        body, grid=(n_j + 1, n_p + 1), in_specs=in_specs, out_specs=out_specs, out_shape=out_shape,
        scratch_shapes=[scratch],
        compiler_params=pltpu.CompilerParams(
            dimension_semantics=("arbitrary", "arbitrary"), vmem_limit_bytes=int(limit)),
        name=name)(*args)


def _linear_scan_rows(a, b, carry):
    t, lanes = b.shape
    g = t // SUBLANES
    b3 = b.reshape(g, SUBLANES, lanes)
    a3 = None if a is None else a.reshape(g, SUBLANES, lanes)
    sub = lax.broadcasted_iota(jnp.int32, b3.shape, 1)
    s = 1
    while s < SUBLANES:
        keep = sub >= s
        b_prev = jnp.where(keep, pltpu.roll(b3, s, 1), 0.0)
        if a3 is None:
            b3 = b3 + b_prev
        else:
            b3 = a3 * b_prev + b3
            a3 = a3 * jnp.where(keep, pltpu.roll(a3, s, 1), 1.0)
        s *= 2
    outs = []
    for j in range(g):
        h = b3[j] + carry if a3 is None else b3[j] + a3[j] * carry
        outs.append(h)
        carry = h[SUBLANES - 1:, :]
    return jnp.concatenate(outs, axis=0), carry


def _softplus(x):
    return jnp.maximum(x, 0.0) + jnp.log1p(jnp.exp(-jnp.abs(x)))


def _gla_body(q_ref, k_ref, v_ref, g_ref, alr_ref, wa_ref, ba_ref, gn_ref, s0_ref,
              o_ref, s_out_ref, st_ref, *, chunk, n_chunks, n_pairs):
    n = pl.program_id(2)

    @pl.when(n == 0)
    def _():
        for p in range(n_pairs):
            st_ref[p] = s0_ref[0, p].T

    c = chunk
    qw, vw = 2 * DK_A, 2 * DV_A
    lane = lax.broadcasted_iota(jnp.int32, (c, qw), 1)
    head0 = lane < DK_A
    row2 = lax.broadcasted_iota(jnp.int32, (2 * c, 2 * c), 0)
    col2 = lax.broadcasted_iota(jnp.int32, (2 * c, 2 * c), 1)
    causal = ((row2 >= c) == (col2 >= c)) & (row2 >= col2)
    nt = (((1,), (1,)), ((), ()))
    tn = (((0,), (0,)), ((), ()))

    def stack_heads(z):
        return jnp.concatenate([jnp.where(head0, z, 0.0), jnp.where(head0, 0.0, z)], axis=0).astype(BF16)

    for ci in range(n_chunks):
        rows = pl.ds(ci * c, c)
        z = jnp.dot(alr_ref[rows, :].astype(BF16), wa_ref[...], preferred_element_type=F32) + ba_ref[...]
        log_alpha = -(jnp.maximum(-z, 0.0) + jnp.log(1.0 + jnp.exp(-jnp.abs(z)))) / GLA_TAU
        b, b_end = _linear_scan_rows(None, log_alpha, jnp.zeros((1, log_alpha.shape[1]), F32))
        q = q_ref[rows, :] * (DK_A ** -0.5)
        k = k_ref[rows, :]
        q_dec = q * jnp.exp(b)
        k_inv = k * jnp.exp(-b)
        k_end = k * jnp.exp(b_end - b)
        decay = jnp.exp(b_end)
        for p in range(n_pairs):
            ql = slice(p * qw, (p + 1) * qw)
            q_t, k_t, k_e = stack_heads(q_dec[:, ql]), stack_heads(k_inv[:, ql]), stack_heads(k_end[:, ql])
            v0 = v_ref[rows, pl.ds(p * vw, DV_A)]
            v1 = v_ref[rows, pl.ds(p * vw + DV_A, DV_A)]
            v_s = jnp.concatenate([v0, v1], axis=0).astype(BF16)
            att = lax.dot_general(q_t, k_t, nt, preferred_element_type=F32)
            att = jnp.where(causal, att, 0.0).astype(BF16)
            st = st_ref[p]
            o = (jnp.dot(att, v_s, preferred_element_type=F32)
                 + lax.dot_general(q_t, st.astype(BF16), nt, preferred_element_type=F32))
            st_ref[p] = decay[:, ql] * st + lax.dot_general(v_s, k_e, tn, preferred_element_type=F32)
            ms = jnp.mean(jnp.square(o), axis=-1, keepdims=True)
            o = (o * lax.rsqrt(ms + EPS) * gn_ref[...])
            g0 = g_ref[rows, pl.ds(p * vw, DV_A)]
            g1 = g_ref[rows, pl.ds(p * vw + DV_A, DV_A)]
            o_ref[rows, pl.ds(p * vw, DV_A)] = (o[:c] * (g0 * jax.nn.sigmoid(g0))).astype(o_ref.dtype)
            o_ref[rows, pl.ds(p * vw + DV_A, DV_A)] = (o[c:] * (g1 * jax.nn.sigmoid(g1))).astype(o_ref.dtype)

    @pl.when(n == pl.num_programs(2) - 1)
    def _():
        for p in range(n_pairs):
            s_out_ref[0, p] = st_ref[p].T


def _gla(hproj, alr, wa, ba, gn, s0, *, n_seq, seq_len, col_q, col_k, col_v, col_g):
    t_total = hproj.shape[0]
    h_a = s0.shape[1]
    pairs = h_a // 2
    pps = GLA_PAIRS_PER_STEP
    assert pairs % pps == 0
    chunk = min(CHUNK, seq_len)
    n_chunks = min(8, seq_len // chunk)
    tb = chunk * n_chunks
    nb = seq_len // tb
    qw, vw = 2 * DK_A * pps, 2 * DV_A * pps
    assert col_q % qw == 0 and col_k % qw == 0 and col_v % vw == 0 and col_g % vw == 0
    cq, ck, cv, cg = col_q // qw, col_k // qw, col_v // vw, col_g // vw
    row = lambda s, n: s * nb + n
    in_specs = [
        pl.BlockSpec((tb, qw), lambda s, p, n: (row(s, n), cq + p)),
        pl.BlockSpec((tb, qw), lambda s, p, n: (row(s, n), ck + p)),
        pl.BlockSpec((tb, vw), lambda s, p, n: (row(s, n), cv + p)),
        pl.BlockSpec((tb, vw), lambda s, p, n: (row(s, n), cg + p)),
        pl.BlockSpec((tb, LANES), lambda s, p, n: (row(s, n), 0)),
        pl.BlockSpec((LANES, qw), lambda s, p, n: (0, p)),
        pl.BlockSpec((1, qw), lambda s, p, n: (0, p)),
        pl.BlockSpec((1, DV_A), lambda s, p, n: (0, 0)),
        pl.BlockSpec((1, pps, 2 * DK_A, DV_A), lambda s, p, n: (s, p, 0, 0)),
    ]
    out_specs = [
        pl.BlockSpec((tb, vw), lambda s, p, n: (row(s, n), p)),
        pl.BlockSpec((1, pps, 2 * DK_A, DV_A), lambda s, p, n: (s, p, 0, 0)),
    ]
    o_a, s_new = pl.pallas_call(
        functools.partial(_gla_body, chunk=chunk, n_chunks=n_chunks, n_pairs=pps),
        grid=(n_seq, pairs // pps, nb), in_specs=in_specs, out_specs=out_specs,
        out_shape=[jax.ShapeDtypeStruct((t_total, h_a * DV_A), BF16),
                   jax.ShapeDtypeStruct((n_seq, pairs, 2 * DK_A, DV_A), F32)],
        scratch_shapes=[pltpu.VMEM((pps, DV_A, 2 * DK_A), F32)],
        compiler_params=pltpu.CompilerParams(
            dimension_semantics=("parallel", "parallel", "arbitrary"),
            vmem_limit_bytes=_vmem_limit(_nbytes((tb, 2 * qw + 2 * vw + LANES), F32), _nbytes((tb, vw), BF16))),
        name="gla")(hproj, hproj, hproj, hproj, alr, wa, ba, gn, s0.reshape(n_seq, pairs, 2 * DK_A, DV_A))
    return o_a, s_new.reshape(n_seq, h_a, DK_A, DV_A)


def _gmlp_body(u_ref, v_ref, w_ref, b_ref, o_ref, *vg_refs, c, dh):
    tril = (lax.broadcasted_iota(jnp.int32, (c, c), 0) >= lax.broadcasted_iota(jnp.int32, (c, c), 1))
    w_tril = [jnp.where(tril, w_ref[h, :c, :c], 0.0).astype(BF16) for h in range(w_ref.shape[0])]
    for ci in range(u_ref.shape[0] // c):
        rows = pl.ds(ci * c, c)
        vg = jax.nn.gelu(v_ref[rows, :])
        if vg_refs:
            vg_refs[0][rows, :] = vg
        ug = jax.nn.gelu(u_ref[rows, :])
        for h, w in enumerate(w_tril):
            cols = pl.ds(h * dh, dh)
            mixed = (jnp.dot(w, vg[:, h * dh:(h + 1) * dh].astype(BF16), preferred_element_type=F32)
                     + b_ref[:c, cols])
            o_ref[rows, cols] = (ug[:, h * dh:(h + 1) * dh] * mixed).astype(o_ref.dtype)


def _gmlp(hproj, w_sp, b_full, *, seq_len, col_u, col_v, want_v):
    t_total = hproj.shape[0]
    n_h, gc, _ = w_sp.shape
    w_b = b_full.shape[1]
    dh = w_b // n_h
    c = min(gc, seq_len)
    tb = c * min(4, seq_len // c)
    blk = lambda col: pl.BlockSpec((tb, w_b), lambda r: (r, col // w_b))
    out_specs = [pl.BlockSpec((tb, w_b), lambda r: (r, 0))]
    out_shape = [jax.ShapeDtypeStruct((t_total, w_b), BF16)]
    if want_v:
        out_specs.append(pl.BlockSpec((tb, w_b), lambda r: (r, 0)))
        out_shape.append(jax.ShapeDtypeStruct((t_total, w_b), F32))
    outs = pl.pallas_call(
        functools.partial(_gmlp_body, c=c, dh=dh), grid=(t_total // tb,),
        in_specs=[blk(col_u), blk(col_v),
                  pl.BlockSpec((n_h, gc, gc), lambda r: (0, 0, 0)),
                  pl.BlockSpec((gc, w_b), lambda r: (0, 0))],
        out_specs=out_specs, out_shape=out_shape,
        compiler_params=pltpu.CompilerParams(
            dimension_semantics=("parallel",),
            vmem_limit_bytes=_vmem_limit(4 * _nbytes((tb, w_b), F32), _nbytes((n_h, gc, gc), F32))),
        name="gmlp")(hproj, hproj, w_sp, b_full)
    return (outs[0], outs[1]) if want_v else (outs[0], None)


def _lru_body(x_ref, gate_ref, cw_ref, cb_ref, wr_ref, br_ref, wi_ref, bi_ref, lam_ref, h0_ref, tail0_ref,
              o_ref, h_ref, tail_ref, *, dh):
    n = pl.program_id(1)

    @pl.when(n == 0)
    def _():
        h_ref[...] = h0_ref[...]
        tail_ref[...] = tail0_ref[...]

    tb = x_ref.shape[0]
    row8 = lax.broadcasted_iota(jnp.int32, (SUBLANES, dh), 0)
    for hb in range(x_ref.shape[1] // dh):
        cols = pl.ds(hb * dh, dh)
        x = x_ref[:, cols]
        tail = tail_ref[0, :, cols]

        def delayed(j):
            rolled = pltpu.roll(x, j, 0)
            head = jnp.where(row8 >= j, rolled[:SUBLANES], pltpu.roll(tail, j, 0))
            return jnp.concatenate([head, rolled[SUBLANES:]], axis=0)

        y = cb_ref[:, cols] + delayed(CONV_W - 1) * cw_ref[0:1, cols]
        for j in range(1, CONV_W - 1):
            y = y + delayed(CONV_W - 1 - j) * cw_ref[j:j + 1, cols]
        y = y + x * cw_ref[CONV_W - 1:CONV_W, cols]
        tail_ref[0, :, cols] = x[tb - SUBLANES:]

        yb = y.astype(BF16)
        r = jax.nn.sigmoid(jnp.dot(yb, wr_ref[hb].astype(BF16), preferred_element_type=F32) + br_ref[:, cols])
        ig = jax.nn.sigmoid(jnp.dot(yb, wi_ref[hb].astype(BF16), preferred_element_type=F32) + bi_ref[:, cols])
        log_a = r * (-LRU_C * _softplus(-lam_ref[:, cols]))
        a = jnp.exp(log_a)
        gain2 = -jnp.tanh(log_a) * (a * a + 1.0)
        bt = jnp.where(gain2 > 0.0, gain2 * lax.rsqrt(gain2), 0.0) * (ig * y)
        hseq, h_last = _linear_scan_rows(a, bt, h_ref[0, :, cols])
        h_ref[0, :, cols] = h_last
        o_ref[:, cols] = (hseq * jax.nn.gelu(gate_ref[:, cols])).astype(o_ref.dtype)


def _lru(hproj, cw, cb, wr, br, wi, bi, lam, h0, tail0, *, n_seq, seq_len, col_x, col_gate):
    t_total = hproj.shape[0]
    w_c = cw.shape[1]
    n_h, dh, _ = wr.shape
    tb = min(seq_len, 256)
    nb = seq_len // tb
    vec = lambda rows: pl.BlockSpec((rows, w_c), lambda s, n: (0, 0))
    wspec = pl.BlockSpec((n_h, dh, dh), lambda s, n: (0, 0, 0))
    hspec = pl.BlockSpec((1, 1, w_c), lambda s, n: (s, 0, 0))
    tspec = pl.BlockSpec((1, SUBLANES, w_c), lambda s, n: (s, 0, 0))
    o_c, h_new, tail = pl.pallas_call(
        functools.partial(_lru_body, dh=dh), grid=(n_seq, nb),
        in_specs=[pl.BlockSpec((tb, w_c), lambda s, n: (s * nb + n, col_x // w_c)),
                  pl.BlockSpec((tb, w_c), lambda s, n: (s * nb + n, col_gate // w_c)),
                  vec(CONV_W), vec(1), wspec, vec(1), wspec, vec(1), vec(1), hspec, tspec],
        out_specs=[pl.BlockSpec((tb, w_c), lambda s, n: (s * nb + n, 0)), hspec, tspec],
        out_shape=[jax.ShapeDtypeStruct((t_total, w_c), BF16),
                   jax.ShapeDtypeStruct((n_seq, 1, w_c), F32),
                   jax.ShapeDtypeStruct((n_seq, SUBLANES, w_c), F32)],
        compiler_params=pltpu.CompilerParams(
            dimension_semantics=("parallel", "arbitrary"),
            vmem_limit_bytes=_vmem_limit(4 * _nbytes((tb, w_c), F32))),
        name="conv_rglru")(hproj, hproj, cw, cb.reshape(1, w_c), wr, br.reshape(1, w_c), wi,
                           bi.reshape(1, w_c), lam.reshape(1, w_c), h0.reshape(n_seq, 1, w_c), tail0)
    return o_c, h_new.reshape(n_seq, w_c), tail


def _mixers(hproj, alr, s_gla, s_conv, s_lru, w, *, n_seq, seq_len, want_v):
    w_a = s_gla.shape[1] * DV_A
    qk_a = s_gla.shape[1] * DK_A
    w_b = w_c = s_lru.shape[1]
    col_u = 2 * qk_a + 2 * w_a
    o_a, s_gla_new = _gla(hproj, alr, w["w_alpha2"], w["b_alpha"], w["g_onorm"], s_gla, n_seq=n_seq,
                          seq_len=seq_len, col_q=0, col_k=qk_a, col_v=2 * qk_a, col_g=2 * qk_a + w_a)
    o_b, vg = _gmlp(hproj, w["w_spatial"], w["b_spatial"], seq_len=seq_len, col_u=col_u, col_v=col_u + w_b,
                    want_v=want_v)
    tail0 = jnp.pad(s_conv, ((0, 0), (SUBLANES - (CONV_W - 1), 0), (0, 0)))
    o_c, s_lru_new, tail = _lru(hproj, w["conv_w"], w["conv_b"], w["w_rgate"], w["b_rgate"], w["w_igate"],
                                w["b_igate"], w["lru_lambda"], s_lru, tail0, n_seq=n_seq, seq_len=seq_len,
                                col_x=col_u + 2 * w_b, col_gate=col_u + 2 * w_b + w_c)
    return [o_a, o_b, o_c], (s_gla_new, tail[:, SUBLANES - (CONV_W - 1):], s_lru_new, vg)


def _small_layer_weights(l, w_in_t, w_alpha2, b_alpha, g_onorm, w_spatial, b_spatial, conv_w, conv_b,
                         w_rgate, b_rgate, w_igate, b_igate, lru_lambda):
    d = w_in_t.shape[2]
    w_a, w_b = d // 2, d // 4
    qk_a = (w_a // DV_A) * DK_A
    c_alr = 2 * qk_a + 2 * w_a
    w_alr = jnp.pad(w_in_t[l, c_alr:c_alr + GLA_RANK, :], ((0, LANES - GLA_RANK), (0, 0)))
    w_alpha2_p = jnp.pad(w_alpha2[l], ((0, LANES - GLA_RANK), (0, 0))).astype(BF16)
    b_full = jnp.repeat(b_spatial[l].T, w_b // H_B, axis=1)
    return dict(
        w_alr=w_alr, w_alpha2=w_alpha2_p, b_alpha=b_alpha[l].reshape(1, qk_a),
        g_onorm=g_onorm[l].reshape(1, DV_A), w_spatial=w_spatial[l], b_spatial=b_full,
        conv_w=conv_w[l], conv_b=conv_b[l], w_rgate=w_rgate[l], b_rgate=b_rgate[l],
        w_igate=w_igate[l], b_igate=b_igate[l], lru_lambda=lru_lambda[l])


def kernel(x_prompt, x_sample, state_gla, state_conv, state_lru, ln1, w_in, w_alpha2, b_alpha, g_onorm, w_spatial, b_spatial, conv_w, conv_b, w_rgate, b_rgate, w_igate, b_igate, lru_lambda, w_out, ln2, w_up, w_down, ln_final):
    bp, tp, d = x_prompt.shape
    bs, ts, _ = x_sample.shape
    depth = w_in.shape[0]
    w_a, w_b, w_c = d // 2, d // 4, d // 4
    h_a = w_a // DV_A
    qk_a = h_a * DK_A
    n_a = 2 * qk_a + 2 * w_a
    n_b = 2 * w_b + 2 * w_c
    zero_gla = jnp.zeros((bp, h_a, DK_A, DV_A), F32)
    zero_conv = jnp.zeros((bp, CONV_W - 1, w_c), F32)
    zero_lru = jnp.zeros((bp, w_c), F32)
    xp = x_prompt.reshape(bp * tp, d)
    xs = x_sample.reshape(bs * ts, d)
    w_in_t = jnp.swapaxes(w_in, 1, 2)
    mm = _ws_matmul
    st_p, st_s = [], []
    for l in range(depth):
        w = _small_layer_weights(l, w_in_t, w_alpha2, b_alpha, g_onorm, w_spatial, b_spatial, conv_w, conv_b,
                                 w_rgate, b_rgate, w_igate, b_igate, lru_lambda)
        xn_p, alr_p = _rmsnorm(xp, ln1[l], BF16, w_proj=w["w_alr"])
        xn_s, alr_s = _rmsnorm(xs, ln1[l], BF16, w_proj=w["w_alr"])
        hp_p, hp_s = mm([xn_p], [xn_s], w_in_t, l, n_cols=n_a + n_b, transposed=True,
                        w_row_gap=(n_a, GLA_RANK), bm=1024, name="in_proj")
        mix_p, new_p = _mixers(hp_p, alr_p, zero_gla, zero_conv, zero_lru, w, n_seq=bp, seq_len=tp,
                               want_v=False)
        mix_s, new_s = _mixers(hp_s, alr_s, state_gla[l], state_conv[l], state_lru[l], w, n_seq=bs,
                               seq_len=ts, want_v=True)
        st_p.append(new_p)
        st_s.append(new_s)
        xp, xs, xg_p, xg_s, ssq_p, ssq_s = mm(mix_p, mix_s, w_out, l, n_cols=d, res=(xp, xs),
                                              norm_gain=ln2[l].reshape(1, d), name="out_proj")
        hid_p, hid_s, w_down_bf = mm([xg_p], [xg_s], w_up, l, n_cols=w_up.shape[2], act="relu2",
                                     out_dtype=BF16, side_cast=(w_down, l, 1024), bm=1024, name="mlp_up")
        xp, xs = _ktiled_matmul(hid_p, hid_s, w_down_bf, (xp, xs), (ssq_p, ssq_s), d, bm=1024, bn=1024,
                                bk=2048, name="mlp_down")
    y_prompt = _rmsnorm(xp, ln_final, F32).reshape(bp, tp, d)
    y_sample = _rmsnorm(xs, ln_final, F32).reshape(bs, ts, d)
    stack = lambda sts, i: jnp.stack([s[i] for s in sts])
    return (y_prompt, y_sample, stack(st_p, 0), stack(st_p, 1), stack(st_p, 2),
            stack(st_s, 0), stack(st_s, 1), stack(st_s, 2),
            jnp.stack([s[3].reshape(bs, ts, w_b) for s in st_s]))
```

```python
import functools

import jax
import jax.numpy as jnp
from jax import lax
from jax.experimental import pallas as pl
from jax.experimental.pallas import tpu as pltpu

F32 = jnp.float32
BF16 = jnp.bfloat16

DV_A = 128
DK_A = 64
GLA_RANK = 16
GLA_TAU = 16.0
CHUNK = 64
GMLP_CHUNK = 128
H_B = 8
H_C = 8
CONV_W = 4
LRU_C = 8.0
EPS = 1e-6

LANES = 128
SUBLANES = 8
V7X_VMEM_BYTES = 64 * 1024 * 1024
VMEM_CAP = V7X_VMEM_BYTES - 6 * 1024 * 1024

GLA_PAIRS_PER_STEP = 8


def _vmem_limit(*block_bytes):
    need = 4 * sum(block_bytes)
    return int(min(max(need, 16 * 1024 * 1024), VMEM_CAP))


def _nbytes(shape, dtype):
    n = 1
    for s in shape:
        n *= s
    return n * jnp.dtype(dtype).itemsize


def _rmsnorm_body(x_ref, g_ref, o_ref):
    x = x_ref[...]
    ms = jnp.mean(jnp.square(x), axis=-1, keepdims=True)
    o_ref[...] = (x * lax.rsqrt(ms + EPS) * g_ref[...]).astype(o_ref.dtype)


def _rmsnorm_proj_body(x_ref, g_ref, w_ref, o_ref, p_ref):
    x = x_ref[...]
    ms = jnp.mean(jnp.square(x), axis=-1, keepdims=True)
    y = (x * lax.rsqrt(ms + EPS) * g_ref[...]).astype(BF16)
    o_ref[...] = y
    p_ref[...] = lax.dot_general(y, w_ref[...].astype(BF16), (((1,), (1,)), ((), ())),
                                 preferred_element_type=F32)


def _rmsnorm(x, g, out_dtype, w_proj=None):
    t, d = x.shape
    tb = min(t, 512)
    grid = (t // tb,)
    x_spec = pl.BlockSpec((tb, d), lambda i: (i, 0))
    g_spec = pl.BlockSpec((1, d), lambda i: (0, 0))
    params = pltpu.CompilerParams(
        dimension_semantics=("parallel",),
        vmem_limit_bytes=_vmem_limit(_nbytes((tb, d), F32), _nbytes((tb, d), out_dtype)))
    if w_proj is None:
        return pl.pallas_call(
            _rmsnorm_body, grid=grid, in_specs=[x_spec, g_spec], out_specs=x_spec,
            out_shape=jax.ShapeDtypeStruct((t, d), out_dtype), compiler_params=params,
            name="rmsnorm")(x, g.reshape(1, d))
    n = w_proj.shape[0]
    return pl.pallas_call(
        _rmsnorm_proj_body, grid=grid,
        in_specs=[x_spec, g_spec, pl.BlockSpec((n, d), lambda i: (0, 0))],
        out_specs=[x_spec, pl.BlockSpec((tb, n), lambda i: (i, 0))],
        out_shape=[jax.ShapeDtypeStruct((t, d), BF16), jax.ShapeDtypeStruct((t, n), F32)],
        compiler_params=params, name="rmsnorm_proj")(x, g.reshape(1, d), w_proj)


def _ktiled_matmul_body(a_p, a_s, w_ref, res_p, res_s, ssq_p, ssq_s, o_p, o_s, *, n_prompt_blocks, width):
    i = pl.program_id(0)

    def run(a_ref, res_ref, ssq_ref, o_ref):
        def scaled_dot():
            ss = ssq_ref[:, 0:1]
            for j in range(1, ssq_ref.shape[1] // LANES):
                ss = ss + ssq_ref[:, j * LANES:j * LANES + 1]
            acc = jnp.dot(a_ref[...], w_ref[...], preferred_element_type=F32)
            return acc * (1.0 / (ss / width + EPS))

        @pl.when(pl.program_id(2) == 0)
        def _():
            o_ref[...] = res_ref[...] + scaled_dot()

        @pl.when(pl.program_id(2) > 0)
        def _():
            o_ref[...] += scaled_dot()

    @pl.when(i < n_prompt_blocks)
    def _():
        run(a_p, res_p, ssq_p, o_p)

    @pl.when(i == n_prompt_blocks)
    def _():
        run(a_s, res_s, ssq_s, o_s)


def _ktiled_matmul(a_p, a_s, w, res, ssq, width, *, bm, bn, bk, name):
    mp, ms = a_p.shape[0], a_s.shape[0]
    k_total, n_cols = w.shape
    assert mp % bm == 0 and n_cols % bn == 0 and k_total % bk == 0
    n_p, n_j, nk = mp // bm, n_cols // bn, k_total // bk

    def p_row(i):
        return jnp.minimum(i, n_p - 1)

    def p_col(i, j):
        return jnp.where(i < n_p, j, n_j - 1)

    def s_col(i, j):
        return jnp.where(i == n_p, j, 0)

    io_p = pl.BlockSpec((bm, bn), lambda i, j, k: (p_row(i), p_col(i, j)))
    io_s = pl.BlockSpec((ms, bn), lambda i, j, k: (0, s_col(i, j)))
    in_specs = [
        pl.BlockSpec((bm, bk), lambda i, j, k: (p_row(i), jnp.where(i < n_p, k, nk - 1))),
        pl.BlockSpec((ms, bk), lambda i, j, k: (0, jnp.where(i == n_p, k, 0))),
        pl.BlockSpec((bk, bn), lambda i, j, k: (k, j)),
        io_p, io_s,
        pl.BlockSpec((bm, ssq[0].shape[1]), lambda i, j, k: (p_row(i), 0)),
        pl.BlockSpec((ms, ssq[1].shape[1]), lambda i, j, k: (0, 0)),
    ]
    limit = _vmem_limit(_nbytes((bm + ms, bk), BF16), _nbytes((bk, bn), BF16), 2 * _nbytes((bm + ms, bn), F32),
                        _nbytes((bm + ms, ssq[0].shape[1]), F32))
    return pl.pallas_call(
        functools.partial(_ktiled_matmul_body, n_prompt_blocks=n_p, width=width),
        grid=(n_p + 1, n_j, nk), in_specs=in_specs, out_specs=[io_p, io_s],
        out_shape=[jax.ShapeDtypeStruct((mp, n_cols), F32), jax.ShapeDtypeStruct((ms, n_cols), F32)],
        compiler_params=pltpu.CompilerParams(
            dimension_semantics=("arbitrary", "arbitrary", "arbitrary"), vmem_limit_bytes=limit),
        name=name)(a_p, a_s, w, res[0], res[1], ssq[0], ssq[1])


def _ws_matmul_body(*refs, k_bounds, has_res, has_side, has_gain, act, n_prompt_blocks, chunk_rows,
                    transposed):
    refs = list(refs)
    n_lhs = len(k_bounds)
    take = lambda n: [refs.pop(0) for _ in range(n)]
    lhs_p, lhs_s = take(n_lhs), take(n_lhs)
    (wc_ref,) = take(1)
    res_p, res_s = take(2) if has_res else (None, None)
    (side_in,) = take(1) if has_side else (None,)
    (gain_ref,) = take(1) if has_gain else (None,)
    o_p, o_s = take(2)
    (side_out,) = take(1) if has_side else (None,)
    gained_p, gained_s, ssq_p, ssq_s = take(4) if has_gain else (None,) * 4
    (w_scr,) = take(1)
    jj, i = pl.program_id(0), pl.program_id(1)
    slot_load = jj % 2
    slot_use = 1 - slot_load

    def stage():
        r0 = pl.multiple_of(jnp.minimum(i, n_prompt_blocks - 1) * chunk_rows, chunk_rows)
        chunk = wc_ref[0] if len(wc_ref.shape) == 3 else wc_ref[...]
        w_scr[slot_load, pl.ds(r0, chunk_rows), :] = chunk.astype(BF16)

    def run(lhs_refs, res_ref, o_ref, gained_ref, ssq_ref):
        acc = None
        for a_ref, (k0, k1) in zip(lhs_refs, k_bounds):
            if transposed:
                part = lax.dot_general(a_ref[...], w_scr[slot_use, :, k0:k1], (((1,), (1,)), ((), ())),
                                       preferred_element_type=F32)
            else:
                part = jnp.dot(a_ref[...], w_scr[slot_use, k0:k1, :], preferred_element_type=F32)
            acc = part if acc is None else acc + part
        if has_res:
            acc = res_ref[...] + acc
        if act == "relu2":
            acc = jnp.square(jnp.maximum(acc, 0.0))
        o_ref[...] = acc.astype(o_ref.dtype)
        if has_gain:
            gained_ref[...] = (acc * gain_ref[...]).astype(gained_ref.dtype)
            ssq_ref[...] = jnp.broadcast_to(jnp.sum(acc * acc, axis=1, keepdims=True), ssq_ref.shape)

    @pl.when((jj == 0) & (i < n_prompt_blocks))
    def _():
        stage()

    @pl.when((jj > 0) & (i < n_prompt_blocks))
    def _():
        stage()
        if has_side:
            side_out[...] = side_in[...].astype(side_out.dtype)
        run(lhs_p, res_p, o_p, gained_p, ssq_p)

    @pl.when((jj > 0) & (i == n_prompt_blocks))
    def _():
        run(lhs_s, res_s, o_s, gained_s, ssq_s)


def _ws_matmul(lhs_p, lhs_s, w, layer, *, n_cols, transposed=False, w_row_gap=(0, 0), bm=512, bn=1024,
               res=None, act=None, out_dtype=F32, side_cast=None, norm_gain=None, name="ws_matmul"):
    mp, ms = lhs_p[0].shape[0], lhs_s[0].shape[0]
    k_total = w.shape[2] if transposed else w.shape[1]
    widths = [a.shape[1] for a in lhs_p]
    assert sum(widths) == k_total and mp % bm == 0 and n_cols % bn == 0
    n_p, n_j = mp // bm, n_cols // bn
    bounds, k0 = [], 0
    for wd in widths:
        bounds.append((k0, k0 + wd))
        k0 += wd

    def row_blk(jj, i):
        return jnp.where(jj == 0, 0, jnp.minimum(i, n_p - 1))

    def lhs_row_blk(jj, i):
        return jnp.where((jj == 0) | (i == n_p), 0, i)

    def col_use(jj):
        return jnp.maximum(jj - 1, 0)

    def ahead(jj, i):
        return (i == n_p) & (jj + 1 < n_j)

    def chunk(jj, i):
        return jnp.where(ahead(jj, i), 0, jnp.where(jj < n_j, jnp.minimum(i, n_p - 1), n_p - 1))

    def col_load(jj, i):
        return jnp.minimum(jnp.where(ahead(jj, i), jj + 1, jj), n_j - 1)

    if transposed:
        chunk_rows = bn // n_p
        gap_row, gap_rows = w_row_gap
        assert gap_row % bn == 0 and gap_rows % SUBLANES == 0

        def w_row(jj, i):
            row = (col_load(jj, i) * n_p + chunk(jj, i)) * chunk_rows
            return pl.multiple_of(row + jnp.where(row >= gap_row, gap_rows, 0), SUBLANES)

        w_spec = pl.BlockSpec((pl.Element(1), pl.Element(chunk_rows), pl.Element(k_total)),
                              lambda jj, i: (layer, w_row(jj, i), 0))
        scratch = pltpu.VMEM((2, bn, k_total), BF16)
    else:
        chunk_rows = k_total // n_p
        w_spec = pl.BlockSpec((None, chunk_rows, bn), lambda jj, i: (layer, chunk(jj, i), col_load(jj, i)))
        scratch = pltpu.VMEM((2, k_total, bn), BF16)
    assert chunk_rows * n_p == (bn if transposed else k_total) and chunk_rows % 16 == 0
    specs_p = [pl.BlockSpec((bm, wd), lambda jj, i: (lhs_row_blk(jj, i), 0)) for wd in widths]
    specs_s = [pl.BlockSpec((ms, wd), lambda jj, i: (0, 0)) for wd in widths]
    io_p = pl.BlockSpec((bm, bn), lambda jj, i: (row_blk(jj, i), col_use(jj)))
    io_s = pl.BlockSpec((ms, bn), lambda jj, i: (0, col_use(jj)))
    in_specs = specs_p + specs_s + [w_spec]
    args = list(lhs_p) + list(lhs_s) + [w]
    if res is not None:
        in_specs += [io_p, io_s]
        args += list(res)
    out_specs = [io_p, io_s]
    out_shape = [jax.ShapeDtypeStruct((mp, n_cols), out_dtype), jax.ShapeDtypeStruct((ms, n_cols), out_dtype)]
    side_bytes = 0
    if side_cast is not None:
        side, side_layer = side_cast
        side_rows, side_cols = side.shape[1], side.shape[2]
        rb = side_rows // (n_j * n_p)
        assert rb * n_j * n_p == side_rows and rb % 16 == 0

        def side_blk(jj, i):
            return jnp.where(jj == 0, 0, (jj - 1) * n_p + jnp.minimum(i, n_p - 1))

        def side_in_blk(jj, i):
            return jnp.where((i == n_p) & (jj >= 1) & (jj < n_j), jj * n_p, side_blk(jj, i))

        in_specs.append(pl.BlockSpec((None, rb, side_cols), lambda jj, i: (side_layer, side_in_blk(jj, i), 0)))
        args.append(side)
        out_specs.append(pl.BlockSpec((rb, side_cols), lambda jj, i: (side_blk(jj, i), 0)))
        out_shape.append(jax.ShapeDtypeStruct((side_rows, side_cols), BF16))
        side_bytes = _nbytes((rb, side_cols), F32) + _nbytes((rb, side_cols), BF16)
    if norm_gain is not None:
        in_specs.append(pl.BlockSpec((1, bn), lambda jj, i: (0, col_use(jj))))
        args.append(norm_gain)
        out_specs += [io_p, io_s,
                      pl.BlockSpec((bm, LANES), lambda jj, i: (row_blk(jj, i), col_use(jj))),
                      pl.BlockSpec((ms, LANES), lambda jj, i: (0, col_use(jj)))]
        out_shape += [jax.ShapeDtypeStruct((mp, n_cols), BF16), jax.ShapeDtypeStruct((ms, n_cols), BF16),
                      jax.ShapeDtypeStruct((mp, n_j * LANES), F32), jax.ShapeDtypeStruct((ms, n_j * LANES), F32)]
        side_bytes += _nbytes((bm + ms, bn), BF16) + _nbytes((bm + ms, LANES), F32)
    body = functools.partial(_ws_matmul_body, k_bounds=tuple(bounds), has_res=res is not None,
                             has_side=side_cast is not None, has_gain=norm_gain is not None, act=act,
                             n_prompt_blocks=n_p, chunk_rows=chunk_rows, transposed=transposed)
    n_io = 2 if res is not None else 1
    pipelined = (_nbytes((bm + ms, k_total), BF16) + _nbytes((k_total // n_p, bn), F32)
                 + n_io * _nbytes((bm + ms, bn), F32) + side_bytes)
    limit = min(VMEM_CAP, 2 * pipelined + _nbytes((2, k_total, bn), BF16) + 2 * _nbytes((bm, bn), F32))
    return pl.pallas_call(
        body, grid=(n_j + 1, n_p + 1), in_specs=in_specs, out_specs=out_specs, out_shape=out_shape,
        scratch_shapes=[scratch],
        compiler_params=pltpu.CompilerParams(
            dimension_semantics=("arbitrary", "arbitrary"), vmem_limit_bytes=int(limit)),
        name=name)(*args)


def _linear_scan_rows(a, b, carry):
    t, lanes = b.shape
    g = t // SUBLANES
    b3 = b.reshape(g, SUBLANES, lanes)
    a3 = None if a is None else a.reshape(g, SUBLANES, lanes)
    sub = lax.broadcasted_iota(jnp.int32, b3.shape, 1)
    s = 1
    while s < SUBLANES:
        keep = sub >= s
        b_prev = jnp.where(keep, pltpu.roll(b3, s, 1), 0.0)
        if a3 is None:
            b3 = b3 + b_prev
        else:
            b3 = a3 * b_prev + b3
            a3 = a3 * jnp.where(keep, pltpu.roll(a3, s, 1), 1.0)
        s *= 2
    outs = []
    for j in range(g):
        h = b3[j] + carry if a3 is None else b3[j] + a3[j] * carry
        outs.append(h)
        carry = h[SUBLANES - 1:, :]
    return jnp.concatenate(outs, axis=0), carry


def _softplus(x):
    return jnp.maximum(x, 0.0) + jnp.log1p(jnp.exp(-jnp.abs(x)))


def _gla_body(q_ref, k_ref, v_ref, g_ref, alr_ref, wa_ref, ba_ref, gn_ref, s0_ref,
              o_ref, s_out_ref, st_ref, *, chunk, n_chunks, n_pairs):
    n = pl.program_id(2)

    @pl.when(n == 0)
    def _():
        for p in range(n_pairs):
            st_ref[p] = s0_ref[0, p].T

    c = chunk
    qw, vw = 2 * DK_A, 2 * DV_A
    lane = lax.broadcasted_iota(jnp.int32, (c, qw), 1)
    head0 = lane < DK_A
    row2 = lax.broadcasted_iota(jnp.int32, (2 * c, 2 * c), 0)
    col2 = lax.broadcasted_iota(jnp.int32, (2 * c, 2 * c), 1)
    causal = ((row2 >= c) == (col2 >= c)) & (row2 >= col2)
    nt = (((1,), (1,)), ((), ()))
    tn = (((0,), (0,)), ((), ()))

    def stack_heads(z):
        return jnp.concatenate([jnp.where(head0, z, 0.0), jnp.where(head0, 0.0, z)], axis=0).astype(BF16)

    for ci in range(n_chunks):
        rows = pl.ds(ci * c, c)
        z = jnp.dot(alr_ref[rows, :].astype(BF16), wa_ref[...], preferred_element_type=F32) + ba_ref[...]
        log_alpha = -(jnp.maximum(-z, 0.0) + jnp.log(1.0 + jnp.exp(-jnp.abs(z)))) / GLA_TAU
        b, b_end = _linear_scan_rows(None, log_alpha, jnp.zeros((1, log_alpha.shape[1]), F32))
        q = q_ref[rows, :] * (DK_A ** -0.5)
        k = k_ref[rows, :]
        q_dec = q * jnp.exp(b)
        k_inv = k * jnp.exp(-b)
        k_end = k * jnp.exp(b_end - b)
        decay = jnp.exp(b_end)
        for p in range(n_pairs):
            ql = slice(p * qw, (p + 1) * qw)
            q_t, k_t, k_e = stack_heads(q_dec[:, ql]), stack_heads(k_inv[:, ql]), stack_heads(k_end[:, ql])
            v0 = v_ref[rows, pl.ds(p * vw, DV_A)]
            v1 = v_ref[rows, pl.ds(p * vw + DV_A, DV_A)]
            v_s = jnp.concatenate([v0, v1], axis=0).astype(BF16)
            att = lax.dot_general(q_t, k_t, nt, preferred_element_type=F32)
            att = jnp.where(causal, att, 0.0).astype(BF16)
            st = st_ref[p]
            o = (jnp.dot(att, v_s, preferred_element_type=F32)
                 + lax.dot_general(q_t, st.astype(BF16), nt, preferred_element_type=F32))
            st_ref[p] = decay[:, ql] * st + lax.dot_general(v_s, k_e, tn, preferred_element_type=F32)
            ms = jnp.mean(jnp.square(o), axis=-1, keepdims=True)
            o = (o * lax.rsqrt(ms + EPS) * gn_ref[...])
            g0 = g_ref[rows, pl.ds(p * vw, DV_A)]
            g1 = g_ref[rows, pl.ds(p * vw + DV_A, DV_A)]
            o_ref[rows, pl.ds(p * vw, DV_A)] = (o[:c] * (g0 * jax.nn.sigmoid(g0))).astype(o_ref.dtype)
            o_ref[rows, pl.ds(p * vw + DV_A, DV_A)] = (o[c:] * (g1 * jax.nn.sigmoid(g1))).astype(o_ref.dtype)

    @pl.when(n == pl.num_programs(2) - 1)
    def _():
        for p in range(n_pairs):
            s_out_ref[0, p] = st_ref[p].T


def _gla(hproj, alr, wa, ba, gn, s0, *, n_seq, seq_len, col_q, col_k, col_v, col_g):
    t_total = hproj.shape[0]
    h_a = s0.shape[1]
    pairs = h_a // 2
    pps = GLA_PAIRS_PER_STEP
    assert pairs % pps == 0
    chunk = min(CHUNK, seq_len)
    n_chunks = min(8, seq_len // chunk)
    tb = chunk * n_chunks
    nb = seq_len // tb
    qw, vw = 2 * DK_A * pps, 2 * DV_A * pps
    assert col_q % qw == 0 and col_k % qw == 0 and col_v % vw == 0 and col_g % vw == 0
    cq, ck, cv, cg = col_q // qw, col_k // qw, col_v // vw, col_g // vw
    row = lambda s, n: s * nb + n
    in_specs = [
        pl.BlockSpec((tb, qw), lambda s, p, n: (row(s, n), cq + p)),
        pl.BlockSpec((tb, qw), lambda s, p, n: (row(s, n), ck + p)),
        pl.BlockSpec((tb, vw), lambda s, p, n: (row(s, n), cv + p)),
        pl.BlockSpec((tb, vw), lambda s, p, n: (row(s, n), cg + p)),
        pl.BlockSpec((tb, LANES), lambda s, p, n: (row(s, n), 0)),
        pl.BlockSpec((LANES, qw), lambda s, p, n: (0, p)),
        pl.BlockSpec((1, qw), lambda s, p, n: (0, p)),
        pl.BlockSpec((1, DV_A), lambda s, p, n: (0, 0)),
        pl.BlockSpec((1, pps, 2 * DK_A, DV_A), lambda s, p, n: (s, p, 0, 0)),
    ]
    out_specs = [
        pl.BlockSpec((tb, vw), lambda s, p, n: (row(s, n), p)),
        pl.BlockSpec((1, pps, 2 * DK_A, DV_A), lambda s, p, n: (s, p, 0, 0)),
    ]
    o_a, s_new = pl.pallas_call(
        functools.partial(_gla_body, chunk=chunk, n_chunks=n_chunks, n_pairs=pps),
        grid=(n_seq, pairs // pps, nb), in_specs=in_specs, out_specs=out_specs,
        out_shape=[jax.ShapeDtypeStruct((t_total, h_a * DV_A), BF16),
                   jax.ShapeDtypeStruct((n_seq, pairs, 2 * DK_A, DV_A), F32)],
        scratch_shapes=[pltpu.VMEM((pps, DV_A, 2 * DK_A), F32)],
        compiler_params=pltpu.CompilerParams(
            dimension_semantics=("parallel", "parallel", "arbitrary"),
            vmem_limit_bytes=_vmem_limit(_nbytes((tb, 2 * qw + 2 * vw + LANES), F32), _nbytes((tb, vw), BF16))),
        name="gla")(hproj, hproj, hproj, hproj, alr, wa, ba, gn, s0.reshape(n_seq, pairs, 2 * DK_A, DV_A))
    return o_a, s_new.reshape(n_seq, h_a, DK_A, DV_A)


def _gmlp_body(u_ref, v_ref, w_ref, b_ref, o_ref, *vg_refs, c, dh):
    tril = (lax.broadcasted_iota(jnp.int32, (c, c), 0) >= lax.broadcasted_iota(jnp.int32, (c, c), 1))
    w_tril = [jnp.where(tril, w_ref[h, :c, :c], 0.0).astype(BF16) for h in range(w_ref.shape[0])]
    for ci in range(u_ref.shape[0] // c):
        rows = pl.ds(ci * c, c)
        vg = jax.nn.gelu(v_ref[rows, :])
        if vg_refs:
            vg_refs[0][rows, :] = vg
        ug = jax.nn.gelu(u_ref[rows, :])
        for h, w in enumerate(w_tril):
            cols = pl.ds(h * dh, dh)
            mixed = (jnp.dot(w, vg[:, h * dh:(h + 1) * dh].astype(BF16), preferred_element_type=F32)
                     + b_ref[:c, cols])
            o_ref[rows, cols] = (ug[:, h * dh:(h + 1) * dh] * mixed).astype(o_ref.dtype)


def _gmlp(hproj, w_sp, b_full, *, seq_len, col_u, col_v, want_v):
    t_total = hproj.shape[0]
    n_h, gc, _ = w_sp.shape
    w_b = b_full.shape[1]
    dh = w_b // n_h
    c = min(gc, seq_len)
    tb = c * min(4, seq_len // c)
    blk = lambda col: pl.BlockSpec((tb, w_b), lambda r: (r, col // w_b))
    out_specs = [pl.BlockSpec((tb, w_b), lambda r: (r, 0))]
    out_shape = [jax.ShapeDtypeStruct((t_total, w_b), BF16)]
    if want_v:
        out_specs.append(pl.BlockSpec((tb, w_b), lambda r: (r, 0)))
        out_shape.append(jax.ShapeDtypeStruct((t_total, w_b), F32))
    outs = pl.pallas_call(
        functools.partial(_gmlp_body, c=c, dh=dh), grid=(t_total // tb,),
        in_specs=[blk(col_u), blk(col_v),
                  pl.BlockSpec((n_h, gc, gc), lambda r: (0, 0, 0)),
                  pl.BlockSpec((gc, w_b), lambda r: (0, 0))],
        out_specs=out_specs, out_shape=out_shape,
        compiler_params=pltpu.CompilerParams(
            dimension_semantics=("parallel",),
            vmem_limit_bytes=_vmem_limit(4 * _nbytes((tb, w_b), F32), _nbytes((n_h, gc, gc), F32))),
        name="gmlp")(hproj, hproj, w_sp, b_full)
    return (outs[0], outs[1]) if want_v else (outs[0], None)


def _lru_body(x_ref, gate_ref, cw_ref, cb_ref, wr_ref, br_ref, wi_ref, bi_ref, lam_ref, h0_ref, tail0_ref,
              o_ref, h_ref, tail_ref, *, dh):
    n = pl.program_id(1)

    @pl.when(n == 0)
    def _():
        h_ref[...] = h0_ref[...]
        tail_ref[...] = tail0_ref[...]

    tb = x_ref.shape[0]
    row8 = lax.broadcasted_iota(jnp.int32, (SUBLANES, dh), 0)
    for hb in range(x_ref.shape[1] // dh):
        cols = pl.ds(hb * dh, dh)
        x = x_ref[:, cols]
        tail = tail_ref[0, :, cols]

        def delayed(j):
            rolled = pltpu.roll(x, j, 0)
            head = jnp.where(row8 >= j, rolled[:SUBLANES], pltpu.roll(tail, j, 0))
            return jnp.concatenate([head, rolled[SUBLANES:]], axis=0)

        y = cb_ref[:, cols] + delayed(CONV_W - 1) * cw_ref[0:1, cols]
        for j in range(1, CONV_W - 1):
            y = y + delayed(CONV_W - 1 - j) * cw_ref[j:j + 1, cols]
        y = y + x * cw_ref[CONV_W - 1:CONV_W, cols]
        tail_ref[0, :, cols] = x[tb - SUBLANES:]

        yb = y.astype(BF16)
        r = jax.nn.sigmoid(jnp.dot(yb, wr_ref[hb].astype(BF16), preferred_element_type=F32) + br_ref[:, cols])
        ig = jax.nn.sigmoid(jnp.dot(yb, wi_ref[hb].astype(BF16), preferred_element_type=F32) + bi_ref[:, cols])
        log_a = r * (-LRU_C * _softplus(-lam_ref[:, cols]))
        a = jnp.exp(log_a)
        gain2 = -jnp.tanh(log_a) * (a * a + 1.0)
        bt = jnp.where(gain2 > 0.0, gain2 * lax.rsqrt(gain2), 0.0) * (ig * y)
        hseq, h_last = _linear_scan_rows(a, bt, h_ref[0, :, cols])
        h_ref[0, :, cols] = h_last
        o_ref[:, cols] = (hseq * jax.nn.gelu(gate_ref[:, cols])).astype(o_ref.dtype)


def _lru(hproj, cw, cb, wr, br, wi, bi, lam, h0, tail0, *, n_seq, seq_len, col_x, col_gate):
    t_total = hproj.shape[0]
    w_c = cw.shape[1]
    n_h, dh, _ = wr.shape
    tb = min(seq_len, 256)
    nb = seq_len // tb
    vec = lambda rows: pl.BlockSpec((rows, w_c), lambda s, n: (0, 0))
    wspec = pl.BlockSpec((n_h, dh, dh), lambda s, n: (0, 0, 0))
    hspec = pl.BlockSpec((1, 1, w_c), lambda s, n: (s, 0, 0))
    tspec = pl.BlockSpec((1, SUBLANES, w_c), lambda s, n: (s, 0, 0))
    o_c, h_new, tail = pl.pallas_call(
        functools.partial(_lru_body, dh=dh), grid=(n_seq, nb),
        in_specs=[pl.BlockSpec((tb, w_c), lambda s, n: (s * nb + n, col_x // w_c)),
                  pl.BlockSpec((tb, w_c), lambda s, n: (s * nb + n, col_gate // w_c)),
                  vec(CONV_W), vec(1), wspec, vec(1), wspec, vec(1), vec(1), hspec, tspec],
        out_specs=[pl.BlockSpec((tb, w_c), lambda s, n: (s * nb + n, 0)), hspec, tspec],
        out_shape=[jax.ShapeDtypeStruct((t_total, w_c), BF16),
                   jax.ShapeDtypeStruct((n_seq, 1, w_c), F32),
                   jax.ShapeDtypeStruct((n_seq, SUBLANES, w_c), F32)],
        compiler_params=pltpu.CompilerParams(
            dimension_semantics=("parallel", "arbitrary"),
            vmem_limit_bytes=_vmem_limit(4 * _nbytes((tb, w_c), F32))),
        name="conv_rglru")(hproj, hproj, cw, cb.reshape(1, w_c), wr, br.reshape(1, w_c), wi,
                           bi.reshape(1, w_c), lam.reshape(1, w_c), h0.reshape(n_seq, 1, w_c), tail0)
    return o_c, h_new.reshape(n_seq, w_c), tail


def _mixers(hproj, alr, s_gla, s_conv, s_lru, w, *, n_seq, seq_len, want_v):
    w_a = s_gla.shape[1] * DV_A
    qk_a = s_gla.shape[1] * DK_A
    w_b = w_c = s_lru.shape[1]
    col_u = 2 * qk_a + 2 * w_a
    o_a, s_gla_new = _gla(hproj, alr, w["w_alpha2"], w["b_alpha"], w["g_onorm"], s_gla, n_seq=n_seq,
                          seq_len=seq_len, col_q=0, col_k=qk_a, col_v=2 * qk_a, col_g=2 * qk_a + w_a)
    o_b, vg = _gmlp(hproj, w["w_spatial"], w["b_spatial"], seq_len=seq_len, col_u=col_u, col_v=col_u + w_b,
                    want_v=want_v)
    tail0 = jnp.pad(s_conv, ((0, 0), (SUBLANES - (CONV_W - 1), 0), (0, 0)))
    o_c, s_lru_new, tail = _lru(hproj, w["conv_w"], w["conv_b"], w["w_rgate"], w["b_rgate"], w["w_igate"],
                                w["b_igate"], w["lru_lambda"], s_lru, tail0, n_seq=n_seq, seq_len=seq_len,
                                col_x=col_u + 2 * w_b, col_gate=col_u + 2 * w_b + w_c)
    return [o_a, o_b, o_c], (s_gla_new, tail[:, SUBLANES - (CONV_W - 1):], s_lru_new, vg)


def _small_layer_weights(l, w_in_t, w_alpha2, b_alpha, g_onorm, w_spatial, b_spatial, conv_w, conv_b,
                         w_rgate, b_rgate, w_igate, b_igate, lru_lambda):
    d = w_in_t.shape[2]
    w_a, w_b = d // 2, d // 4
    qk_a = (w_a // DV_A) * DK_A
    c_alr = 2 * qk_a + 2 * w_a
    w_alr = jnp.pad(w_in_t[l, c_alr:c_alr + GLA_RANK, :], ((0, LANES - GLA_RANK), (0, 0)))
    w_alpha2_p = jnp.pad(w_alpha2[l], ((0, LANES - GLA_RANK), (0, 0))).astype(BF16)
    b_full = jnp.repeat(b_spatial[l].T, w_b // H_B, axis=1)
    return dict(
        w_alr=w_alr, w_alpha2=w_alpha2_p, b_alpha=b_alpha[l].reshape(1, qk_a),
        g_onorm=g_onorm[l].reshape(1, DV_A), w_spatial=w_spatial[l], b_spatial=b_full,
        conv_w=conv_w[l], conv_b=conv_b[l], w_rgate=w_rgate[l], b_rgate=b_rgate[l],
        w_igate=w_igate[l], b_igate=b_igate[l], lru_lambda=lru_lambda[l])


def kernel(x_prompt, x_sample, state_gla, state_conv, state_lru, ln1, w_in, w_alpha2, b_alpha, g_onorm, w_spatial, b_spatial, conv_w, conv_b, w_rgate, b_rgate, w_igate, b_igate, lru_lambda, w_out, ln2, w_up, w_down, ln_final):
    bp, tp, d = x_prompt.shape
    bs, ts, _ = x_sample.shape
    depth = w_in.shape[0]
    w_a, w_b, w_c = d // 2, d // 4, d // 4
    h_a = w_a // DV_A
    qk_a = h_a * DK_A
    n_a = 2 * qk_a + 2 * w_a
    n_b = 2 * w_b + 2 * w_c
    zero_gla = jnp.zeros((bp, h_a, DK_A, DV_A), F32)
    zero_conv = jnp.zeros((bp, CONV_W - 1, w_c), F32)
    zero_lru = jnp.zeros((bp, w_c), F32)
    xp = x_prompt.reshape(bp * tp, d)
    xs = x_sample.reshape(bs * ts, d)
    w_in_t = jnp.swapaxes(w_in, 1, 2)
    mm = _ws_matmul
    st_p, st_s = [], []
    for l in range(depth):
        w = _small_layer_weights(l, w_in_t, w_alpha2, b_alpha, g_onorm, w_spatial, b_spatial, conv_w, conv_b,
                                 w_rgate, b_rgate, w_igate, b_igate, lru_lambda)
        xn_p, alr_p = _rmsnorm(xp, ln1[l], BF16, w_proj=w["w_alr"])
        xn_s, alr_s = _rmsnorm(xs, ln1[l], BF16, w_proj=w["w_alr"])
        hp_p, hp_s = mm([xn_p], [xn_s], w_in_t, l, n_cols=n_a + n_b, transposed=True,
                        w_row_gap=(n_a, GLA_RANK), bm=1024, name="in_proj")
        mix_p, new_p = _mixers(hp_p, alr_p, zero_gla, zero_conv, zero_lru, w, n_seq=bp, seq_len=tp,
                               want_v=False)
        mix_s, new_s = _mixers(hp_s, alr_s, state_gla[l], state_conv[l], state_lru[l], w, n_seq=bs,
                               seq_len=ts, want_v=True)
        st_p.append(new_p)
        st_s.append(new_s)
        xp, xs, xg_p, xg_s, ssq_p, ssq_s = mm(mix_p, mix_s, w_out, l, n_cols=d, res=(xp, xs),
                                              norm_gain=ln2[l].reshape(1, d), name="out_proj")
        hid_p, hid_s, w_down_bf = mm([xg_p], [xg_s], w_up, l, n_cols=w_up.shape[2], act="relu2",
                                     out_dtype=BF16, side_cast=(w_down, l), bm=1024, name="mlp_up")
        xp, xs = _ktiled_matmul(hid_p, hid_s, w_down_bf, (xp, xs), (ssq_p, ssq_s), d, bm=1024, bn=1024,
                                bk=2048, name="mlp_down")
    y_prompt = _rmsnorm(xp, ln_final, F32).reshape(bp, tp, d)
    y_sample = _rmsnorm(xs, ln_final, F32).reshape(bs, ts, d)
    stack = lambda sts, i: jnp.stack([s[i] for s in sts])
    return (y_prompt, y_sample, stack(st_p, 0), stack(st_p, 1), stack(st_p, 2),
            stack(st_s, 0), stack(st_s, 1), stack(st_s, 2),
            jnp.stack([s[3].reshape(bs, ts, w_b) for s in st_s]))
```

```python
import functools

import jax
import jax.numpy as jnp
from jax import lax
from jax.experimental import pallas as pl
from jax.experimental.pallas import tpu as pltpu

F32 = jnp.float32
BF16 = jnp.bfloat16

DV_A = 128
DK_A = 64
GLA_RANK = 16
GLA_TAU = 16.0
CHUNK = 64
GMLP_CHUNK = 128
H_B = 8
H_C = 8
CONV_W = 4
LRU_C = 8.0
EPS = 1e-6

LANES = 128
SUBLANES = 8
V7X_VMEM_BYTES = 64 * 1024 * 1024
VMEM_CAP = V7X_VMEM_BYTES - 6 * 1024 * 1024

GLA_PAIRS_PER_STEP = 8


def _vmem_limit(*block_bytes):
    need = 4 * sum(block_bytes)
    return int(min(max(need, 16 * 1024 * 1024), VMEM_CAP))


def _nbytes(shape, dtype):
    n = 1
    for s in shape:
        n *= s
    return n * jnp.dtype(dtype).itemsize


def _rmsnorm_body(x_ref, g_ref, o_ref):
    x = x_ref[...]
    ms = jnp.mean(jnp.square(x), axis=-1, keepdims=True)
    o_ref[...] = (x * lax.rsqrt(ms + EPS) * g_ref[...]).astype(o_ref.dtype)


def _rmsnorm_proj_body(x_ref, g_ref, w_ref, o_ref, p_ref):
    x = x_ref[...]
    ms = jnp.mean(jnp.square(x), axis=-1, keepdims=True)
    y = (x * lax.rsqrt(ms + EPS) * g_ref[...]).astype(BF16)
    o_ref[...] = y
    p_ref[...] = lax.dot_general(y, w_ref[...].astype(BF16), (((1,), (1,)), ((), ())),
                                 preferred_element_type=F32)


def _rmsnorm(x, g, out_dtype, w_proj=None):
    t, d = x.shape
    tb = min(t, 512)
    grid = (t // tb,)
    x_spec = pl.BlockSpec((tb, d), lambda i: (i, 0))
    g_spec = pl.BlockSpec((1, d), lambda i: (0, 0))
    params = pltpu.CompilerParams(
        dimension_semantics=("parallel",),
        vmem_limit_bytes=_vmem_limit(_nbytes((tb, d), F32), _nbytes((tb, d), out_dtype)))
    if w_proj is None:
        return pl.pallas_call(
            _rmsnorm_body, grid=grid, in_specs=[x_spec, g_spec], out_specs=x_spec,
            out_shape=jax.ShapeDtypeStruct((t, d), out_dtype), compiler_params=params,
            name="rmsnorm")(x, g.reshape(1, d))
    n = w_proj.shape[0]
    return pl.pallas_call(
        _rmsnorm_proj_body, grid=grid,
        in_specs=[x_spec, g_spec, pl.BlockSpec((n, d), lambda i: (0, 0))],
        out_specs=[x_spec, pl.BlockSpec((tb, n), lambda i: (i, 0))],
        out_shape=[jax.ShapeDtypeStruct((t, d), BF16), jax.ShapeDtypeStruct((t, n), F32)],
        compiler_params=params, name="rmsnorm_proj")(x, g.reshape(1, d), w_proj)


def _ktiled_matmul_body(a_p, a_s, w_ref, res_p, res_s, ssq_p, ssq_s, o_p, o_s, *, n_prompt_blocks, width):
    i = pl.program_id(0)

    def run(a_ref, res_ref, ssq_ref, o_ref):
        def scaled_dot():
            ss = ssq_ref[:, 0:1]
            for j in range(1, ssq_ref.shape[1] // LANES):
                ss = ss + ssq_ref[:, j * LANES:j * LANES + 1]
            acc = jnp.dot(a_ref[...], w_ref[...], preferred_element_type=F32)
            return acc * (1.0 / (ss / width + EPS))

        @pl.when(pl.program_id(2) == 0)
        def _():
            o_ref[...] = res_ref[...] + scaled_dot()

        @pl.when(pl.program_id(2) > 0)
        def _():
            o_ref[...] += scaled_dot()

    @pl.when(i < n_prompt_blocks)
    def _():
        run(a_p, res_p, ssq_p, o_p)

    @pl.when(i == n_prompt_blocks)
    def _():
        run(a_s, res_s, ssq_s, o_s)


def _ktiled_matmul(a_p, a_s, w, res, ssq, width, *, bm, bn, bk, name):
    mp, ms = a_p.shape[0], a_s.shape[0]
    k_total, n_cols = w.shape
    assert mp % bm == 0 and n_cols % bn == 0 and k_total % bk == 0
    n_p, n_j, nk = mp // bm, n_cols // bn, k_total // bk

    def p_row(i):
        return jnp.minimum(i, n_p - 1)

    def p_col(i, j):
        return jnp.where(i < n_p, j, n_j - 1)

    def s_col(i, j):
        return jnp.where(i == n_p, j, 0)

    io_p = pl.BlockSpec((bm, bn), lambda i, j, k: (p_row(i), p_col(i, j)))
    io_s = pl.BlockSpec((ms, bn), lambda i, j, k: (0, s_col(i, j)))
    in_specs = [
        pl.BlockSpec((bm, bk), lambda i, j, k: (p_row(i), jnp.where(i < n_p, k, nk - 1))),
        pl.BlockSpec((ms, bk), lambda i, j, k: (0, jnp.where(i == n_p, k, 0))),
        pl.BlockSpec((bk, bn), lambda i, j, k: (k, j)),
        io_p, io_s,
        pl.BlockSpec((bm, ssq[0].shape[1]), lambda i, j, k: (p_row(i), 0)),
        pl.BlockSpec((ms, ssq[1].shape[1]), lambda i, j, k: (0, 0)),
    ]
    limit = _vmem_limit(_nbytes((bm + ms, bk), BF16), _nbytes((bk, bn), BF16), 2 * _nbytes((bm + ms, bn), F32),
                        _nbytes((bm + ms, ssq[0].shape[1]), F32))
    return pl.pallas_call(
        functools.partial(_ktiled_matmul_body, n_prompt_blocks=n_p, width=width),
        grid=(n_p + 1, n_j, nk), in_specs=in_specs, out_specs=[io_p, io_s],
        out_shape=[jax.ShapeDtypeStruct((mp, n_cols), F32), jax.ShapeDtypeStruct((ms, n_cols), F32)],
        compiler_params=pltpu.CompilerParams(
            dimension_semantics=("arbitrary", "arbitrary", "arbitrary"), vmem_limit_bytes=limit),
        name=name)(a_p, a_s, w, res[0], res[1], ssq[0], ssq[1])


def _ws_matmul_body(*refs, k_bounds, has_res, has_side, has_gain, act, n_prompt_blocks, chunk_rows,
                    transposed):
    refs = list(refs)
    n_lhs = len(k_bounds)
    take = lambda n: [refs.pop(0) for _ in range(n)]
    lhs_p, lhs_s = take(n_lhs), take(n_lhs)
    (wc_ref,) = take(1)
    res_p, res_s = take(2) if has_res else (None, None)
    (side_in,) = take(1) if has_side else (None,)
    (gain_ref,) = take(1) if has_gain else (None,)
    o_p, o_s = take(2)
    (side_out,) = take(1) if has_side else (None,)
    gained_p, gained_s, ssq_p, ssq_s = take(4) if has_gain else (None,) * 4
    (w_scr,) = take(1)
    jj, i = pl.program_id(0), pl.program_id(1)
    slot_load = jj % 2
    slot_use = 1 - slot_load

    def stage():
        r0 = pl.multiple_of(jnp.minimum(i, n_prompt_blocks - 1) * chunk_rows, chunk_rows)
        chunk = wc_ref[0] if len(wc_ref.shape) == 3 else wc_ref[...]
        w_scr[slot_load, pl.ds(r0, chunk_rows), :] = chunk.astype(BF16)

    def run(lhs_refs, res_ref, o_ref, gained_ref, ssq_ref):
        stage()
        acc = None
        for a_ref, (k0, k1) in zip(lhs_refs, k_bounds):
            if transposed:
                part = lax.dot_general(a_ref[...], w_scr[slot_use, :, k0:k1], (((1,), (1,)), ((), ())),
                                       preferred_element_type=F32)
            else:
                part = jnp.dot(a_ref[...], w_scr[slot_use, k0:k1, :], preferred_element_type=F32)
            acc = part if acc is None else acc + part
        if has_res:
            acc = res_ref[...] + acc
        if act == "relu2":
            acc = jnp.square(jnp.maximum(acc, 0.0))
        o_ref[...] = acc.astype(o_ref.dtype)
        if has_gain:
            gained_ref[...] = (acc * gain_ref[...]).astype(gained_ref.dtype)
            ssq_ref[...] = jnp.broadcast_to(jnp.sum(acc * acc, axis=1, keepdims=True), ssq_ref.shape)

    @pl.when(jj == 0)
    def _():
        stage()

    @pl.when((jj > 0) & (i < n_prompt_blocks))
    def _():
        if has_side:
            side_out[...] = side_in[...].astype(side_out.dtype)
        run(lhs_p, res_p, o_p, gained_p, ssq_p)

    @pl.when((jj > 0) & (i == n_prompt_blocks))
    def _():
        run(lhs_s, res_s, o_s, gained_s, ssq_s)


def _ws_matmul(lhs_p, lhs_s, w, layer, *, n_cols, transposed=False, w_row_gap=(0, 0), bm=512, bn=1024,
               res=None, act=None, out_dtype=F32, side_cast=None, norm_gain=None, name="ws_matmul"):
    mp, ms = lhs_p[0].shape[0], lhs_s[0].shape[0]
    k_total = w.shape[2] if transposed else w.shape[1]
    widths = [a.shape[1] for a in lhs_p]
    assert sum(widths) == k_total and mp % bm == 0 and n_cols % bn == 0
    n_p, n_j = mp // bm, n_cols // bn
    bounds, k0 = [], 0
    for wd in widths:
        bounds.append((k0, k0 + wd))
        k0 += wd

    def row_blk(jj, i):
        return jnp.where(jj == 0, 0, jnp.minimum(i, n_p - 1))

    def lhs_row_blk(jj, i):
        return jnp.where((jj == 0) | (i == n_p), 0, i)

    def col_use(jj):
        return jnp.maximum(jj - 1, 0)

    def chunk(jj, i):
        return jnp.where(jj < n_j, jnp.minimum(i, n_p - 1), n_p - 1)

    def col_load(jj):
        return jnp.minimum(jj, n_j - 1)

    if transposed:
        chunk_rows = bn // n_p
        gap_row, gap_rows = w_row_gap
        assert gap_row % bn == 0 and gap_rows % SUBLANES == 0

        def w_row(jj, i):
            row = (col_load(jj) * n_p + chunk(jj, i)) * chunk_rows
            return pl.multiple_of(row + jnp.where(row >= gap_row, gap_rows, 0), SUBLANES)

        w_spec = pl.BlockSpec((pl.Element(1), pl.Element(chunk_rows), pl.Element(k_total)),
                              lambda jj, i: (layer, w_row(jj, i), 0))
        scratch = pltpu.VMEM((2, bn, k_total), BF16)
    else:
        chunk_rows = k_total // n_p
        w_spec = pl.BlockSpec((None, chunk_rows, bn), lambda jj, i: (layer, chunk(jj, i), col_load(jj)))
        scratch = pltpu.VMEM((2, k_total, bn), BF16)
    assert chunk_rows * n_p == (bn if transposed else k_total) and chunk_rows % 16 == 0
    specs_p = [pl.BlockSpec((bm, wd), lambda jj, i: (lhs_row_blk(jj, i), 0)) for wd in widths]
    specs_s = [pl.BlockSpec((ms, wd), lambda jj, i: (0, 0)) for wd in widths]
    io_p = pl.BlockSpec((bm, bn), lambda jj, i: (row_blk(jj, i), col_use(jj)))
    io_s = pl.BlockSpec((ms, bn), lambda jj, i: (0, col_use(jj)))
    in_specs = specs_p + specs_s + [w_spec]
    args = list(lhs_p) + list(lhs_s) + [w]
    if res is not None:
        in_specs += [io_p, io_s]
        args += list(res)
    out_specs = [io_p, io_s]
    out_shape = [jax.ShapeDtypeStruct((mp, n_cols), out_dtype), jax.ShapeDtypeStruct((ms, n_cols), out_dtype)]
    side_bytes = 0
    if side_cast is not None:
        side, side_layer = side_cast
        side_rows, side_cols = side.shape[1], side.shape[2]
        rb = side_rows // (n_j * n_p)
        assert rb * n_j * n_p == side_rows and rb % 16 == 0

        def side_blk(jj, i):
            return jnp.where(jj == 0, 0, (jj - 1) * n_p + jnp.minimum(i, n_p - 1))

        in_specs.append(pl.BlockSpec((None, rb, side_cols), lambda jj, i: (side_layer, side_blk(jj, i), 0)))
        args.append(side)
        out_specs.append(pl.BlockSpec((rb, side_cols), lambda jj, i: (side_blk(jj, i), 0)))
        out_shape.append(jax.ShapeDtypeStruct((side_rows, side_cols), BF16))
        side_bytes = _nbytes((rb, side_cols), F32) + _nbytes((rb, side_cols), BF16)
    if norm_gain is not None:
        in_specs.append(pl.BlockSpec((1, bn), lambda jj, i: (0, col_use(jj))))
        args.append(norm_gain)
        out_specs += [io_p, io_s,
                      pl.BlockSpec((bm, LANES), lambda jj, i: (row_blk(jj, i), col_use(jj))),
                      pl.BlockSpec((ms, LANES), lambda jj, i: (0, col_use(jj)))]
        out_shape += [jax.ShapeDtypeStruct((mp, n_cols), BF16), jax.ShapeDtypeStruct((ms, n_cols), BF16),
                      jax.ShapeDtypeStruct((mp, n_j * LANES), F32), jax.ShapeDtypeStruct((ms, n_j * LANES), F32)]
        side_bytes += _nbytes((bm + ms, bn), BF16) + _nbytes((bm + ms, LANES), F32)
    body = functools.partial(_ws_matmul_body, k_bounds=tuple(bounds), has_res=res is not None,
                             has_side=side_cast is not None, has_gain=norm_gain is not None, act=act,
                             n_prompt_blocks=n_p, chunk_rows=chunk_rows, transposed=transposed)
    n_io = 2 if res is not None else 1
    pipelined = (_nbytes((bm + ms, k_total), BF16) + _nbytes((k_total // n_p, bn), F32)
                 + n_io * _nbytes((bm + ms, bn), F32) + side_bytes)
    limit = min(VMEM_CAP, 2 * pipelined + _nbytes((2, k_total, bn), BF16) + 2 * _nbytes((bm, bn), F32))
    return pl.pallas_call(
        body, grid=(n_j + 1, n_p + 1), in_specs=in_specs, out_specs=out_specs, out_shape=out_shape,
        scratch_shapes=[scratch],
        compiler_params=pltpu.CompilerParams(
            dimension_semantics=("arbitrary", "arbitrary"), vmem_limit_bytes=int(limit)),
        name=name)(*args)


def _linear_scan_rows(a, b, carry):
    t, lanes = b.shape
    g = t // SUBLANES
    b3 = b.reshape(g, SUBLANES, lanes)
    a3 = None if a is None else a.reshape(g, SUBLANES, lanes)
    sub = lax.broadcasted_iota(jnp.int32, b3.shape, 1)
    s = 1
    while s < SUBLANES:
        keep = sub >= s
        b_prev = jnp.where(keep, pltpu.roll(b3, s, 1), 0.0)
        if a3 is None:
            b3 = b3 + b_prev
        else:
            b3 = a3 * b_prev + b3
            a3 = a3 * jnp.where(keep, pltpu.roll(a3, s, 1), 1.0)
        s *= 2
    outs = []
    for j in range(g):
        h = b3[j] + carry if a3 is None else b3[j] + a3[j] * carry
        outs.append(h)
        carry = h[SUBLANES - 1:, :]
    return jnp.concatenate(outs, axis=0), carry


def _softplus(x):
    return jnp.maximum(x, 0.0) + jnp.log1p(jnp.exp(-jnp.abs(x)))


def _gla_body(q_ref, k_ref, v_ref, g_ref, alr_ref, wa_ref, ba_ref, gn_ref, s0_ref,
              o_ref, s_out_ref, st_ref, *, chunk, n_chunks, n_pairs):
    n = pl.program_id(2)

    @pl.when(n == 0)
    def _():
        for p in range(n_pairs):
            st_ref[p] = s0_ref[0, p].T

    c = chunk
    qw, vw = 2 * DK_A, 2 * DV_A
    lane = lax.broadcasted_iota(jnp.int32, (c, qw), 1)
    head0 = lane < DK_A
    row2 = lax.broadcasted_iota(jnp.int32, (2 * c, 2 * c), 0)
    col2 = lax.broadcasted_iota(jnp.int32, (2 * c, 2 * c), 1)
    causal = ((row2 >= c) == (col2 >= c)) & (row2 >= col2)
    nt = (((1,), (1,)), ((), ()))
    tn = (((0,), (0,)), ((), ()))

    def stack_heads(z):
        return jnp.concatenate([jnp.where(head0, z, 0.0), jnp.where(head0, 0.0, z)], axis=0).astype(BF16)

    for ci in range(n_chunks):
        rows = pl.ds(ci * c, c)
        z = jnp.dot(alr_ref[rows, :].astype(BF16), wa_ref[...], preferred_element_type=F32) + ba_ref[...]
        log_alpha = -(jnp.maximum(-z, 0.0) + jnp.log(1.0 + jnp.exp(-jnp.abs(z)))) / GLA_TAU
        b, b_end = _linear_scan_rows(None, log_alpha, jnp.zeros((1, log_alpha.shape[1]), F32))
        q = q_ref[rows, :] * (DK_A ** -0.5)
        k = k_ref[rows, :]
        q_dec = q * jnp.exp(b)
        k_inv = k * jnp.exp(-b)
        k_end = k * jnp.exp(b_end - b)
        decay = jnp.exp(b_end)
        for p in range(n_pairs):
            ql = slice(p * qw, (p + 1) * qw)
            q_t, k_t, k_e = stack_heads(q_dec[:, ql]), stack_heads(k_inv[:, ql]), stack_heads(k_end[:, ql])
            v0 = v_ref[rows, pl.ds(p * vw, DV_A)]
            v1 = v_ref[rows, pl.ds(p * vw + DV_A, DV_A)]
            v_s = jnp.concatenate([v0, v1], axis=0).astype(BF16)
            att = lax.dot_general(q_t, k_t, nt, preferred_element_type=F32)
            att = jnp.where(causal, att, 0.0).astype(BF16)
            st = st_ref[p]
            o = (jnp.dot(att, v_s, preferred_element_type=F32)
                 + lax.dot_general(q_t, st.astype(BF16), nt, preferred_element_type=F32))
            st_ref[p] = decay[:, ql] * st + lax.dot_general(v_s, k_e, tn, preferred_element_type=F32)
            ms = jnp.mean(jnp.square(o), axis=-1, keepdims=True)
            o = (o * lax.rsqrt(ms + EPS) * gn_ref[...])
            g0 = g_ref[rows, pl.ds(p * vw, DV_A)]
            g1 = g_ref[rows, pl.ds(p * vw + DV_A, DV_A)]
            o_ref[rows, pl.ds(p * vw, DV_A)] = (o[:c] * (g0 * jax.nn.sigmoid(g0))).astype(o_ref.dtype)
            o_ref[rows, pl.ds(p * vw + DV_A, DV_A)] = (o[c:] * (g1 * jax.nn.sigmoid(g1))).astype(o_ref.dtype)

    @pl.when(n == pl.num_programs(2) - 1)
    def _():
        for p in range(n_pairs):
            s_out_ref[0, p] = st_ref[p].T


def _gla(hproj, alr, wa, ba, gn, s0, *, n_seq, seq_len, col_q, col_k, col_v, col_g):
    t_total = hproj.shape[0]
    h_a = s0.shape[1]
    pairs = h_a // 2
    pps = GLA_PAIRS_PER_STEP
    assert pairs % pps == 0
    chunk = min(CHUNK, seq_len)
    n_chunks = min(8, seq_len // chunk)
    tb = chunk * n_chunks
    nb = seq_len // tb
    qw, vw = 2 * DK_A * pps, 2 * DV_A * pps
    assert col_q % qw == 0 and col_k % qw == 0 and col_v % vw == 0 and col_g % vw == 0
    cq, ck, cv, cg = col_q // qw, col_k // qw, col_v // vw, col_g // vw
    row = lambda s, n: s * nb + n
    in_specs = [
        pl.BlockSpec((tb, qw), lambda s, p, n: (row(s, n), cq + p)),
        pl.BlockSpec((tb, qw), lambda s, p, n: (row(s, n), ck + p)),
        pl.BlockSpec((tb, vw), lambda s, p, n: (row(s, n), cv + p)),
        pl.BlockSpec((tb, vw), lambda s, p, n: (row(s, n), cg + p)),
        pl.BlockSpec((tb, LANES), lambda s, p, n: (row(s, n), 0)),
        pl.BlockSpec((LANES, qw), lambda s, p, n: (0, p)),
        pl.BlockSpec((1, qw), lambda s, p, n: (0, p)),
        pl.BlockSpec((1, DV_A), lambda s, p, n: (0, 0)),
        pl.BlockSpec((1, pps, 2 * DK_A, DV_A), lambda s, p, n: (s, p, 0, 0)),
    ]
    out_specs = [
        pl.BlockSpec((tb, vw), lambda s, p, n: (row(s, n), p)),
        pl.BlockSpec((1, pps, 2 * DK_A, DV_A), lambda s, p, n: (s, p, 0, 0)),
    ]
    o_a, s_new = pl.pallas_call(
        functools.partial(_gla_body, chunk=chunk, n_chunks=n_chunks, n_pairs=pps),
        grid=(n_seq, pairs // pps, nb), in_specs=in_specs, out_specs=out_specs,
        out_shape=[jax.ShapeDtypeStruct((t_total, h_a * DV_A), BF16),
                   jax.ShapeDtypeStruct((n_seq, pairs, 2 * DK_A, DV_A), F32)],
        scratch_shapes=[pltpu.VMEM((pps, DV_A, 2 * DK_A), F32)],
        compiler_params=pltpu.CompilerParams(
            dimension_semantics=("parallel", "parallel", "arbitrary"),
            vmem_limit_bytes=_vmem_limit(_nbytes((tb, 2 * qw + 2 * vw + LANES), F32), _nbytes((tb, vw), BF16))),
        name="gla")(hproj, hproj, hproj, hproj, alr, wa, ba, gn, s0.reshape(n_seq, pairs, 2 * DK_A, DV_A))
    return o_a, s_new.reshape(n_seq, h_a, DK_A, DV_A)


def _gmlp_body(u_ref, v_ref, w_ref, b_ref, o_ref, *vg_refs, c, dh):
    tril = (lax.broadcasted_iota(jnp.int32, (c, c), 0) >= lax.broadcasted_iota(jnp.int32, (c, c), 1))
    w_tril = [jnp.where(tril, w_ref[h, :c, :c], 0.0).astype(BF16) for h in range(w_ref.shape[0])]
    for ci in range(u_ref.shape[0] // c):
        rows = pl.ds(ci * c, c)
        vg = jax.nn.gelu(v_ref[rows, :])
        if vg_refs:
            vg_refs[0][rows, :] = vg
        ug = jax.nn.gelu(u_ref[rows, :])
        for h, w in enumerate(w_tril):
            cols = pl.ds(h * dh, dh)
            mixed = (jnp.dot(w, vg[:, h * dh:(h + 1) * dh].astype(BF16), preferred_element_type=F32)
                     + b_ref[:c, cols])
            o_ref[rows, cols] = (ug[:, h * dh:(h + 1) * dh] * mixed).astype(o_ref.dtype)


def _gmlp(hproj, w_sp, b_full, *, seq_len, col_u, col_v, want_v):
    t_total = hproj.shape[0]
    n_h, gc, _ = w_sp.shape
    w_b = b_full.shape[1]
    dh = w_b // n_h
    c = min(gc, seq_len)
    tb = c * min(4, seq_len // c)
    blk = lambda col: pl.BlockSpec((tb, w_b), lambda r: (r, col // w_b))
    out_specs = [pl.BlockSpec((tb, w_b), lambda r: (r, 0))]
    out_shape = [jax.ShapeDtypeStruct((t_total, w_b), BF16)]
    if want_v:
        out_specs.append(pl.BlockSpec((tb, w_b), lambda r: (r, 0)))
        out_shape.append(jax.ShapeDtypeStruct((t_total, w_b), F32))
    outs = pl.pallas_call(
        functools.partial(_gmlp_body, c=c, dh=dh), grid=(t_total // tb,),
        in_specs=[blk(col_u), blk(col_v),
                  pl.BlockSpec((n_h, gc, gc), lambda r: (0, 0, 0)),
                  pl.BlockSpec((gc, w_b), lambda r: (0, 0))],
        out_specs=out_specs, out_shape=out_shape,
        compiler_params=pltpu.CompilerParams(
            dimension_semantics=("parallel",),
            vmem_limit_bytes=_vmem_limit(4 * _nbytes((tb, w_b), F32), _nbytes((n_h, gc, gc), F32))),
        name="gmlp")(hproj, hproj, w_sp, b_full)
    return (outs[0], outs[1]) if want_v else (outs[0], None)


def _lru_body(x_ref, gate_ref, cw_ref, cb_ref, wr_ref, br_ref, wi_ref, bi_ref, lam_ref, h0_ref, tail0_ref,
              o_ref, h_ref, tail_ref, *, dh):
    n = pl.program_id(1)

    @pl.when(n == 0)
    def _():
        h_ref[...] = h0_ref[...]
        tail_ref[...] = tail0_ref[...]

    tb = x_ref.shape[0]
    row8 = lax.broadcasted_iota(jnp.int32, (SUBLANES, dh), 0)
    for hb in range(x_ref.shape[1] // dh):
        cols = pl.ds(hb * dh, dh)
        x = x_ref[:, cols]
        tail = tail_ref[0, :, cols]

        def delayed(j):
            rolled = pltpu.roll(x, j, 0)
            head = jnp.where(row8 >= j, rolled[:SUBLANES], pltpu.roll(tail, j, 0))
            return jnp.concatenate([head, rolled[SUBLANES:]], axis=0)

        y = cb_ref[:, cols] + delayed(CONV_W - 1) * cw_ref[0:1, cols]
        for j in range(1, CONV_W - 1):
            y = y + delayed(CONV_W - 1 - j) * cw_ref[j:j + 1, cols]
        y = y + x * cw_ref[CONV_W - 1:CONV_W, cols]
        tail_ref[0, :, cols] = x[tb - SUBLANES:]

        yb = y.astype(BF16)
        r = jax.nn.sigmoid(jnp.dot(yb, wr_ref[hb].astype(BF16), preferred_element_type=F32) + br_ref[:, cols])
        ig = jax.nn.sigmoid(jnp.dot(yb, wi_ref[hb].astype(BF16), preferred_element_type=F32) + bi_ref[:, cols])
        log_a = r * (-LRU_C * _softplus(-lam_ref[:, cols]))
        a = jnp.exp(log_a)
        gain2 = -jnp.tanh(log_a) * (a * a + 1.0)
        bt = jnp.where(gain2 > 0.0, gain2 * lax.rsqrt(gain2), 0.0) * (ig * y)
        hseq, h_last = _linear_scan_rows(a, bt, h_ref[0, :, cols])
        h_ref[0, :, cols] = h_last
        o_ref[:, cols] = (hseq * jax.nn.gelu(gate_ref[:, cols])).astype(o_ref.dtype)


def _lru(hproj, cw, cb, wr, br, wi, bi, lam, h0, tail0, *, n_seq, seq_len, col_x, col_gate):
    t_total = hproj.shape[0]
    w_c = cw.shape[1]
    n_h, dh, _ = wr.shape
    tb = min(seq_len, 512)
    nb = seq_len // tb
    vec = lambda rows: pl.BlockSpec((rows, w_c), lambda s, n: (0, 0))
    wspec = pl.BlockSpec((n_h, dh, dh), lambda s, n: (0, 0, 0))
    hspec = pl.BlockSpec((1, 1, w_c), lambda s, n: (s, 0, 0))
    tspec = pl.BlockSpec((1, SUBLANES, w_c), lambda s, n: (s, 0, 0))
    o_c, h_new, tail = pl.pallas_call(
        functools.partial(_lru_body, dh=dh), grid=(n_seq, nb),
        in_specs=[pl.BlockSpec((tb, w_c), lambda s, n: (s * nb + n, col_x // w_c)),
                  pl.BlockSpec((tb, w_c), lambda s, n: (s * nb + n, col_gate // w_c)),
                  vec(CONV_W), vec(1), wspec, vec(1), wspec, vec(1), vec(1), hspec, tspec],
        out_specs=[pl.BlockSpec((tb, w_c), lambda s, n: (s * nb + n, 0)), hspec, tspec],
        out_shape=[jax.ShapeDtypeStruct((t_total, w_c), BF16),
                   jax.ShapeDtypeStruct((n_seq, 1, w_c), F32),
                   jax.ShapeDtypeStruct((n_seq, SUBLANES, w_c), F32)],
        compiler_params=pltpu.CompilerParams(
            dimension_semantics=("parallel", "arbitrary"),
            vmem_limit_bytes=_vmem_limit(4 * _nbytes((tb, w_c), F32))),
        name="conv_rglru")(hproj, hproj, cw, cb.reshape(1, w_c), wr, br.reshape(1, w_c), wi,
                           bi.reshape(1, w_c), lam.reshape(1, w_c), h0.reshape(n_seq, 1, w_c), tail0)
    return o_c, h_new.reshape(n_seq, w_c), tail


def _mixers(hproj, alr, s_gla, s_conv, s_lru, w, *, n_seq, seq_len, want_v):
    w_a = s_gla.shape[1] * DV_A
    qk_a = s_gla.shape[1] * DK_A
    w_b = w_c = s_lru.shape[1]
    col_u = 2 * qk_a + 2 * w_a
    o_a, s_gla_new = _gla(hproj, alr, w["w_alpha2"], w["b_alpha"], w["g_onorm"], s_gla, n_seq=n_seq,
                          seq_len=seq_len, col_q=0, col_k=qk_a, col_v=2 * qk_a, col_g=2 * qk_a + w_a)
    o_b, vg = _gmlp(hproj, w["w_spatial"], w["b_spatial"], seq_len=seq_len, col_u=col_u, col_v=col_u + w_b,
                    want_v=want_v)
    tail0 = jnp.pad(s_conv, ((0, 0), (SUBLANES - (CONV_W - 1), 0), (0, 0)))
    o_c, s_lru_new, tail = _lru(hproj, w["conv_w"], w["conv_b"], w["w_rgate"], w["b_rgate"], w["w_igate"],
                                w["b_igate"], w["lru_lambda"], s_lru, tail0, n_seq=n_seq, seq_len=seq_len,
                                col_x=col_u + 2 * w_b, col_gate=col_u + 2 * w_b + w_c)
    return [o_a, o_b, o_c], (s_gla_new, tail[:, SUBLANES - (CONV_W - 1):], s_lru_new, vg)


def _small_layer_weights(l, w_in_t, w_alpha2, b_alpha, g_onorm, w_spatial, b_spatial, conv_w, conv_b,
                         w_rgate, b_rgate, w_igate, b_igate, lru_lambda):
    d = w_in_t.shape[2]
    w_a, w_b = d // 2, d // 4
    qk_a = (w_a // DV_A) * DK_A
    c_alr = 2 * qk_a + 2 * w_a
    w_alr = jnp.pad(w_in_t[l, c_alr:c_alr + GLA_RANK, :], ((0, LANES - GLA_RANK), (0, 0)))
    w_alpha2_p = jnp.pad(w_alpha2[l], ((0, LANES - GLA_RANK), (0, 0))).astype(BF16)
    b_full = jnp.repeat(b_spatial[l].T, w_b // H_B, axis=1)
    return dict(
        w_alr=w_alr, w_alpha2=w_alpha2_p, b_alpha=b_alpha[l].reshape(1, qk_a),
        g_onorm=g_onorm[l].reshape(1, DV_A), w_spatial=w_spatial[l], b_spatial=b_full,
        conv_w=conv_w[l], conv_b=conv_b[l], w_rgate=w_rgate[l], b_rgate=b_rgate[l],
        w_igate=w_igate[l], b_igate=b_igate[l], lru_lambda=lru_lambda[l])


def kernel(x_prompt, x_sample, state_gla, state_conv, state_lru, ln1, w_in, w_alpha2, b_alpha, g_onorm, w_spatial, b_spatial, conv_w, conv_b, w_rgate, b_rgate, w_igate, b_igate, lru_lambda, w_out, ln2, w_up, w_down, ln_final):
    bp, tp, d = x_prompt.shape
    bs, ts, _ = x_sample.shape
    depth = w_in.shape[0]
    w_a, w_b, w_c = d // 2, d // 4, d // 4
    h_a = w_a // DV_A
    qk_a = h_a * DK_A
    n_a = 2 * qk_a + 2 * w_a
    n_b = 2 * w_b + 2 * w_c
    zero_gla = jnp.zeros((bp, h_a, DK_A, DV_A), F32)
    zero_conv = jnp.zeros((bp, CONV_W - 1, w_c), F32)
    zero_lru = jnp.zeros((bp, w_c), F32)
    xp = x_prompt.reshape(bp * tp, d)
    xs = x_sample.reshape(bs * ts, d)
    w_in_t = jnp.swapaxes(w_in, 1, 2)
    mm = _ws_matmul
    st_p, st_s = [], []
    for l in range(depth):
        w = _small_layer_weights(l, w_in_t, w_alpha2, b_alpha, g_onorm, w_spatial, b_spatial, conv_w, conv_b,
                                 w_rgate, b_rgate, w_igate, b_igate, lru_lambda)
        xn_p, alr_p = _rmsnorm(xp, ln1[l], BF16, w_proj=w["w_alr"])
        xn_s, alr_s = _rmsnorm(xs, ln1[l], BF16, w_proj=w["w_alr"])
        hp_p, hp_s = mm([xn_p], [xn_s], w_in_t, l, n_cols=n_a + n_b, transposed=True,
                        w_row_gap=(n_a, GLA_RANK), bm=1024, name="in_proj")
        mix_p, new_p = _mixers(hp_p, alr_p, zero_gla, zero_conv, zero_lru, w, n_seq=bp, seq_len=tp,
                               want_v=False)
        mix_s, new_s = _mixers(hp_s, alr_s, state_gla[l], state_conv[l], state_lru[l], w, n_seq=bs,
                               seq_len=ts, want_v=True)
        st_p.append(new_p)
        st_s.append(new_s)
        xp, xs, xg_p, xg_s, ssq_p, ssq_s = mm(mix_p, mix_s, w_out, l, n_cols=d, res=(xp, xs),
                                              norm_gain=ln2[l].reshape(1, d), name="out_proj")
        hid_p, hid_s, w_down_bf = mm([xg_p], [xg_s], w_up, l, n_cols=w_up.shape[2], act="relu2",
                                     out_dtype=BF16, side_cast=(w_down, l), bm=1024, name="mlp_up")
        xp, xs = _ktiled_matmul(hid_p, hid_s, w_down_bf, (xp, xs), (ssq_p, ssq_s), d, bm=1024, bn=1024,
                                bk=2048, name="mlp_down")
    y_prompt = _rmsnorm(xp, ln_final, F32).reshape(bp, tp, d)
    y_sample = _rmsnorm(xs, ln_final, F32).reshape(bs, ts, d)
    stack = lambda sts, i: jnp.stack([s[i] for s in sts])
    return (y_prompt, y_sample, stack(st_p, 0), stack(st_p, 1), stack(st_p, 2),
            stack(st_s, 0), stack(st_s, 1), stack(st_s, 2),
            jnp.stack([s[3].reshape(bs, ts, w_b) for s in st_s]))
```

```python
import functools

import jax
import jax.numpy as jnp
from jax import lax
from jax.experimental import pallas as pl
from jax.experimental.pallas import tpu as pltpu

F32 = jnp.float32
BF16 = jnp.bfloat16

DV_A = 128
DK_A = 64
GLA_RANK = 16
GLA_TAU = 16.0
CHUNK = 64
GMLP_CHUNK = 128
H_B = 8
H_C = 8
CONV_W = 4
LRU_C = 8.0
EPS = 1e-6

LANES = 128
SUBLANES = 8
V7X_VMEM_BYTES = 64 * 1024 * 1024
VMEM_CAP = V7X_VMEM_BYTES - 6 * 1024 * 1024

GLA_PAIRS_PER_STEP = 8


def _vmem_limit(*block_bytes):
    need = 4 * sum(block_bytes)
    return int(min(max(need, 16 * 1024 * 1024), VMEM_CAP))


def _nbytes(shape, dtype):
    n = 1
    for s in shape:
        n *= s
    return n * jnp.dtype(dtype).itemsize


def _rmsnorm_body(x_ref, g_ref, o_ref):
    x = x_ref[...]
    ms = jnp.mean(jnp.square(x), axis=-1, keepdims=True)
    o_ref[...] = (x * lax.rsqrt(ms + EPS) * g_ref[...]).astype(o_ref.dtype)


def _rmsnorm_proj_body(x_ref, g_ref, w_ref, o_ref, p_ref):
    x = x_ref[...]
    ms = jnp.mean(jnp.square(x), axis=-1, keepdims=True)
    y = (x * lax.rsqrt(ms + EPS) * g_ref[...]).astype(BF16)
    o_ref[...] = y
    p_ref[...] = lax.dot_general(y, w_ref[...].astype(BF16), (((1,), (1,)), ((), ())),
                                 preferred_element_type=F32)


def _rmsnorm(x, g, out_dtype, w_proj=None):
    t, d = x.shape
    tb = min(t, 512)
    grid = (t // tb,)
    x_spec = pl.BlockSpec((tb, d), lambda i: (i, 0))
    g_spec = pl.BlockSpec((1, d), lambda i: (0, 0))
    params = pltpu.CompilerParams(
        dimension_semantics=("parallel",),
        vmem_limit_bytes=_vmem_limit(_nbytes((tb, d), F32), _nbytes((tb, d), out_dtype)))
    if w_proj is None:
        return pl.pallas_call(
            _rmsnorm_body, grid=grid, in_specs=[x_spec, g_spec], out_specs=x_spec,
            out_shape=jax.ShapeDtypeStruct((t, d), out_dtype), compiler_params=params,
            name="rmsnorm")(x, g.reshape(1, d))
    n = w_proj.shape[0]
    return pl.pallas_call(
        _rmsnorm_proj_body, grid=grid,
        in_specs=[x_spec, g_spec, pl.BlockSpec((n, d), lambda i: (0, 0))],
        out_specs=[x_spec, pl.BlockSpec((tb, n), lambda i: (i, 0))],
        out_shape=[jax.ShapeDtypeStruct((t, d), BF16), jax.ShapeDtypeStruct((t, n), F32)],
        compiler_params=params, name="rmsnorm_proj")(x, g.reshape(1, d), w_proj)


def _ktiled_matmul_body(a_p, a_s, w_ref, res_p, res_s, ssq_p, ssq_s, o_p, o_s, *, n_prompt_blocks, width):
    i = pl.program_id(0)

    def run(a_ref, res_ref, ssq_ref, o_ref):
        def scaled_dot():
            ss = ssq_ref[:, 0:1]
            for j in range(1, ssq_ref.shape[1] // LANES):
                ss = ss + ssq_ref[:, j * LANES:j * LANES + 1]
            acc = jnp.dot(a_ref[...], w_ref[...], preferred_element_type=F32)
            return acc * (1.0 / (ss / width + EPS))

        @pl.when(pl.program_id(2) == 0)
        def _():
            o_ref[...] = res_ref[...] + scaled_dot()

        @pl.when(pl.program_id(2) > 0)
        def _():
            o_ref[...] += scaled_dot()

    @pl.when(i < n_prompt_blocks)
    def _():
        run(a_p, res_p, ssq_p, o_p)

    @pl.when(i == n_prompt_blocks)
    def _():
        run(a_s, res_s, ssq_s, o_s)


def _ktiled_matmul(a_p, a_s, w, res, ssq, width, *, bm, bn, bk, name):
    mp, ms = a_p.shape[0], a_s.shape[0]
    k_total, n_cols = w.shape
    assert mp % bm == 0 and n_cols % bn == 0 and k_total % bk == 0
    n_p, n_j, nk = mp // bm, n_cols // bn, k_total // bk

    def p_row(i):
        return jnp.minimum(i, n_p - 1)

    def p_col(i, j):
        return jnp.where(i < n_p, j, n_j - 1)

    def s_col(i, j):
        return jnp.where(i == n_p, j, 0)

    io_p = pl.BlockSpec((bm, bn), lambda i, j, k: (p_row(i), p_col(i, j)))
    io_s = pl.BlockSpec((ms, bn), lambda i, j, k: (0, s_col(i, j)))
    in_specs = [
        pl.BlockSpec((bm, bk), lambda i, j, k: (p_row(i), jnp.where(i < n_p, k, nk - 1))),
        pl.BlockSpec((ms, bk), lambda i, j, k: (0, jnp.where(i == n_p, k, 0))),
        pl.BlockSpec((bk, bn), lambda i, j, k: (k, j)),
        io_p, io_s,
        pl.BlockSpec((bm, ssq[0].shape[1]), lambda i, j, k: (p_row(i), 0)),
        pl.BlockSpec((ms, ssq[1].shape[1]), lambda i, j, k: (0, 0)),
    ]
    limit = _vmem_limit(_nbytes((bm + ms, bk), BF16), _nbytes((bk, bn), BF16), 2 * _nbytes((bm + ms, bn), F32),
                        _nbytes((bm + ms, ssq[0].shape[1]), F32))
    return pl.pallas_call(
        functools.partial(_ktiled_matmul_body, n_prompt_blocks=n_p, width=width),
        grid=(n_p + 1, n_j, nk), in_specs=in_specs, out_specs=[io_p, io_s],
        out_shape=[jax.ShapeDtypeStruct((mp, n_cols), F32), jax.ShapeDtypeStruct((ms, n_cols), F32)],
        compiler_params=pltpu.CompilerParams(
            dimension_semantics=("arbitrary", "arbitrary", "arbitrary"), vmem_limit_bytes=limit),
        name=name)(a_p, a_s, w, res[0], res[1], ssq[0], ssq[1])


def _ws_matmul_body(*refs, k_bounds, has_res, has_side, has_gain, act, n_prompt_blocks, chunk_rows,
                    transposed):
    refs = list(refs)
    n_lhs = len(k_bounds)
    take = lambda n: [refs.pop(0) for _ in range(n)]
    lhs_p, lhs_s = take(n_lhs), take(n_lhs)
    (wc_ref,) = take(1)
    res_p, res_s = take(2) if has_res else (None, None)
    (side_in,) = take(1) if has_side else (None,)
    (gain_ref,) = take(1) if has_gain else (None,)
    o_p, o_s = take(2)
    (side_out,) = take(1) if has_side else (None,)
    gained_p, gained_s, ssq_p, ssq_s = take(4) if has_gain else (None,) * 4
    (w_scr,) = take(1)
    jj, i = pl.program_id(0), pl.program_id(1)
    slot_load = jj % 2
    slot_use = 1 - slot_load

    def stage():
        r0 = pl.multiple_of(jnp.minimum(i, n_prompt_blocks - 1) * chunk_rows, chunk_rows)
        chunk = wc_ref[0] if len(wc_ref.shape) == 3 else wc_ref[...]
        w_scr[slot_load, pl.ds(r0, chunk_rows), :] = chunk.astype(BF16)

    def run(lhs_refs, res_ref, o_ref, gained_ref, ssq_ref):
        stage()
        acc = None
        for a_ref, (k0, k1) in zip(lhs_refs, k_bounds):
            if transposed:
                part = lax.dot_general(a_ref[...], w_scr[slot_use, :, k0:k1], (((1,), (1,)), ((), ())),
                                       preferred_element_type=F32)
            else:
                part = jnp.dot(a_ref[...], w_scr[slot_use, k0:k1, :], preferred_element_type=F32)
            acc = part if acc is None else acc + part
        if has_res:
            acc = res_ref[...] + acc
        if act == "relu2":
            acc = jnp.square(jnp.maximum(acc, 0.0))
        o_ref[...] = acc.astype(o_ref.dtype)
        if has_gain:
            gained_ref[...] = (acc * gain_ref[...]).astype(gained_ref.dtype)
            ssq_ref[...] = jnp.broadcast_to(jnp.sum(acc * acc, axis=1, keepdims=True), ssq_ref.shape)

    @pl.when(jj == 0)
    def _():
        stage()

    @pl.when((jj > 0) & (i < n_prompt_blocks))
    def _():
        if has_side:
            side_out[...] = side_in[...].astype(side_out.dtype)
        run(lhs_p, res_p, o_p, gained_p, ssq_p)

    @pl.when((jj > 0) & (i == n_prompt_blocks))
    def _():
        run(lhs_s, res_s, o_s, gained_s, ssq_s)


def _ws_matmul(lhs_p, lhs_s, w, layer, *, n_cols, transposed=False, w_row_gap=(0, 0), bm=512, bn=1024,
               res=None, act=None, out_dtype=F32, side_cast=None, norm_gain=None, name="ws_matmul"):
    mp, ms = lhs_p[0].shape[0], lhs_s[0].shape[0]
    k_total = w.shape[2] if transposed else w.shape[1]
    widths = [a.shape[1] for a in lhs_p]
    assert sum(widths) == k_total and mp % bm == 0 and n_cols % bn == 0
    n_p, n_j = mp // bm, n_cols // bn
    bounds, k0 = [], 0
    for wd in widths:
        bounds.append((k0, k0 + wd))
        k0 += wd

    def row_blk(jj, i):
        return jnp.where(jj == 0, 0, jnp.minimum(i, n_p - 1))

    def lhs_row_blk(jj, i):
        return jnp.where((jj == 0) | (i == n_p), 0, i)

    def col_use(jj):
        return jnp.maximum(jj - 1, 0)

    def chunk(jj, i):
        return jnp.where(jj < n_j, jnp.minimum(i, n_p - 1), n_p - 1)

    def col_load(jj):
        return jnp.minimum(jj, n_j - 1)

    if transposed:
        chunk_rows = bn // n_p
        gap_row, gap_rows = w_row_gap
        assert gap_row % bn == 0 and gap_rows % SUBLANES == 0

        def w_row(jj, i):
            row = (col_load(jj) * n_p + chunk(jj, i)) * chunk_rows
            return pl.multiple_of(row + jnp.where(row >= gap_row, gap_rows, 0), SUBLANES)

        w_spec = pl.BlockSpec((pl.Element(1), pl.Element(chunk_rows), pl.Element(k_total)),
                              lambda jj, i: (layer, w_row(jj, i), 0))
        scratch = pltpu.VMEM((2, bn, k_total), BF16)
    else:
        chunk_rows = k_total // n_p
        w_spec = pl.BlockSpec((None, chunk_rows, bn), lambda jj, i: (layer, chunk(jj, i), col_load(jj)))
        scratch = pltpu.VMEM((2, k_total, bn), BF16)
    assert chunk_rows * n_p == (bn if transposed else k_total) and chunk_rows % 16 == 0
    specs_p = [pl.BlockSpec((bm, wd), lambda jj, i: (lhs_row_blk(jj, i), 0)) for wd in widths]
    specs_s = [pl.BlockSpec((ms, wd), lambda jj, i: (0, 0)) for wd in widths]
    io_p = pl.BlockSpec((bm, bn), lambda jj, i: (row_blk(jj, i), col_use(jj)))
    io_s = pl.BlockSpec((ms, bn), lambda jj, i: (0, col_use(jj)))
    in_specs = specs_p + specs_s + [w_spec]
    args = list(lhs_p) + list(lhs_s) + [w]
    if res is not None:
        in_specs += [io_p, io_s]
        args += list(res)
    out_specs = [io_p, io_s]
    out_shape = [jax.ShapeDtypeStruct((mp, n_cols), out_dtype), jax.ShapeDtypeStruct((ms, n_cols), out_dtype)]
    side_bytes = 0
    if side_cast is not None:
        side, side_layer = side_cast
        side_rows, side_cols = side.shape[1], side.shape[2]
        rb = side_rows // (n_j * n_p)
        assert rb * n_j * n_p == side_rows and rb % 16 == 0

        def side_blk(jj, i):
            return jnp.where(jj == 0, 0, (jj - 1) * n_p + jnp.minimum(i, n_p - 1))

        in_specs.append(pl.BlockSpec((None, rb, side_cols), lambda jj, i: (side_layer, side_blk(jj, i), 0)))
        args.append(side)
        out_specs.append(pl.BlockSpec((rb, side_cols), lambda jj, i: (side_blk(jj, i), 0)))
        out_shape.append(jax.ShapeDtypeStruct((side_rows, side_cols), BF16))
        side_bytes = _nbytes((rb, side_cols), F32) + _nbytes((rb, side_cols), BF16)
    if norm_gain is not None:
        in_specs.append(pl.BlockSpec((1, bn), lambda jj, i: (0, col_use(jj))))
        args.append(norm_gain)
        out_specs += [io_p, io_s,
                      pl.BlockSpec((bm, LANES), lambda jj, i: (row_blk(jj, i), col_use(jj))),
                      pl.BlockSpec((ms, LANES), lambda jj, i: (0, col_use(jj)))]
        out_shape += [jax.ShapeDtypeStruct((mp, n_cols), BF16), jax.ShapeDtypeStruct((ms, n_cols), BF16),
                      jax.ShapeDtypeStruct((mp, n_j * LANES), F32), jax.ShapeDtypeStruct((ms, n_j * LANES), F32)]
        side_bytes += _nbytes((bm + ms, bn), BF16) + _nbytes((bm + ms, LANES), F32)
    body = functools.partial(_ws_matmul_body, k_bounds=tuple(bounds), has_res=res is not None,
                             has_side=side_cast is not None, has_gain=norm_gain is not None, act=act,
                             n_prompt_blocks=n_p, chunk_rows=chunk_rows, transposed=transposed)
    n_io = 2 if res is not None else 1
    pipelined = (_nbytes((bm + ms, k_total), BF16) + _nbytes((k_total // n_p, bn), F32)
                 + n_io * _nbytes((bm + ms, bn), F32) + side_bytes)
    limit = min(VMEM_CAP, 2 * pipelined + _nbytes((2, k_total, bn), BF16) + 2 * _nbytes((bm, bn), F32))
    return pl.pallas_call(
        body, grid=(n_j + 1, n_p + 1), in_specs=in_specs, out_specs=out_specs, out_shape=out_shape,
        scratch_shapes=[scratch],
        compiler_params=pltpu.CompilerParams(
            dimension_semantics=("arbitrary", "arbitrary"), vmem_limit_bytes=int(limit)),
        name=name)(*args)


def _linear_scan_rows(a, b, carry):
    t, lanes = b.shape
    g = t // SUBLANES
    b3 = b.reshape(g, SUBLANES, lanes)
    a3 = None if a is None else a.reshape(g, SUBLANES, lanes)
    sub = lax.broadcasted_iota(jnp.int32, b3.shape, 1)
    s = 1
    while s < SUBLANES:
        keep = sub >= s
        b_prev = jnp.where(keep, pltpu.roll(b3, s, 1), 0.0)
        if a3 is None:
            b3 = b3 + b_prev
        else:
            b3 = a3 * b_prev + b3
            a3 = a3 * jnp.where(keep, pltpu.roll(a3, s, 1), 1.0)
        s *= 2
    outs = []
    for j in range(g):
        h = b3[j] + carry if a3 is None else b3[j] + a3[j] * carry
        outs.append(h)
        carry = h[SUBLANES - 1:, :]
    return jnp.concatenate(outs, axis=0), carry


def _softplus(x):
    return jnp.maximum(x, 0.0) + jnp.log1p(jnp.exp(-jnp.abs(x)))


def _gla_body(q_ref, k_ref, v_ref, g_ref, alr_ref, wa_ref, ba_ref, gn_ref, s0_ref,
              o_ref, s_out_ref, st_ref, *, chunk, n_chunks, n_pairs):
    n = pl.program_id(2)

    @pl.when(n == 0)
    def _():
        for p in range(n_pairs):
            st_ref[p] = s0_ref[0, p].T

    c = chunk
    qw, vw = 2 * DK_A, 2 * DV_A
    lane = lax.broadcasted_iota(jnp.int32, (c, qw), 1)
    head0 = lane < DK_A
    row2 = lax.broadcasted_iota(jnp.int32, (2 * c, 2 * c), 0)
    col2 = lax.broadcasted_iota(jnp.int32, (2 * c, 2 * c), 1)
    causal = ((row2 >= c) == (col2 >= c)) & (row2 >= col2)
    nt = (((1,), (1,)), ((), ()))
    tn = (((0,), (0,)), ((), ()))

    def stack_heads(z):
        return jnp.concatenate([jnp.where(head0, z, 0.0), jnp.where(head0, 0.0, z)], axis=0).astype(BF16)

    for ci in range(n_chunks):
        rows = pl.ds(ci * c, c)
        z = jnp.dot(alr_ref[rows, :].astype(BF16), wa_ref[...], preferred_element_type=F32) + ba_ref[...]
        log_alpha = -(jnp.maximum(-z, 0.0) + jnp.log(1.0 + jnp.exp(-jnp.abs(z)))) / GLA_TAU
        b, b_end = _linear_scan_rows(None, log_alpha, jnp.zeros((1, log_alpha.shape[1]), F32))
        q = q_ref[rows, :] * (DK_A ** -0.5)
        k = k_ref[rows, :]
        q_dec = q * jnp.exp(b)
        k_inv = k * jnp.exp(-b)
        k_end = k * jnp.exp(b_end - b)
        decay = jnp.exp(b_end)
        for p in range(n_pairs):
            ql = slice(p * qw, (p + 1) * qw)
            q_t, k_t, k_e = stack_heads(q_dec[:, ql]), stack_heads(k_inv[:, ql]), stack_heads(k_end[:, ql])
            v0 = v_ref[rows, pl.ds(p * vw, DV_A)]
            v1 = v_ref[rows, pl.ds(p * vw + DV_A, DV_A)]
            v_s = jnp.concatenate([v0, v1], axis=0).astype(BF16)
            att = lax.dot_general(q_t, k_t, nt, preferred_element_type=F32)
            att = jnp.where(causal, att, 0.0).astype(BF16)
            st = st_ref[p]
            o = (jnp.dot(att, v_s, preferred_element_type=F32)
                 + lax.dot_general(q_t, st.astype(BF16), nt, preferred_element_type=F32))
            st_ref[p] = decay[:, ql] * st + lax.dot_general(v_s, k_e, tn, preferred_element_type=F32)
            ms = jnp.mean(jnp.square(o), axis=-1, keepdims=True)
            o = (o * lax.rsqrt(ms + EPS) * gn_ref[...])
            g0 = g_ref[rows, pl.ds(p * vw, DV_A)]
            g1 = g_ref[rows, pl.ds(p * vw + DV_A, DV_A)]
            o_ref[rows, pl.ds(p * vw, DV_A)] = (o[:c] * (g0 * jax.nn.sigmoid(g0))).astype(o_ref.dtype)
            o_ref[rows, pl.ds(p * vw + DV_A, DV_A)] = (o[c:] * (g1 * jax.nn.sigmoid(g1))).astype(o_ref.dtype)

    @pl.when(n == pl.num_programs(2) - 1)
    def _():
        for p in range(n_pairs):
            s_out_ref[0, p] = st_ref[p].T


def _gla(hproj, alr, wa, ba, gn, s0, *, n_seq, seq_len, col_q, col_k, col_v, col_g):
    t_total = hproj.shape[0]
    h_a = s0.shape[1]
    pairs = h_a // 2
    pps = GLA_PAIRS_PER_STEP
    assert pairs % pps == 0
    chunk = min(CHUNK, seq_len)
    n_chunks = min(8, seq_len // chunk)
    tb = chunk * n_chunks
    nb = seq_len // tb
    qw, vw = 2 * DK_A * pps, 2 * DV_A * pps
    assert col_q % qw == 0 and col_k % qw == 0 and col_v % vw == 0 and col_g % vw == 0
    cq, ck, cv, cg = col_q // qw, col_k // qw, col_v // vw, col_g // vw
    row = lambda s, n: s * nb + n
    in_specs = [
        pl.BlockSpec((tb, qw), lambda s, p, n: (row(s, n), cq + p)),
        pl.BlockSpec((tb, qw), lambda s, p, n: (row(s, n), ck + p)),
        pl.BlockSpec((tb, vw), lambda s, p, n: (row(s, n), cv + p)),
        pl.BlockSpec((tb, vw), lambda s, p, n: (row(s, n), cg + p)),
        pl.BlockSpec((tb, LANES), lambda s, p, n: (row(s, n), 0)),
        pl.BlockSpec((LANES, qw), lambda s, p, n: (0, p)),
        pl.BlockSpec((1, qw), lambda s, p, n: (0, p)),
        pl.BlockSpec((1, DV_A), lambda s, p, n: (0, 0)),
        pl.BlockSpec((1, pps, 2 * DK_A, DV_A), lambda s, p, n: (s, p, 0, 0)),
    ]
    out_specs = [
        pl.BlockSpec((tb, vw), lambda s, p, n: (row(s, n), p)),
        pl.BlockSpec((1, pps, 2 * DK_A, DV_A), lambda s, p, n: (s, p, 0, 0)),
    ]
    o_a, s_new = pl.pallas_call(
        functools.partial(_gla_body, chunk=chunk, n_chunks=n_chunks, n_pairs=pps),
        grid=(n_seq, pairs // pps, nb), in_specs=in_specs, out_specs=out_specs,
        out_shape=[jax.ShapeDtypeStruct((t_total, 2 * h_a * DV_A), BF16),
                   jax.ShapeDtypeStruct((n_seq, pairs, 2 * DK_A, DV_A), F32)],
        scratch_shapes=[pltpu.VMEM((pps, DV_A, 2 * DK_A), F32)],
        compiler_params=pltpu.CompilerParams(
            dimension_semantics=("parallel", "parallel", "arbitrary"),
            vmem_limit_bytes=_vmem_limit(_nbytes((tb, 2 * qw + 2 * vw + LANES), F32), _nbytes((tb, vw), BF16))),
        name="gla")(hproj, hproj, hproj, hproj, alr, wa, ba, gn, s0.reshape(n_seq, pairs, 2 * DK_A, DV_A))
    return o_a, s_new.reshape(n_seq, h_a, DK_A, DV_A)


def _gmlp_body(u_ref, v_ref, w_ref, b_ref, mix_ref, o_ref, *vg_refs, c, dh):
    tril = (lax.broadcasted_iota(jnp.int32, (c, c), 0) >= lax.broadcasted_iota(jnp.int32, (c, c), 1))
    w_tril = [jnp.where(tril, w_ref[h, :c, :c], 0.0).astype(BF16) for h in range(w_ref.shape[0])]
    for ci in range(u_ref.shape[0] // c):
        rows = pl.ds(ci * c, c)
        vg = jax.nn.gelu(v_ref[rows, :])
        if vg_refs:
            vg_refs[0][rows, :] = vg
        ug = jax.nn.gelu(u_ref[rows, :])
        for h, w in enumerate(w_tril):
            cols = pl.ds(h * dh, dh)
            mixed = (jnp.dot(w, vg[:, h * dh:(h + 1) * dh].astype(BF16), preferred_element_type=F32)
                     + b_ref[:c, cols])
            o_ref[rows, cols] = (ug[:, h * dh:(h + 1) * dh] * mixed).astype(o_ref.dtype)


def _gmlp(hproj, w_sp, b_full, mix, mix_col, *, seq_len, col_u, col_v, want_v):
    t_total = hproj.shape[0]
    n_h, gc, _ = w_sp.shape
    w_b = b_full.shape[1]
    dh = w_b // n_h
    c = min(gc, seq_len)
    tb = c * min(4, seq_len // c)
    blk = lambda col: pl.BlockSpec((tb, w_b), lambda r: (r, col // w_b))
    out_specs = [pl.BlockSpec((tb, w_b), lambda r: (r, mix_col // w_b))]
    out_shape = [jax.ShapeDtypeStruct(mix.shape, mix.dtype)]
    if want_v:
        out_specs.append(pl.BlockSpec((tb, w_b), lambda r: (r, 0)))
        out_shape.append(jax.ShapeDtypeStruct((t_total, w_b), F32))
    outs = pl.pallas_call(
        functools.partial(_gmlp_body, c=c, dh=dh), grid=(t_total // tb,),
        in_specs=[blk(col_u), blk(col_v),
                  pl.BlockSpec((n_h, gc, gc), lambda r: (0, 0, 0)),
                  pl.BlockSpec((gc, w_b), lambda r: (0, 0)),
                  pl.BlockSpec(memory_space=pl.ANY)],
        out_specs=out_specs, out_shape=out_shape, input_output_aliases={4: 0},
        compiler_params=pltpu.CompilerParams(
            dimension_semantics=("parallel",),
            vmem_limit_bytes=_vmem_limit(4 * _nbytes((tb, w_b), F32), _nbytes((n_h, gc, gc), F32))),
        name="gmlp")(hproj, hproj, w_sp, b_full, mix)
    return (outs[0], outs[1]) if want_v else (outs[0], None)


def _lru_body(x_ref, gate_ref, cw_ref, cb_ref, wr_ref, br_ref, wi_ref, bi_ref, lam_ref, h0_ref, tail0_ref,
              mix_ref, o_ref, h_ref, tail_ref, *, dh):
    n = pl.program_id(1)

    @pl.when(n == 0)
    def _():
        h_ref[...] = h0_ref[...]
        tail_ref[...] = tail0_ref[...]

    tb = x_ref.shape[0]
    row8 = lax.broadcasted_iota(jnp.int32, (SUBLANES, dh), 0)
    for hb in range(x_ref.shape[1] // dh):
        cols = pl.ds(hb * dh, dh)
        x = x_ref[:, cols]
        tail = tail_ref[0, :, cols]

        def delayed(j):
            rolled = pltpu.roll(x, j, 0)
            head = jnp.where(row8 >= j, rolled[:SUBLANES], pltpu.roll(tail, j, 0))
            return jnp.concatenate([head, rolled[SUBLANES:]], axis=0)

        y = cb_ref[:, cols] + delayed(CONV_W - 1) * cw_ref[0:1, cols]
        for j in range(1, CONV_W - 1):
            y = y + delayed(CONV_W - 1 - j) * cw_ref[j:j + 1, cols]
        y = y + x * cw_ref[CONV_W - 1:CONV_W, cols]
        tail_ref[0, :, cols] = x[tb - SUBLANES:]

        yb = y.astype(BF16)
        r = jax.nn.sigmoid(jnp.dot(yb, wr_ref[hb].astype(BF16), preferred_element_type=F32) + br_ref[:, cols])
        ig = jax.nn.sigmoid(jnp.dot(yb, wi_ref[hb].astype(BF16), preferred_element_type=F32) + bi_ref[:, cols])
        log_a = r * (-LRU_C * _softplus(-lam_ref[:, cols]))
        a = jnp.exp(log_a)
        gain2 = -jnp.tanh(log_a) * (a * a + 1.0)
        bt = jnp.where(gain2 > 0.0, gain2 * lax.rsqrt(gain2), 0.0) * (ig * y)
        hseq, h_last = _linear_scan_rows(a, bt, h_ref[0, :, cols])
        h_ref[0, :, cols] = h_last
        o_ref[:, cols] = (hseq * jax.nn.gelu(gate_ref[:, cols])).astype(o_ref.dtype)


def _lru(hproj, cw, cb, wr, br, wi, bi, lam, h0, tail0, mix, mix_col, *, n_seq, seq_len, col_x, col_gate):
    t_total = hproj.shape[0]
    w_c = cw.shape[1]
    n_h, dh, _ = wr.shape
    tb = min(seq_len, 512)
    nb = seq_len // tb
    vec = lambda rows: pl.BlockSpec((rows, w_c), lambda s, n: (0, 0))
    wspec = pl.BlockSpec((n_h, dh, dh), lambda s, n: (0, 0, 0))
    hspec = pl.BlockSpec((1, 1, w_c), lambda s, n: (s, 0, 0))
    tspec = pl.BlockSpec((1, SUBLANES, w_c), lambda s, n: (s, 0, 0))
    o_c, h_new, tail = pl.pallas_call(
        functools.partial(_lru_body, dh=dh), grid=(n_seq, nb),
        in_specs=[pl.BlockSpec((tb, w_c), lambda s, n: (s * nb + n, col_x // w_c)),
                  pl.BlockSpec((tb, w_c), lambda s, n: (s * nb + n, col_gate // w_c)),
                  vec(CONV_W), vec(1), wspec, vec(1), wspec, vec(1), vec(1), hspec, tspec,
                  pl.BlockSpec(memory_space=pl.ANY)],
        out_specs=[pl.BlockSpec((tb, w_c), lambda s, n: (s * nb + n, mix_col // w_c)), hspec, tspec],
        input_output_aliases={11: 0},
        out_shape=[jax.ShapeDtypeStruct(mix.shape, mix.dtype),
                   jax.ShapeDtypeStruct((n_seq, 1, w_c), F32),
                   jax.ShapeDtypeStruct((n_seq, SUBLANES, w_c), F32)],
        compiler_params=pltpu.CompilerParams(
            dimension_semantics=("parallel", "arbitrary"),
            vmem_limit_bytes=_vmem_limit(4 * _nbytes((tb, w_c), F32))),
        name="conv_rglru")(hproj, hproj, cw, cb.reshape(1, w_c), wr, br.reshape(1, w_c), wi,
                           bi.reshape(1, w_c), lam.reshape(1, w_c), h0.reshape(n_seq, 1, w_c), tail0, mix)
    return o_c, h_new.reshape(n_seq, w_c), tail


def _mixers(hproj, alr, s_gla, s_conv, s_lru, w, *, n_seq, seq_len, want_v):
    w_a = s_gla.shape[1] * DV_A
    qk_a = s_gla.shape[1] * DK_A
    w_b = w_c = s_lru.shape[1]
    col_u = 2 * qk_a + 2 * w_a
    mix, s_gla_new = _gla(hproj, alr, w["w_alpha2"], w["b_alpha"], w["g_onorm"], s_gla, n_seq=n_seq,
                          seq_len=seq_len, col_q=0, col_k=qk_a, col_v=2 * qk_a, col_g=2 * qk_a + w_a)
    mix, vg = _gmlp(hproj, w["w_spatial"], w["b_spatial"], mix, w_a, seq_len=seq_len, col_u=col_u,
                    col_v=col_u + w_b, want_v=want_v)
    tail0 = jnp.pad(s_conv, ((0, 0), (SUBLANES - (CONV_W - 1), 0), (0, 0)))
    mix, s_lru_new, tail = _lru(hproj, w["conv_w"], w["conv_b"], w["w_rgate"], w["b_rgate"], w["w_igate"],
                                w["b_igate"], w["lru_lambda"], s_lru, tail0, mix, w_a + w_b, n_seq=n_seq,
                                seq_len=seq_len, col_x=col_u + 2 * w_b, col_gate=col_u + 2 * w_b + w_c)
    return [mix], (s_gla_new, tail[:, SUBLANES - (CONV_W - 1):], s_lru_new, vg)


def _small_layer_weights(l, w_in_t, w_alpha2, b_alpha, g_onorm, w_spatial, b_spatial, conv_w, conv_b,
                         w_rgate, b_rgate, w_igate, b_igate, lru_lambda):
    d = w_in_t.shape[2]
    w_a, w_b = d // 2, d // 4
    qk_a = (w_a // DV_A) * DK_A
    c_alr = 2 * qk_a + 2 * w_a
    w_alr = jnp.pad(w_in_t[l, c_alr:c_alr + GLA_RANK, :], ((0, LANES - GLA_RANK), (0, 0)))
    w_alpha2_p = jnp.pad(w_alpha2[l], ((0, LANES - GLA_RANK), (0, 0))).astype(BF16)
    b_full = jnp.repeat(b_spatial[l].T, w_b // H_B, axis=1)
    return dict(
        w_alr=w_alr, w_alpha2=w_alpha2_p, b_alpha=b_alpha[l].reshape(1, qk_a),
        g_onorm=g_onorm[l].reshape(1, DV_A), w_spatial=w_spatial[l], b_spatial=b_full,
        conv_w=conv_w[l], conv_b=conv_b[l], w_rgate=w_rgate[l], b_rgate=b_rgate[l],
        w_igate=w_igate[l], b_igate=b_igate[l], lru_lambda=lru_lambda[l])


def kernel(x_prompt, x_sample, state_gla, state_conv, state_lru, ln1, w_in, w_alpha2, b_alpha, g_onorm, w_spatial, b_spatial, conv_w, conv_b, w_rgate, b_rgate, w_igate, b_igate, lru_lambda, w_out, ln2, w_up, w_down, ln_final):
    bp, tp, d = x_prompt.shape
    bs, ts, _ = x_sample.shape
    depth = w_in.shape[0]
    w_a, w_b, w_c = d // 2, d // 4, d // 4
    h_a = w_a // DV_A
    qk_a = h_a * DK_A
    n_a = 2 * qk_a + 2 * w_a
    n_b = 2 * w_b + 2 * w_c
    zero_gla = jnp.zeros((bp, h_a, DK_A, DV_A), F32)
    zero_conv = jnp.zeros((bp, CONV_W - 1, w_c), F32)
    zero_lru = jnp.zeros((bp, w_c), F32)
    xp = x_prompt.reshape(bp * tp, d)
    xs = x_sample.reshape(bs * ts, d)
    w_in_t = jnp.swapaxes(w_in, 1, 2)
    mm = _ws_matmul
    st_p, st_s = [], []
    for l in range(depth):
        w = _small_layer_weights(l, w_in_t, w_alpha2, b_alpha, g_onorm, w_spatial, b_spatial, conv_w, conv_b,
                                 w_rgate, b_rgate, w_igate, b_igate, lru_lambda)
        xn_p, alr_p = _rmsnorm(xp, ln1[l], BF16, w_proj=w["w_alr"])
        xn_s, alr_s = _rmsnorm(xs, ln1[l], BF16, w_proj=w["w_alr"])
        hp_p, hp_s = mm([xn_p], [xn_s], w_in_t, l, n_cols=n_a + n_b, transposed=True,
                        w_row_gap=(n_a, GLA_RANK), bm=1024, name="in_proj")
        mix_p, new_p = _mixers(hp_p, alr_p, zero_gla, zero_conv, zero_lru, w, n_seq=bp, seq_len=tp,
                               want_v=False)
        mix_s, new_s = _mixers(hp_s, alr_s, state_gla[l], state_conv[l], state_lru[l], w, n_seq=bs,
                               seq_len=ts, want_v=True)
        st_p.append(new_p)
        st_s.append(new_s)
        xp, xs, xg_p, xg_s, ssq_p, ssq_s = mm(mix_p, mix_s, w_out, l, n_cols=d, res=(xp, xs),
                                              norm_gain=ln2[l].reshape(1, d), name="out_proj")
        hid_p, hid_s, w_down_bf = mm([xg_p], [xg_s], w_up, l, n_cols=w_up.shape[2], act="relu2",
                                     out_dtype=BF16, side_cast=(w_down, l), bm=1024, name="mlp_up")
        xp, xs = _ktiled_matmul(hid_p, hid_s, w_down_bf, (xp, xs), (ssq_p, ssq_s), d, bm=1024, bn=1024,
                                bk=2048, name="mlp_down")
    y_prompt = _rmsnorm(xp, ln_final, F32).reshape(bp, tp, d)
    y_sample = _rmsnorm(xs, ln_final, F32).reshape(bs, ts, d)
    stack = lambda sts, i: jnp.stack([s[i] for s in sts])
    return (y_prompt, y_sample, stack(st_p, 0), stack(st_p, 1), stack(st_p, 2),
            stack(st_s, 0), stack(st_s, 1), stack(st_s, 2),
            jnp.stack([s[3].reshape(bs, ts, w_b) for s in st_s]))
```
